```python
import math
import jax, jax.numpy as jnp
from jax import lax
import numpy as np

D_MODEL = 4096
BATCH = 4
SEQ = 2048
DEPTH = 2

CTX_LEN = 256
GRID_W = 64
HEAD_DIM = 128
MIX_WIDTH = D_MODEL
N_EVEN = (DEPTH + 1) // 2
N_ODD = DEPTH // 2
N_MOD = 6
FNO_WIDTH = MIX_WIDTH // 2
FNO_GROUPS = FNO_WIDTH // HEAD_DIM
HY_WIDTH = MIX_WIDTH - FNO_WIDTH
HY_ORDER = 2
HY_SHORT_K = 3
HY_FILTER_WIDTH = 64
HY_POS_BANDS = 16
HY_POS_DIM = 1 + 2 * HY_POS_BANDS
HY_DECAY_TARGET = 1e-2
HY_FAST_DECAY = 0.3
HY_SLOW_DECAY = 1.5
HY_FILTER_INIT = 0.05
EVEN_IN = FNO_WIDTH + (HY_ORDER + 1) * HY_WIDTH
NA_WIDTH = MIX_WIDTH // 2
NA_HEADS = NA_WIDTH // HEAD_DIM
NA_WIN_R = 8
NA_WIN_C = 16
MLA_HEADS = (MIX_WIDTH - NA_WIDTH) // HEAD_DIM
MLA_Q_RANK = 1024
MLA_KV_RANK = 512
MLA_NOPE = HEAD_DIM
MLA_ROPE = 64
MLA_V = HEAD_DIM
ODD_IN = 3 * NA_WIDTH + MLA_Q_RANK + MLA_KV_RANK + MLA_ROPE
ROPE_BASE = 10000.0
D_FF = 11008
N_EXPERTS = 8
TOP_K = 2
D_FF_EXPERT = 2048
Q_BLOCK = 128
EPS = 1e-6

kernel_name = "hybrid_fourier_hyena_natten_mla_moe_dit"


def rmsnorm(x, g):
    xf = x.astype(jnp.float32)
    y = xf * lax.rsqrt(jnp.mean(xf * xf, axis=-1, keepdims=True) + EPS)
    return (y * g.astype(jnp.float32)).astype(x.dtype)


def modulate(h, shift, scale):
    return h * (1 + scale[:, None, :]) + shift[:, None, :]


def swiglu(h, w1, w3, w2):
    return (jax.nn.silu(h @ w1) * (h @ w3)) @ w2


def moe_swiglu(h, router_w, router_b, w1, w3, w2):
    b, l, d = h.shape
    t = h.reshape(b * l, d)
    logits = (t @ router_w + router_b).astype(jnp.float32)
    top_v, top_i = lax.top_k(logits, TOP_K)
    gates = jax.nn.softmax(top_v, axis=-1)
    combine = jnp.sum(jax.nn.one_hot(top_i, N_EXPERTS, dtype=jnp.float32) * gates[..., None], axis=1).astype(h.dtype)
    out = jnp.zeros_like(t)
    for e in range(N_EXPERTS):
        out = out + combine[:, e:e + 1] * swiglu(t, w1[e], w3[e], w2[e])
    return out.reshape(b, l, d)


def fourier_mix(u):
    b, l, _ = u.shape
    ug = u.astype(jnp.float32).reshape(b, l, FNO_GROUPS, HEAD_DIM)
    y = jnp.fft.fftn(ug, axes=(1, 3), norm="ortho").real
    return y.reshape(b, l, FNO_WIDTH).astype(u.dtype)


def centred_depthwise_conv(u, w, bias):
    k = w.shape[0]
    y = lax.conv_general_dilated(u, w[:, None, :].astype(u.dtype), window_strides=(1,),
                                 padding=[(k // 2, k - 1 - k // 2)],
                                 dimension_numbers=("NWC", "WIO", "NWC"),
                                 feature_group_count=u.shape[-1])
    return y + bias.astype(y.dtype)


def hyena_filters(l, pe_w, pe_b, w1, b1, w2, b2, w_out, freq):
    f32 = jnp.float32
    t = jnp.linspace(0.0, 1.0, l, dtype=f32)[:, None]
    w = (2.0 * math.pi / l) * jnp.arange(l, dtype=f32)[:, None]
    bands = jnp.linspace(1e-4, HY_POS_BANDS - 1, HY_POS_BANDS, dtype=f32)[None, :]
    z = jnp.concatenate([t, jnp.cos(bands * w), -jnp.sin(bands * w)], axis=-1)
    fr = freq.astype(f32)
    h = jnp.sin(fr * (z @ pe_w.astype(f32) + pe_b.astype(f32)))
    h = jnp.sin(fr * (h @ w1.astype(f32) + b1.astype(f32)))
    h = jnp.sin(fr * (h @ w2.astype(f32) + b2.astype(f32)))
    h = (h @ w_out.astype(f32)).reshape(l, HY_ORDER, 2, HY_WIDTH)
    max_decay = math.log(HY_DECAY_TARGET) / HY_FAST_DECAY
    min_decay = math.log(HY_DECAY_TARGET) / HY_SLOW_DECAY
    deltas = jnp.abs(jnp.linspace(min_decay, max_decay, HY_WIDTH, dtype=f32))
    decay = jnp.exp(-t * deltas)
    return h * decay[:, None, None, :]


def bidir_fft_conv(u, h_fwd, h_bwd, skip):
    l, ch = h_fwd.shape
    k = jnp.concatenate([h_fwd, jnp.zeros((1, ch), h_fwd.dtype), h_bwd[:0:-1]], axis=0)
    uf = jnp.fft.rfft(u.astype(jnp.float32), n=2 * l, axis=1)
    kf = jnp.fft.rfft(k, n=2 * l, axis=0)
    y = jnp.fft.irfft(uf * kf[None], n=2 * l, axis=1)[:, :l]
    return (y + u.astype(jnp.float32) * skip.astype(jnp.float32)).astype(u.dtype)


def hyena_mix(u, conv_w, conv_b, filt_params, skip):
    u = centred_depthwise_conv(u, conv_w, conv_b)
    v, x1, x2 = jnp.split(u, HY_ORDER + 1, axis=-1)
    filt = hyena_filters(u.shape[1], *filt_params)
    z = v
    for n, gate in enumerate((x1, x2)):
        z = gate * bidir_fft_conv(z, filt[:, n, 0], filt[:, n, 1], skip[n])
    return z


def even_mixer(h, w_in, w_out, conv_w, conv_b, filt_params, skip):
    proj = h @ w_in
    y_f = fourier_mix(proj[..., :FNO_WIDTH])
    y_h = hyena_mix(proj[..., FNO_WIDTH:], conv_w, conv_b, filt_params, skip)
    return jnp.concatenate([y_f, y_h], axis=-1) @ w_out


def odd_keys_values(proj, kv_g, w_ukv):
    b, l, _ = proj.shape
    k_na = proj[..., NA_WIDTH:2 * NA_WIDTH].reshape(b, l, NA_HEADS, HEAD_DIM)
    v_na = proj[..., 2 * NA_WIDTH:3 * NA_WIDTH].reshape(b, l, NA_HEADS, HEAD_DIM)
    o = 3 * NA_WIDTH + MLA_Q_RANK
    c_kv = proj[..., o:o + MLA_KV_RANK]
    k_rope = proj[..., o + MLA_KV_RANK:]
    kv = (rmsnorm(c_kv, kv_g) @ w_ukv).reshape(b, l, MLA_HEADS, MLA_NOPE + MLA_V)
    return k_na, v_na, kv[..., :MLA_NOPE], k_rope, kv[..., MLA_NOPE:]


def odd_queries(proj, q_g, w_uq):
    b, l, _ = proj.shape
    q_na = proj[..., :NA_WIDTH].reshape(b, l, NA_HEADS, HEAD_DIM)
    c_q = proj[..., 3 * NA_WIDTH:3 * NA_WIDTH + MLA_Q_RANK]
    q = (rmsnorm(c_q, q_g) @ w_uq).reshape(b, l, MLA_HEADS, MLA_NOPE + MLA_ROPE)
    return q_na, q[..., :MLA_NOPE], q[..., MLA_NOPE:]


def axial_angles(l):
    pos = jnp.arange(l)
    row = (pos // GRID_W).astype(jnp.float32)
    col = (pos % GRID_W).astype(jnp.float32)
    half = MLA_ROPE // 2
    inv = ROPE_BASE ** (-jnp.arange(0, half, 2, dtype=jnp.float32) / half)
    return row[:, None] * inv, col[:, None] * inv


def rotate(x, ang):
    x1, x2 = jnp.split(x, 2, axis=-1)
    cos, sin = jnp.cos(ang).astype(x.dtype), jnp.sin(ang).astype(x.dtype)
    return jnp.concatenate([x1 * cos - x2 * sin, x1 * sin + x2 * cos], axis=-1)


def axial_rope(x, ang_r, ang_c):
    xr, xc = jnp.split(x, 2, axis=-1)
    return jnp.concatenate([rotate(xr, ang_r), rotate(xc, ang_c)], axis=-1)


def dense_attend(q, k, v):
    s = jnp.einsum("bqhd,bkhd->bhqk", q, k).astype(jnp.float32) * HEAD_DIM ** -0.5
    p = jax.nn.softmax(s, axis=-1).astype(v.dtype)
    return jnp.einsum("bhqk,bkhd->bqhd", p, v)


def mla_attend(q_nope, q_rope, k_nope, k_rope, v):
    s = (jnp.einsum("bqhd,bkhd->bhqk", q_nope, k_nope)
         + jnp.einsum("bqhd,bkd->bhqk", q_rope, k_rope)).astype(jnp.float32)
    p = jax.nn.softmax(s * (MLA_NOPE + MLA_ROPE) ** -0.5, axis=-1).astype(v.dtype)
    return jnp.einsum("bhqk,bkhd->bqhd", p, v)


def block_sweep(fn, qs, *kv):
    b, l = qs[0].shape[:2]
    nb = l // Q_BLOCK
    blocks = tuple(jnp.moveaxis(q.reshape(b, nb, Q_BLOCK, *q.shape[2:]), 1, 0) for q in qs)
    out = lax.map(lambda qb: fn(*qb, *kv), blocks)
    return jnp.moveaxis(out, 0, 1).reshape(b, l, *out.shape[3:])


def neighbourhood_attend(q, k, v, k_ctx, v_ctx, rpb):
    b, l, h, d = q.shape
    rows = l // GRID_W
    kr, kc = min(NA_WIN_R, rows), NA_WIN_C
    cols = jnp.arange(GRID_W)
    row_start = jnp.clip(jnp.arange(rows) - kr // 2, 0, rows - kr)
    col_idx = jnp.clip(cols - kc // 2, 0, GRID_W - kc)[:, None] + jnp.arange(kc)
    dc = col_idx - cols[:, None] + (NA_WIN_C - 1)
    kg = k.reshape(b, rows, GRID_W, h, d)
    vg = v.reshape(b, rows, GRID_W, h, d)
    qg = jnp.moveaxis(q.reshape(b, rows, GRID_W, h, d), 1, 0)
    scale = d ** -0.5

    def row_block(args):
        q_row, ri = args
        key_rows = row_start[ri] + jnp.arange(kr)
        k_win = jnp.take(kg, key_rows, axis=1)[:, :, col_idx]
        v_win = jnp.take(vg, key_rows, axis=1)[:, :, col_idx]
        dr = key_rows - ri + (NA_WIN_R - 1)
        bias = jnp.transpose(rpb[:, dr][:, :, dc], (0, 2, 1, 3)).astype(jnp.float32)
        s_loc = jnp.einsum("bwhd,brwkhd->bhwrk", q_row, k_win).astype(jnp.float32) * scale + bias
        s_ctx = jnp.einsum("bwhd,bchd->bhwc", q_row, k_ctx).astype(jnp.float32) * scale
        s = jnp.concatenate([s_loc.reshape(b, h, GRID_W, kr * kc), s_ctx], axis=-1)
        p = jax.nn.softmax(s, axis=-1).astype(v.dtype)
        p_loc = p[..., :kr * kc].reshape(b, h, GRID_W, kr, kc)
        return (jnp.einsum("bhwrk,brwkhd->bwhd", p_loc, v_win)
                + jnp.einsum("bhwc,bchd->bwhd", p[..., kr * kc:], v_ctx))

    out = lax.map(row_block, (qg, jnp.arange(rows)))
    return jnp.moveaxis(out, 0, 1).reshape(b, l, h * d)


def odd_mixer(h_lat, h_ctx, with_ctx, w_in, w_out, rpb, q_g, w_uq, kv_g, w_ukv):
    b, l, _ = h_lat.shape
    p_lat = h_lat @ w_in
    p_ctx = h_ctx @ w_in
    kna_l, vna_l, kn_l, kr_l, v_l = odd_keys_values(p_lat, kv_g, w_ukv)
    kna_c, vna_c, kn_c, kr_c, v_c = odd_keys_values(p_ctx, kv_g, w_ukv)
    qna_l, qn_l, qr_l = odd_queries(p_lat, q_g, w_uq)
    ang_r, ang_c = axial_angles(l)
    qr_l = axial_rope(qr_l, ang_r[:, None], ang_c[:, None])
    kr_l = axial_rope(kr_l, ang_r, ang_c)
    na_l = neighbourhood_attend(qna_l, kna_l, vna_l, kna_c, vna_c, rpb)
    kn = jnp.concatenate([kn_l, kn_c], axis=1)
    kr = jnp.concatenate([kr_l, kr_c], axis=1)
    vv = jnp.concatenate([v_l, v_c], axis=1)
    mla_l = block_sweep(mla_attend, (qn_l, qr_l), kn, kr, vv).reshape(b, l, MLA_HEADS * MLA_V)
    y_lat = jnp.concatenate([na_l, mla_l], axis=-1) @ w_out
    if not with_ctx:
        return y_lat, None
    bc, lc, _ = h_ctx.shape
    qna_c, qn_c, qr_c = odd_queries(p_ctx, q_g, w_uq)
    na_c = dense_attend(qna_c, kna_c, vna_c).reshape(bc, lc, NA_WIDTH)
    mla_c = mla_attend(qn_c, qr_c, kn_c, kr_c, v_c).reshape(bc, lc, MLA_HEADS * MLA_V)
    y_ctx = jnp.concatenate([na_c, mla_c], axis=-1) @ w_out
    return y_lat, y_ctx


def setup_inputs(seed: int = 0) -> dict:
    key = jax.random.key(seed)
    keys = iter(jax.random.split(key, 48))
    f32 = jnp.float32

    def nrm(shape, scale):
        return jax.random.normal(next(keys), shape, f32) * scale

    def gain(shape):
        return 1.0 + 0.02 * jax.random.normal(next(keys), shape, f32)

    D, NE, NO = D_MODEL, N_EVEN, N_ODD
    return {
        "x": nrm((BATCH, SEQ, D), 1.0),
        "c": nrm((BATCH, D), 1.0),
        "ctx": nrm((BATCH, CTX_LEN, D), 1.0),
        "c_ctx": nrm((D,), 1.0),
        "mod_w": nrm((DEPTH, D, N_MOD * D), D ** -0.5),
        "mod_b": nrm((DEPTH, N_MOD * D), 0.02),
        "norm_mix_g": gain((DEPTH, D)),
        "norm_ffn_g": gain((DEPTH, D)),
        "ev_w_in": nrm((NE, D, EVEN_IN), D ** -0.5),
        "ev_w_out": nrm((NE, MIX_WIDTH, D), MIX_WIDTH ** -0.5),
        "hy_conv_w": nrm((NE, HY_SHORT_K, (HY_ORDER + 1) * HY_WIDTH), HY_SHORT_K ** -0.5),
        "hy_conv_b": nrm((NE, (HY_ORDER + 1) * HY_WIDTH), 0.02),
        "hy_pe_w": nrm((NE, HY_POS_DIM, HY_FILTER_WIDTH), HY_POS_DIM ** -0.5),
        "hy_pe_b": nrm((NE, HY_FILTER_WIDTH), 0.1),
        "hy_w1": nrm((NE, HY_FILTER_WIDTH, HY_FILTER_WIDTH), HY_FILTER_WIDTH ** -0.5),
        "hy_b1": nrm((NE, HY_FILTER_WIDTH), 0.1),
        "hy_w2": nrm((NE, HY_FILTER_WIDTH, HY_FILTER_WIDTH), HY_FILTER_WIDTH ** -0.5),
        "hy_b2": nrm((NE, HY_FILTER_WIDTH), 0.1),
        "hy_w_out": nrm((NE, HY_FILTER_WIDTH, HY_ORDER * 2 * HY_WIDTH), HY_FILTER_INIT * HY_FILTER_WIDTH ** -0.5),
        "hy_freq": gain((NE, HY_FILTER_WIDTH)),
        "hy_skip": nrm((NE, HY_ORDER, HY_WIDTH), 0.5),
        "ffn_w1": nrm((NE, D, D_FF), D ** -0.5),
        "ffn_w3": nrm((NE, D, D_FF), D ** -0.5),
        "ffn_w2": nrm((NE, D_FF, D), D_FF ** -0.5),
        "od_w_in": nrm((NO, D, ODD_IN), D ** -0.5),
        "od_w_out": nrm((NO, MIX_WIDTH, D), MIX_WIDTH ** -0.5),
        "na_rpb": nrm((NO, NA_HEADS, 2 * NA_WIN_R - 1, 2 * NA_WIN_C - 1), 0.02),
        "mla_q_g": gain((NO, MLA_Q_RANK)),
        "mla_w_uq": nrm((NO, MLA_Q_RANK, MLA_HEADS * (MLA_NOPE + MLA_ROPE)), MLA_Q_RANK ** -0.5),
        "mla_kv_g": gain((NO, MLA_KV_RANK)),
        "mla_w_ukv": nrm((NO, MLA_KV_RANK, MLA_HEADS * (MLA_NOPE + MLA_V)), MLA_KV_RANK ** -0.5),
        "moe_router_w": nrm((NO, D, N_EXPERTS), D ** -0.5),
        "moe_router_b": nrm((NO, N_EXPERTS), 0.01),
        "moe_w1": nrm((NO, N_EXPERTS, D, D_FF_EXPERT), D ** -0.5),
        "moe_w3": nrm((NO, N_EXPERTS, D, D_FF_EXPERT), D ** -0.5),
        "moe_w2": nrm((NO, N_EXPERTS, D_FF_EXPERT, D), D_FF_EXPERT ** -0.5),
        "final_g": gain((D,)),
    }


def reference(x, c, ctx, c_ctx, mod_w, mod_b, norm_mix_g, norm_ffn_g,
              ev_w_in, ev_w_out, hy_conv_w, hy_conv_b, hy_pe_w, hy_pe_b, hy_w1, hy_b1,
              hy_w2, hy_b2, hy_w_out, hy_freq, hy_skip, ffn_w1, ffn_w3, ffn_w2,
              od_w_in, od_w_out, na_rpb, mla_q_g, mla_w_uq, mla_kv_g, mla_w_ukv,
              moe_router_w, moe_router_b, moe_w1, moe_w3, moe_w2, final_g):
    xl, xc = x, ctx
    for i in range(DEPTH):
        last = i == DEPTH - 1
        j = i // 2
        m_lat = jax.nn.silu(c) @ mod_w[i] + mod_b[i]
        m_ctx = jax.nn.silu(c_ctx)[None, :] @ mod_w[i] + mod_b[i]
        sh1, sc1, g1, sh2, sc2, g2 = jnp.split(m_lat, N_MOD, axis=-1)
        csh1, csc1, cg1, csh2, csc2, cg2 = jnp.split(m_ctx, N_MOD, axis=-1)
        h_lat = modulate(rmsnorm(xl, norm_mix_g[i]), sh1, sc1)
        h_ctx = modulate(rmsnorm(xc, norm_mix_g[i]), csh1, csc1)
        if i % 2 == 0:
            filt = (hy_pe_w[j], hy_pe_b[j], hy_w1[j], hy_b1[j], hy_w2[j], hy_b2[j], hy_w_out[j], hy_freq[j])
            y_lat = even_mixer(h_lat, ev_w_in[j], ev_w_out[j], hy_conv_w[j], hy_conv_b[j], filt, hy_skip[j])
            y_ctx = None if last else even_mixer(h_ctx, ev_w_in[j], ev_w_out[j], hy_conv_w[j], hy_conv_b[j], filt, hy_skip[j])
        else:
            y_lat, y_ctx = odd_mixer(h_lat, h_ctx, not last, od_w_in[j], od_w_out[j], na_rpb[j],
                                     mla_q_g[j], mla_w_uq[j], mla_kv_g[j], mla_w_ukv[j])
        xl = xl + g1[:, None, :] * y_lat
        h2 = modulate(rmsnorm(xl, norm_ffn_g[i]), sh2, sc2)
        if i % 2 == 0:
            xl = xl + g2[:, None, :] * swiglu(h2, ffn_w1[j], ffn_w3[j], ffn_w2[j])
        else:
            xl = xl + g2[:, None, :] * moe_swiglu(h2, moe_router_w[j], moe_router_b[j], moe_w1[j], moe_w3[j], moe_w2[j])
        if not last:
            xc = xc + cg1[:, None, :] * y_ctx
            hc2 = modulate(rmsnorm(xc, norm_ffn_g[i]), csh2, csc2)
            if i % 2 == 0:
                xc = xc + cg2[:, None, :] * swiglu(hc2, ffn_w1[j], ffn_w3[j], ffn_w2[j])
            else:
                xc = xc + cg2[:, None, :] * moe_swiglu(hc2, moe_router_w[j], moe_router_b[j], moe_w1[j], moe_w3[j], moe_w2[j])
    return rmsnorm(xl, final_g)
```

```python
import functools
import math

import jax
import jax.numpy as jnp
import numpy as np
from jax import lax
from jax.experimental import pallas as pl
from jax.experimental.pallas import tpu as pltpu

F32 = jnp.float32
BF16 = jnp.bfloat16

GRID_W = 64
HEAD_DIM = 128
N_MOD = 6
HY_ORDER = 2
HY_POS_BANDS = 16
HY_DECAY_TARGET = 1e-2
HY_FAST_DECAY = 0.3
HY_SLOW_DECAY = 1.5
NA_WIN_R = 8
NA_WIN_C = 16
MLA_ROPE = 64
ROPE_BASE = 10000.0
TOP_K = 2
EPS = 1e-6

LANES = 128
MOD_ROWS = 8
V7X_VMEM_LIMIT = 58 * 2 ** 20
NEG_BIG = -1e30


def _tile(n, pref):
    if n <= pref:
        return n
    t = (pref // LANES) * LANES
    while t >= LANES:
        if n % t == 0:
            return t
        t -= LANES
    raise ValueError(f"no lane-aligned tile for {n} <= {pref}")


def _params(n_grid):
    return pltpu.CompilerParams(dimension_semantics=("arbitrary",) * n_grid,
                                vmem_limit_bytes=V7X_VMEM_LIMIT)


def _bf(a):
    return a if a.dtype == BF16 else a.astype(BF16)


def _dot(a, b):
    return jnp.dot(_bf(a), _bf(b), preferred_element_type=F32)


def _dot_nt(a, b):
    return lax.dot_general(_bf(a), _bf(b), (((1,), (1,)), ((), ())), preferred_element_type=F32)


def _split(a):
    hi = a.astype(BF16)
    lo = (a - hi.astype(F32)).astype(BF16)
    return hi, lo


def _dot3(a, b):
    ah, al = _split(a)
    bh, bl = _split(b)
    return (jnp.dot(ah, bh, preferred_element_type=F32) + jnp.dot(ah, bl, preferred_element_type=F32)
            + jnp.dot(al, bh, preferred_element_type=F32))


def _silu(a):
    return a * (1.0 / (1.0 + jnp.exp(-a)))


def _mm_body(*refs, n_x, n_w, n_e, n_o, groups, nk, epilogue, n_grid):
    xs = refs[:n_x]
    ws = refs[n_x:n_x + n_w]
    es = refs[n_x + n_w:n_x + n_w + n_e]
    outs = refs[n_x + n_w + n_e:n_x + n_w + n_e + n_o]
    accs = refs[n_x + n_w + n_e + n_o:]
    pids = [pl.program_id(a) for a in range(n_grid)]
    xv = [_bf(x[...]) for x in xs]
    wv = [_bf(w[...]) for w in ws]
    parts = []
    for group in groups:
        p = None
        for xi, wi in group:
            d = jnp.dot(xv[xi], wv[wi], preferred_element_type=F32)
            p = d if p is None else p + d
        parts.append(p)
    if nk == 1:
        epilogue(parts, es, outs, pids)
        return
    k = pids[-1]

    @pl.when(k == 0)
    def _():
        for a, p in zip(accs, parts):
            a[...] = p

    @pl.when(k > 0)
    def _():
        for a, p in zip(accs, parts):
            a[...] += p

    @pl.when(k == nk - 1)
    def _():
        epilogue([a[...] for a in accs], es, outs, pids)


def _mm(name, grid, xs, x_specs, ws, w_specs, es, e_specs, out_shapes, out_specs, groups, epilogue,
        nk=1, acc_block=None, aliases=None):
    body = functools.partial(_mm_body, n_x=len(xs), n_w=len(ws), n_e=len(es), n_o=len(out_shapes),
                             groups=groups, nk=nk, epilogue=epilogue, n_grid=len(grid))
    scratch = [pltpu.VMEM(acc_block, F32) for _ in groups] if nk > 1 else []
    res = pl.pallas_call(
        body, grid=grid, in_specs=list(x_specs) + list(w_specs) + list(e_specs),
        out_specs=list(out_specs), out_shape=list(out_shapes), scratch_shapes=scratch,
        input_output_aliases=aliases or {}, compiler_params=_params(len(grid)), name=name,
    )(*xs, *ws, *es)
    return res


def _store0(parts, es, outs, pids):
    outs[0][...] = parts[0].astype(outs[0].dtype)


def _linear(name, x, w, out_dtype, *, n_cols=None, tm_pref=1024, tn_pref=512):
    m, k = x.shape
    n = n_cols or w.shape[1]
    tm, tn = _tile(m, tm_pref), _tile(n, tn_pref)
    return _mm(name, (m // tm, n // tn),
               [x], [pl.BlockSpec((tm, k), lambda i, j: (i, 0))],
               [w], [pl.BlockSpec((k, tn), lambda i, j: (0, j))],
               [], [], [jax.ShapeDtypeStruct((m, n), out_dtype)],
               [pl.BlockSpec((tm, tn), lambda i, j: (i, j))], [[(0, 0)]], _store0)[0]


def _mod_body(c_ref, w_ref, b_ref, o_ref):
    o_ref[...] = _dot(_silu(c_ref[...]), w_ref[...]) + b_ref[...]


def _modulation(c_all, mod_w, mod_b):
    depth, d, n = mod_w.shape
    tn = _tile(n, 512)
    return pl.pallas_call(
        _mod_body, grid=(depth, n // tn),
        in_specs=[pl.BlockSpec((MOD_ROWS, d), lambda l, j: (0, 0)),
                  pl.BlockSpec((None, d, tn), lambda l, j: (l, 0, j)),
                  pl.BlockSpec((None, 1, tn), lambda l, j: (l, 0, j))],
        out_specs=pl.BlockSpec((None, MOD_ROWS, tn), lambda l, j: (l, 0, j)),
        out_shape=jax.ShapeDtypeStruct((depth, MOD_ROWS, n), F32),
        compiler_params=_params(2), name="modulation",
    )(c_all, mod_w, mod_b.reshape(depth, 1, n))


def _norm_body(*refs, modulate, router, n_experts):
    it = iter(refs)
    x_ref, g_ref = next(it), next(it)
    x = x_ref[...]
    y = x * lax.rsqrt(jnp.mean(x * x, axis=-1, keepdims=True) + EPS) * g_ref[...]
    if modulate:
        sh_ref, sc_ref = next(it), next(it)
        y = y * (1.0 + sc_ref[...]) + sh_ref[...]
    if router:
        rw_ref, rb_ref = next(it), next(it)
    o_ref = next(it)
    o_ref[...] = y.astype(o_ref.dtype)
    if router:
        comb_ref = next(it)
        logits = _dot3(y, rw_ref[...]) + rb_ref[...]
        lane = lax.broadcasted_iota(jnp.int32, logits.shape, 1).astype(F32)
        neg = jnp.float32(-jnp.inf)
        logits = jnp.where(lane < n_experts, logits, neg)
        m1 = jnp.max(logits, axis=-1, keepdims=True)
        i1 = jnp.min(jnp.where(logits == m1, lane, float(LANES)), axis=-1, keepdims=True)
        rest = jnp.where(lane == i1, neg, logits)
        m2 = jnp.max(rest, axis=-1, keepdims=True)
        i2 = jnp.min(jnp.where(rest == m2, lane, float(LANES)), axis=-1, keepdims=True)
        e2 = jnp.exp(m2 - m1)
        g1 = 1.0 / (1.0 + e2)
        g2 = e2 / (1.0 + e2)
        comb_ref[...] = jnp.where(lane == i1, g1, 0.0) + jnp.where(lane == i2, g2, 0.0)


def _rmsnorm(name, x, gain, out_dtype, *, rows=None, width=None, col_block=0, mod=None, seg_of_tile=None,
             router=None, tr=256):
    rows = rows or x.shape[0]
    width = width or x.shape[1]
    tr = _tile(rows, tr)
    in_specs = [pl.BlockSpec((tr, width), lambda i: (i, col_block)),
                pl.BlockSpec((1, width), lambda i: (0, 0))]
    args = [x, gain.reshape(1, width)]
    if mod is not None:
        seg = seg_of_tile(tr)
        for t in mod:
            in_specs.append(pl.BlockSpec((None, 1, width), lambda i: (seg(i), 0, 0)))
            args.append(t)
    out_shape = [jax.ShapeDtypeStruct((rows, width), out_dtype)]
    out_specs = [pl.BlockSpec((tr, width), lambda i: (i, 0))]
    n_experts = 0
    if router is not None:
        rw, rb = router
        n_experts = rw.shape[1]
        rw_p = jnp.pad(rw, ((0, 0), (0, LANES - n_experts)))
        rb_p = jnp.pad(rb, (0, LANES - n_experts)).reshape(1, LANES)
        in_specs += [pl.BlockSpec((width, LANES), lambda i: (0, 0)), pl.BlockSpec((1, LANES), lambda i: (0, 0))]
        args += [rw_p, rb_p]
        out_shape.append(jax.ShapeDtypeStruct((rows, LANES), F32))
        out_specs.append(pl.BlockSpec((tr, LANES), lambda i: (i, 0)))
    body = functools.partial(_norm_body, modulate=mod is not None, router=router is not None, n_experts=n_experts)
    return pl.pallas_call(body, grid=(rows // tr,), in_specs=in_specs, out_specs=out_specs, out_shape=out_shape,
                          compiler_params=_params(1), name=name)(*args)


def _group_dft_body(p_ref, cs_ref, o_ref, *, groups):
    cs = cs_ref[...]
    fw = groups * HEAD_DIM
    for g in range(groups):
        r = _dot(p_ref[:, g * HEAD_DIM:(g + 1) * HEAD_DIM], cs)
        o_ref[:, g * HEAD_DIM:(g + 1) * HEAD_DIM] = r[:, :HEAD_DIM].astype(o_ref.dtype)
        o_ref[:, fw + g * HEAD_DIM:fw + (g + 1) * HEAD_DIM] = r[:, HEAD_DIM:].astype(o_ref.dtype)


def _group_dft(proj, fw):
    rows = proj.shape[0]
    k = np.arange(HEAD_DIM)
    ang = (2.0 * np.pi / HEAD_DIM) * ((k[:, None] * k[None, :]) % HEAD_DIM)
    cs = jnp.asarray(np.concatenate([np.cos(ang), np.sin(ang)], axis=1), dtype=BF16)
    tm = _tile(rows, 512)
    body = functools.partial(_group_dft_body, groups=fw // HEAD_DIM)
    return pl.pallas_call(
        body, grid=(rows // tm,),
        in_specs=[pl.BlockSpec((tm, fw), lambda i: (i, 0)),
                  pl.BlockSpec((HEAD_DIM, 2 * HEAD_DIM), lambda i: (0, 0))],
        out_specs=pl.BlockSpec((tm, 2 * fw), lambda i: (i, 0)),
        out_shape=jax.ShapeDtypeStruct((rows, 2 * fw), BF16),
        compiler_params=_params(1), name="group_dft",
    )(proj, cs)


def _angles(num, den):
    return (num % den).astype(F32) * (2.0 * math.pi / den)


def _position_dft(name, ab, row0, nb, length, fw):
    f = jnp.arange(length, dtype=jnp.int32)
    ang = _angles(f[:, None] * f[None, :], length)
    norm = 1.0 / math.sqrt(length * HEAD_DIM)
    cs = (jnp.concatenate([jnp.cos(ang), -jnp.sin(ang)], axis=1) * norm).astype(BF16)
    tm, tn = _tile(length, 1024), _tile(fw, 512)
    nm, nn = length // tm, fw // tn
    rb0 = row0 // length
    return _mm(name, (nb, nm, nn, 2),
               [cs], [pl.BlockSpec((tm, length), lambda b, i, j, k: (i, k))],
               [ab], [pl.BlockSpec((length, tn), lambda b, i, j, k: (rb0 + b, k * nn + j))],
               [], [], [jax.ShapeDtypeStruct((nb * length, fw), BF16)],
               [pl.BlockSpec((tm, tn), lambda b, i, j, k: (b * nm + i, j))],
               [[(0, 0)]], _store0, nk=2, acc_block=(tm, tn))[0]


def _short_conv_body(p_ref, w_ref, b_ref, o_ref):
    u = p_ref[...]
    n = u.shape[0]
    row = lax.broadcasted_iota(jnp.int32, u.shape, 0)
    prev = jnp.where(row == 0, 0.0, pltpu.roll(u, 1, 0))
    nxt = jnp.where(row == n - 1, 0.0, pltpu.roll(u, n - 1, 0))
    w = w_ref[...]
    o_ref[...] = w[0:1] * prev + w[1:2] * u + w[2:3] * nxt + b_ref[...]


def _short_conv(name, proj, col0, width, row0, nb, length, conv_w, conv_b):
    tc = _tile(math.gcd(col0, width), 512)
    cb0, rb0 = col0 // tc, row0 // length
    return pl.pallas_call(
        _short_conv_body, grid=(nb, width // tc),
        in_specs=[pl.BlockSpec((length, tc), lambda b, j: (rb0 + b, cb0 + j)),
                  pl.BlockSpec((3, tc), lambda b, j: (0, j)),
                  pl.BlockSpec((1, tc), lambda b, j: (0, j))],
        out_specs=pl.BlockSpec((length, tc), lambda b, j: (b, j)),
        out_shape=jax.ShapeDtypeStruct((nb * length, width), F32),
        compiler_params=_params(2), name=name,
    )(proj, conv_w, conv_b.reshape(1, width))


def _filter_mlp_body(z_ref, pw_ref, pb_ref, w1_ref, b1_ref, w2_ref, b2_ref, fr_ref, o_ref):
    fr = fr_ref[...]
    h = jnp.sin(fr * (_dot3(z_ref[...], pw_ref[...]) + pb_ref[...]))
    h = jnp.sin(fr * (_dot3(h, w1_ref[...]) + b1_ref[...]))
    o_ref[...] = jnp.sin(fr * (_dot3(h, w2_ref[...]) + b2_ref[...]))


def _filter_out_body(h_ref, wf_ref, wb_ref, d_ref, o_ref):
    h = h_ref[...]
    d = d_ref[...]
    fwd = _dot3(h, wf_ref[...]) * d
    bwd = _dot3(h, wb_ref[...]) * d
    row = lax.broadcasted_iota(jnp.int32, bwd.shape, 0)
    bwd = jnp.where(row == 0, 0.0, bwd)
    o_ref[0] = fwd + bwd
    o_ref[1] = fwd - bwd


def _hyena_filter_sums(length, hw, pe_w, pe_b, w1, b1, w2, b2, w_out, freq):
    t = jnp.linspace(0.0, 1.0, length, dtype=F32)[:, None]
    w = (2.0 * math.pi / length) * jnp.arange(length, dtype=F32)[:, None]
    bands = jnp.linspace(1e-4, HY_POS_BANDS - 1, HY_POS_BANDS, dtype=F32)[None, :]
    z = jnp.concatenate([t, jnp.cos(bands * w), -jnp.sin(bands * w)], axis=-1)
    pos_dim, fwid = pe_w.shape
    z = jnp.pad(z, ((0, 0), (0, LANES - pos_dim)))
    pe_w = jnp.pad(pe_w, ((0, LANES - pos_dim), (0, 0)))
    row = lambda v: v.reshape(1, fwid)
    full = lambda a: pl.BlockSpec(a.shape, lambda i: (0,) * a.ndim)
    args = [z, pe_w, row(pe_b), w1, row(b1), w2, row(b2), row(freq)]
    h = pl.pallas_call(
        _filter_mlp_body, grid=(1,), in_specs=[full(a) for a in args],
        out_specs=pl.BlockSpec((length, fwid), lambda i: (0, 0)),
        out_shape=jax.ShapeDtypeStruct((length, fwid), F32),
        compiler_params=_params(1), name="hyena_filter_mlp",
    )(*args)
    max_decay = math.log(HY_DECAY_TARGET) / HY_FAST_DECAY
    min_decay = math.log(HY_DECAY_TARGET) / HY_SLOW_DECAY
    deltas = jnp.abs(jnp.linspace(min_decay, max_decay, hw, dtype=F32))
    decay = jnp.exp(-t * deltas)
    tc = _tile(hw, 256)
    nc = hw // tc
    return pl.pallas_call(
        _filter_out_body, grid=(HY_ORDER, nc),
        in_specs=[pl.BlockSpec((length, fwid), lambda n, j: (0, 0)),
                  pl.BlockSpec((fwid, tc), lambda n, j: (0, (2 * n) * nc + j)),
                  pl.BlockSpec((fwid, tc), lambda n, j: (0, (2 * n + 1) * nc + j)),
                  pl.BlockSpec((length, tc), lambda n, j: (0, j))],
        out_specs=pl.BlockSpec((2, length, tc), lambda n, j: (0, 0, n * nc + j)),
        out_shape=jax.ShapeDtypeStruct((2, length, HY_ORDER * hw), F32),
        compiler_params=_params(2), name="hyena_filter_out",
    )(h, w_out, w_out, decay)


def _odd_dft_matrix(length):
    f = jnp.arange(length, dtype=jnp.int32)
    ang = _angles((2 * f[:, None] + 1) * f[None, :], 4 * length)
    return jnp.concatenate([jnp.cos(ang), jnp.sin(ang)], axis=0)


def _hyena_spectrum_epilogue(parts, es, outs, pids):
    ur, us = parts
    kr, ks = es[0][0], es[0][1]
    outs[0][0] = (ur * kr - us * ks).astype(outs[0].dtype)
    outs[0][1] = (ur * ks + us * kr).astype(outs[0].dtype)


def _hyena_gate_epilogue(parts, es, outs, pids):
    gate_ref, z_ref, skip_ref = es
    outs[0][...] = (gate_ref[...] * (parts[0] + z_ref[...] * skip_ref[...])).astype(outs[0].dtype)


def _hyena_stream(tag, uc, nb, length, hw, filt, skip):
    fo = _odd_dft_matrix(length)
    fo_b = fo.astype(BF16)
    go_b = (fo.T * (1.0 / length)).astype(BF16)
    tm, tn = _tile(length, 1024), _tile(hw, 512)
    nm, nn = length // tm, hw // tn
    kspec = _mm(f"hyena_kspec_{tag}", (2 * nm, HY_ORDER * nn),
                [fo_b], [pl.BlockSpec((tm, length), lambda i, j: (i, 0))],
                [filt.reshape(2 * length, HY_ORDER * hw)],
                [pl.BlockSpec((length, tn), lambda i, j: (i // nm, j))],
                [], [], [jax.ShapeDtypeStruct((2 * length, HY_ORDER * hw), F32)],
                [pl.BlockSpec((tm, tn), lambda i, j: (i, j))], [[(0, 0)]], _store0)[0]
    kspec = kspec.reshape(2, length, HY_ORDER * hw)
    z = uc
    for n in range(HY_ORDER):
        z = _hyena_order(tag, n, uc, z, nb, length, hw, fo_b, go_b, kspec, skip, (tm, tn, nm, nn))
    return z


def _hyena_order(tag, n, uc, z, nb, length, hw, fo_b, go_b, kspec, skip, tiles):
    tm, tn, nm, nn = tiles
    gate_cb = (n + 1) * nn
    spec = _mm(f"hyena_fwd{n}_{tag}", (nm, nb, nn),
               [fo_b, fo_b], [pl.BlockSpec((tm, length), lambda i, b, j: (i, 0)),
                              pl.BlockSpec((tm, length), lambda i, b, j: (nm + i, 0))],
               [z], [pl.BlockSpec((length, tn), lambda i, b, j: (b, j))],
               [kspec], [pl.BlockSpec((2, tm, tn), lambda i, b, j: (0, i, n * nn + j))],
               [jax.ShapeDtypeStruct((nb, 2, length, hw), BF16)],
               [pl.BlockSpec((None, 2, tm, tn), lambda i, b, j: (b, 0, i, j))],
               [[(0, 0)], [(1, 0)]], _hyena_spectrum_epilogue)[0]
    last = n == HY_ORDER - 1
    return _mm(f"hyena_inv{n}_{tag}", (nb, nm, nn),
               [go_b], [pl.BlockSpec((tm, 2 * length), lambda b, i, j: (i, 0))],
               [spec.reshape(nb * 2 * length, hw)],
               [pl.BlockSpec((2 * length, tn), lambda b, i, j: (b, j))],
               [uc, z, skip.reshape(HY_ORDER, 1, hw)],
               [pl.BlockSpec((tm, tn), lambda b, i, j: (b * nm + i, gate_cb + j)),
                pl.BlockSpec((tm, tn), lambda b, i, j: (b * nm + i, j)),
                pl.BlockSpec((None, 1, tn), lambda b, i, j: (n, 0, j))],
               [jax.ShapeDtypeStruct((nb * length, hw), BF16 if last else F32)],
               [pl.BlockSpec((tm, tn), lambda b, i, j: (b * nm + i, j))],
               [[(0, 0)]], _hyena_gate_epilogue)[0]


def _resgate_epilogue(parts, es, outs, pids):
    res_ref, gate_ref = es
    outs[0][...] = res_ref[...] + gate_ref[...] * parts[0]


def _mixer_out(name, x_res, row0, rows, za, zb, w_out, gate, seg_of_tile):
    d = x_res.shape[1]
    half = za.shape[1]
    tm, tn = _tile(rows, 1024), _tile(d, 512)
    rb0 = row0 // tm
    seg = seg_of_tile(tm)
    return _mm(name, (rows // tm, d // tn),
               [za, zb], [pl.BlockSpec((tm, half), lambda i, j: (i, 0))] * 2,
               [w_out, w_out], [pl.BlockSpec((half, tn), lambda i, j: (0, j)),
                                pl.BlockSpec((half, tn), lambda i, j: (1, j))],
               [x_res, gate], [pl.BlockSpec((tm, tn), lambda i, j: (rb0 + i, j)),
                               pl.BlockSpec((None, 1, tn), lambda i, j: (seg(rb0 + i), 0, j))],
               [jax.ShapeDtypeStruct(x_res.shape, F32)],
               [pl.BlockSpec((tm, tn), lambda i, j: (rb0 + i, j))],
               [[(0, 0), (1, 1)]], _resgate_epilogue, aliases={4: 0})[0]


def _swiglu_epilogue(parts, es, outs, pids):
    outs[0][...] = (_silu(parts[0]) * parts[1]).astype(outs[0].dtype)


def _ffn_up(h, w1, w3):
    m, k = h.shape
    n = w1.shape[1]
    tm, tn = _tile(m, 1024), _tile(n, 256)
    return _mm("ffn_up", (m // tm, n // tn),
               [h], [pl.BlockSpec((tm, k), lambda i, j: (i, 0))],
               [w1, w3], [pl.BlockSpec((k, tn), lambda i, j: (0, j))] * 2,
               [], [], [jax.ShapeDtypeStruct((m, n), BF16)],
               [pl.BlockSpec((tm, tn), lambda i, j: (i, j))],
               [[(0, 0)], [(0, 1)]], _swiglu_epilogue)[0]


def _down_residual(name, x_res, rows, u, w2, gate, seg_of_tile, tk_pref, tn_pref):
    d = x_res.shape[1]
    kdim = u.shape[1]
    tm, tn, tk = _tile(rows, 1024), _tile(d, tn_pref), _tile(kdim, tk_pref)
    nk = kdim // tk
    seg = seg_of_tile(tm)
    return _mm(name, (rows // tm, d // tn, nk),
               [u], [pl.BlockSpec((tm, tk), lambda i, j, k: (i, k))],
               [w2], [pl.BlockSpec((tk, tn), lambda i, j, k: (k, j))],
               [x_res, gate], [pl.BlockSpec((tm, tn), lambda i, j, k: (i, j)),
                               pl.BlockSpec((None, 1, tn), lambda i, j, k: (seg(i), 0, j))],
               [jax.ShapeDtypeStruct(x_res.shape, F32)],
               [pl.BlockSpec((tm, tn), lambda i, j, k: (i, j))],
               [[(0, 0)]], _resgate_epilogue, nk=nk, acc_block=(tm, tn), aliases={2: 0})[0]


def _moe_up_epilogue(parts, es, outs, pids):
    comb = es[0][...]
    lane = lax.broadcasted_iota(jnp.int32, comb.shape, 1)
    gate = jnp.sum(jnp.where(lane == pids[1], comb, 0.0), axis=-1, keepdims=True)
    outs[0][...] = (gate * (_silu(parts[0]) * parts[1])).astype(outs[0].dtype)


def _moe_up(h, rows, comb, w1, w3):
    ne, k, f = w1.shape
    tm, tn = _tile(rows, 1024), _tile(f, 256)
    nn = f // tn
    return _mm("moe_up", (rows // tm, ne, nn),
               [h], [pl.BlockSpec((tm, k), lambda i, e, j: (i, 0))],
               [w1, w3], [pl.BlockSpec((None, k, tn), lambda i, e, j: (e, 0, j))] * 2,
               [comb], [pl.BlockSpec((tm, LANES), lambda i, e, j: (i, 0))],
               [jax.ShapeDtypeStruct((rows, ne * f), BF16)],
               [pl.BlockSpec((tm, tn), lambda i, e, j: (i, e * nn + j))],
               [[(0, 0)], [(0, 1)]], _moe_up_epilogue)[0]


def _na_body(q_ref, k_ref, v_ref, kc_ref, vc_ref, bias_ref, o_ref, *, grid_rows, win_rows):
    scale = HEAD_DIM ** -0.5
    kc = _bf(kc_ref[...])
    vc = _bf(vc_ref[...])
    win = win_rows * GRID_W

    def row_block(qr, carry):
        rs = jnp.clip(qr - win_rows // 2, 0, grid_rows - win_rows)
        off = rs - qr + (NA_WIN_R - 1)
        q = q_ref[pl.ds(pl.multiple_of(qr * GRID_W, GRID_W), GRID_W), :]
        k0 = pl.multiple_of(rs * GRID_W, GRID_W)
        kw = k_ref[pl.ds(k0, win), :]
        vw = v_ref[pl.ds(k0, win), :]
        s_loc = _dot_nt(q, kw) * scale + bias_ref[off]
        s_ctx = _dot_nt(q, kc) * scale
        m = jnp.maximum(jnp.max(s_loc, axis=-1, keepdims=True), jnp.max(s_ctx, axis=-1, keepdims=True))
        p_loc = jnp.exp(s_loc - m)
        p_ctx = jnp.exp(s_ctx - m)
        den = jnp.sum(p_loc, axis=-1, keepdims=True) + jnp.sum(p_ctx, axis=-1, keepdims=True)
        o = (_dot(p_loc, vw) + _dot(p_ctx, vc)) / den
        o_ref[pl.ds(pl.multiple_of(qr * GRID_W, GRID_W), GRID_W), :] = o.astype(o_ref.dtype)
        return carry

    lax.fori_loop(0, grid_rows, row_block, 0)


def _na_bias_slabs(rpb, win_rows):
    cols = np.arange(GRID_W)
    start = np.clip(cols - NA_WIN_C // 2, 0, GRID_W - NA_WIN_C)
    kc = cols[None, :]
    inside = (kc >= start[:, None]) & (kc < start[:, None] + NA_WIN_C)
    dc = np.clip(kc - cols[:, None] + (NA_WIN_C - 1), 0, 2 * NA_WIN_C - 2)
    band = jnp.where(jnp.asarray(inside)[None, None], rpb[:, :, dc], NEG_BIG)
    n_off = 2 * NA_WIN_R - win_rows
    dr = np.arange(n_off)[:, None] + np.arange(win_rows)[None, :]
    slabs = band[:, dr]
    slabs = jnp.transpose(slabs, (0, 1, 3, 2, 4))
    return slabs.reshape(rpb.shape[0], n_off, GRID_W, win_rows * GRID_W)


def _neighbourhood_attention(proj, nb, seq, ctx_len, heads, rpb):
    grid_rows = seq // GRID_W
    win_rows = min(NA_WIN_R, grid_rows)
    bias = _na_bias_slabs(rpb, win_rows)
    n_off = bias.shape[1]
    cr0 = nb * seq // ctx_len
    body = functools.partial(_na_body, grid_rows=grid_rows, win_rows=win_rows)
    return pl.pallas_call(
        body, grid=(nb, heads),
        in_specs=[pl.BlockSpec((seq, HEAD_DIM), lambda b, h: (b, h)),
                  pl.BlockSpec((seq, HEAD_DIM), lambda b, h: (b, heads + h)),
                  pl.BlockSpec((seq, HEAD_DIM), lambda b, h: (b, 2 * heads + h)),
                  pl.BlockSpec((ctx_len, HEAD_DIM), lambda b, h: (cr0 + b, heads + h)),
                  pl.BlockSpec((ctx_len, HEAD_DIM), lambda b, h: (cr0 + b, 2 * heads + h)),
                  pl.BlockSpec((None, n_off, GRID_W, win_rows * GRID_W), lambda b, h: (h, 0, 0, 0))],
        out_specs=pl.BlockSpec((seq, HEAD_DIM), lambda b, h: (b, h)),
        out_shape=jax.ShapeDtypeStruct((nb * seq, heads * HEAD_DIM), BF16),
        compiler_params=_params(2), name="neighbourhood_attention",
    )(proj, proj, proj, proj, proj, bias)


def _mla_body(q_ref, kvl_ref, kvc_ref, krl_ref, krc_ref, cq_ref, sq_ref, ck_ref, sk_ref, o_ref, *, nope, vdim):
    scale = (nope + MLA_ROPE) ** -0.5
    q = q_ref[...]
    qn = q[:, :nope]
    qr = q[:, nope:nope + MLA_ROPE] * cq_ref[...] + q[:, nope + MLA_ROPE:] * sq_ref[...]
    krl = krl_ref[...]
    kr_lat = krl[:, :MLA_ROPE] * ck_ref[...] + krl[:, MLA_ROPE:] * sk_ref[...]
    kr_ctx = krc_ref[...][:, :MLA_ROPE]
    kvl = kvl_ref[...]
    kvc = kvc_ref[...]
    s_lat = (_dot_nt(qn, kvl[:, :nope]) + _dot_nt(qr, kr_lat)) * scale
    s_ctx = (_dot_nt(qn, kvc[:, :nope]) + _dot_nt(qr, kr_ctx)) * scale
    m = jnp.maximum(jnp.max(s_lat, axis=-1, keepdims=True), jnp.max(s_ctx, axis=-1, keepdims=True))
    p_lat = jnp.exp(s_lat - m)
    p_ctx = jnp.exp(s_ctx - m)
    den = jnp.sum(p_lat, axis=-1, keepdims=True) + jnp.sum(p_ctx, axis=-1, keepdims=True)
    o = (_dot(p_lat, kvl[:, nope:]) + _dot(p_ctx, kvc[:, nope:])) / den
    o_ref[...] = o.astype(o_ref.dtype)


def _rope_tables(seq):
    pos = np.arange(seq)
    row = (pos // GRID_W).astype(np.float32)
    col = (pos % GRID_W).astype(np.float32)
    half = MLA_ROPE // 2
    inv = jnp.asarray(ROPE_BASE, F32) ** (-jnp.arange(0, half, 2, dtype=F32) / half)
    ang_r = jnp.asarray(row)[:, None] * inv
    ang_c = jnp.asarray(col)[:, None] * inv
    ang = jnp.concatenate([ang_r, ang_r, ang_c, ang_c], axis=-1)
    return jnp.cos(ang), jnp.sin(ang)


def _rotate_partner_columns(w):
    q = MLA_ROPE // 4
    parts = [w[:, i * q:(i + 1) * q] for i in range(4)]
    return jnp.concatenate([-parts[1], parts[0], -parts[3], parts[2]], axis=1)


def _latent_attention(q, kv, kr, nb, seq, ctx_len, heads, nope, vdim):
    cos, sin = _rope_tables(seq)
    tq = _tile(seq, 512)
    nq = seq // tq
    cr0 = nb * seq // ctx_len
    qw = nope + 2 * MLA_ROPE
    body = functools.partial(_mla_body, nope=nope, vdim=vdim)
    return pl.pallas_call(
        body, grid=(nb, heads, nq),
        in_specs=[pl.BlockSpec((tq, qw), lambda b, h, i: (b * nq + i, h)),
                  pl.BlockSpec((seq, nope + vdim), lambda b, h, i: (b, h)),
                  pl.BlockSpec((ctx_len, nope + vdim), lambda b, h, i: (cr0 + b, h)),
                  pl.BlockSpec((seq, 2 * MLA_ROPE), lambda b, h, i: (b, 0)),
                  pl.BlockSpec((ctx_len, 2 * MLA_ROPE), lambda b, h, i: (cr0 + b, 0)),
                  pl.BlockSpec((tq, MLA_ROPE), lambda b, h, i: (i, 0)),
                  pl.BlockSpec((tq, MLA_ROPE), lambda b, h, i: (i, 0)),
                  pl.BlockSpec((seq, MLA_ROPE), lambda b, h, i: (0, 0)),
                  pl.BlockSpec((seq, MLA_ROPE), lambda b, h, i: (0, 0))],
        out_specs=pl.BlockSpec((tq, vdim), lambda b, h, i: (b * nq + i, h)),
        out_shape=jax.ShapeDtypeStruct((nb * seq, heads * vdim), BF16),
        compiler_params=_params(3), name="latent_attention",
    )(q, kv, kv, kr, kr, cos, sin, cos, sin)


def kernel(x, c, ctx, c_ctx, mod_w, mod_b, norm_mix_g, norm_ffn_g, ev_w_in, ev_w_out, hy_conv_w, hy_conv_b, hy_pe_w, hy_pe_b, hy_w1, hy_b1, hy_w2, hy_b2, hy_w_out, hy_freq, hy_skip, ffn_w1, ffn_w3, ffn_w2, od_w_in, od_w_out, na_rpb, mla_q_g, mla_w_uq, mla_kv_g, mla_w_ukv, moe_router_w, moe_router_b, moe_w1, moe_w3, moe_w2, final_g):
    nb, seq, d = x.shape
    ctx_len = ctx.shape[1]
    depth = mod_w.shape[0]
    assert depth == 2 and ev_w_in.shape[0] == 1 and od_w_in.shape[0] == 1, "even layer then odd (last) layer"
    lat_rows, ctx_rows = nb * seq, nb * ctx_len
    rows = lat_rows + ctx_rows
    assert nb + 1 <= MOD_ROWS

    def seg_of_tile(tile_rows):
        assert seq % tile_rows == 0
        return lambda i: jnp.minimum((i * tile_rows) // seq, nb)

    xr = jnp.concatenate([x.reshape(lat_rows, d), ctx.reshape(ctx_rows, d)], axis=0)
    c_all = jnp.concatenate([c, c_ctx[None, :], jnp.zeros((MOD_ROWS - nb - 1, d), F32)], axis=0)
    mod = _modulation(c_all, mod_w, mod_b).reshape(depth, MOD_ROWS, N_MOD, 1, d)
    mods = [[mod[i, :, m] for m in range(N_MOD)] for i in range(depth)]

    sh1, sc1, g1, sh2, sc2, g2 = mods[0]
    fw = (ev_w_out.shape[1]) // 2
    hw = ev_w_out.shape[1] - fw
    h = _rmsnorm("norm_mix0", xr, norm_mix_g[0], BF16, mod=(sh1, sc1), seg_of_tile=seg_of_tile)[0]
    proj = _linear("even_in", h, ev_w_in[0], F32)
    ab = _group_dft(proj, fw)
    filt_args = (hy_pe_w[0], hy_pe_b[0], hy_w1[0], hy_b1[0], hy_w2[0], hy_b2[0], hy_w_out[0], hy_freq[0])
    for tag, row0, length in (("lat", 0, seq), ("ctx", lat_rows, ctx_len)):
        zf = _position_dft(f"fourier_{tag}", ab, row0, nb, length, fw)
        uc = _short_conv(f"short_conv_{tag}", proj, fw, (HY_ORDER + 1) * hw, row0, nb, length,
                         hy_conv_w[0], hy_conv_b[0])
        filt = _hyena_filter_sums(length, hw, *filt_args)
        zh = _hyena_stream(tag, uc, nb, length, hw, filt, hy_skip[0])
        xr = _mixer_out(f"even_out_{tag}", xr, row0, nb * length, zf, zh, ev_w_out[0], g1, seg_of_tile)
    h2 = _rmsnorm("norm_ffn0", xr, norm_ffn_g[0], BF16, mod=(sh2, sc2), seg_of_tile=seg_of_tile)[0]
    u = _ffn_up(h2, ffn_w1[0], ffn_w3[0])
    xr = _down_residual("ffn_down", xr, rows, u, ffn_w2[0], g2, seg_of_tile, tk_pref=5504, tn_pref=256)

    sh1, sc1, g1, sh2, sc2, g2 = mods[1]
    w_in = od_w_in[0]
    na_w = od_w_out.shape[1] // 2
    heads = na_w // HEAD_DIM
    q_rank = mla_w_uq.shape[1]
    kv_rank = mla_w_ukv.shape[1]
    mla_heads = (od_w_out.shape[1] - na_w) // HEAD_DIM
    nope = mla_w_uq.shape[2] // mla_heads - MLA_ROPE
    vdim = mla_w_ukv.shape[2] // mla_heads - nope
    main_cols = 3 * na_w + q_rank + kv_rank
    h = _rmsnorm("norm_mix1", xr, norm_mix_g[1], BF16, mod=(sh1, sc1), seg_of_tile=seg_of_tile)[0]
    proj = _linear("odd_in", h, w_in, F32, n_cols=main_cols)
    w_kr = w_in[:, main_cols:main_cols + MLA_ROPE]
    kr = _linear("odd_in_rope", h, jnp.concatenate([w_kr, _rotate_partner_columns(w_kr)], axis=1), F32)
    cq = _rmsnorm("norm_q", proj, mla_q_g[0], BF16, rows=lat_rows, width=q_rank, col_block=3 * na_w // q_rank)[0]
    ckv = _rmsnorm("norm_kv", proj, mla_kv_g[0], BF16, width=kv_rank, col_block=(3 * na_w + q_rank) // kv_rank)[0]
    w_uq = mla_w_uq[0].reshape(q_rank, mla_heads, nope + MLA_ROPE)
    w_uq = jnp.concatenate([w_uq, jnp.stack([_rotate_partner_columns(w_uq[:, hh, nope:])
                                             for hh in range(mla_heads)], axis=1)], axis=-1)
    q = _linear("mla_q", cq, w_uq.reshape(q_rank, mla_heads * (nope + 2 * MLA_ROPE)), F32)
    kv = _linear("mla_kv", ckv, mla_w_ukv[0], F32)
    z_na = _neighbourhood_attention(proj, nb, seq, ctx_len, heads, na_rpb[0])
    z_mla = _latent_attention(q, kv, kr, nb, seq, ctx_len, mla_heads, nope, vdim)
    xr = _mixer_out("odd_out", xr, 0, lat_rows, z_na, z_mla, od_w_out[0], g1, seg_of_tile)
    h2, comb = _rmsnorm("norm_ffn1", xr, norm_ffn_g[1], BF16, rows=lat_rows, mod=(sh2, sc2),
                        seg_of_tile=seg_of_tile, router=(moe_router_w[0], moe_router_b[0]))
    ne, _, f_exp = moe_w1.shape[1:]
    u = _moe_up(h2, lat_rows, comb, moe_w1[0], moe_w3[0])
    xr = _down_residual("moe_down", xr, lat_rows, u, moe_w2[0].reshape(ne * f_exp, d), g2, seg_of_tile,
                        tk_pref=2048, tn_pref=512)
    out = _rmsnorm("norm_final", xr, final_g, F32, rows=lat_rows)[0]
    return out.reshape(nb, seq, d)
```

```python
import functools
import math

import jax
import jax.numpy as jnp
import numpy as np
from jax import lax
from jax.experimental import pallas as pl
from jax.experimental.pallas import tpu as pltpu

F32 = jnp.float32
BF16 = jnp.bfloat16

GRID_W = 64
HEAD_DIM = 128
N_MOD = 6
HY_ORDER = 2
HY_POS_BANDS = 16
HY_DECAY_TARGET = 1e-2
HY_FAST_DECAY = 0.3
HY_SLOW_DECAY = 1.5
NA_WIN_R = 8
NA_WIN_C = 16
MLA_ROPE = 64
ROPE_BASE = 10000.0
TOP_K = 2
EPS = 1e-6

LANES = 128
MOD_ROWS = 8
V7X_VMEM_LIMIT = 58 * 2 ** 20
NEG_BIG = -1e30


def _tile(n, pref):
    if n <= pref:
        return n
    t = (pref // LANES) * LANES
    while t >= LANES:
        if n % t == 0:
            return t
        t -= LANES
    raise ValueError(f"no lane-aligned tile for {n} <= {pref}")


def _params(n_grid):
    return pltpu.CompilerParams(dimension_semantics=("arbitrary",) * n_grid,
                                vmem_limit_bytes=V7X_VMEM_LIMIT)


def _bf(a):
    return a if a.dtype == BF16 else a.astype(BF16)


def _dot(a, b):
    return jnp.dot(_bf(a), _bf(b), preferred_element_type=F32)


def _dot_nt(a, b):
    return lax.dot_general(_bf(a), _bf(b), (((1,), (1,)), ((), ())), preferred_element_type=F32)


def _split(a):
    hi = a.astype(BF16)
    lo = (a - hi.astype(F32)).astype(BF16)
    return hi, lo


def _dot3(a, b):
    ah, al = _split(a)
    bh, bl = _split(b)
    return (jnp.dot(ah, bh, preferred_element_type=F32) + jnp.dot(ah, bl, preferred_element_type=F32)
            + jnp.dot(al, bh, preferred_element_type=F32))


def _silu(a):
    return a * (1.0 / (1.0 + jnp.exp(-a)))


def _mm_body(*refs, n_x, n_w, n_e, n_o, groups, nk, epilogue, n_grid):
    xs = refs[:n_x]
    ws = refs[n_x:n_x + n_w]
    es = refs[n_x + n_w:n_x + n_w + n_e]
    outs = refs[n_x + n_w + n_e:n_x + n_w + n_e + n_o]
    accs = refs[n_x + n_w + n_e + n_o:]
    pids = [pl.program_id(a) for a in range(n_grid)]
    xv = [_bf(x[...]) for x in xs]
    wv = [_bf(w[...]) for w in ws]
    parts = []
    for group in groups:
        p = None
        for xi, wi in group:
            d = jnp.dot(xv[xi], wv[wi], preferred_element_type=F32)
            p = d if p is None else p + d
        parts.append(p)
    if nk == 1:
        epilogue(parts, es, outs, pids)
        return
    k = pids[-1]

    @pl.when(k == 0)
    def _():
        for a, p in zip(accs, parts):
            a[...] = p

    @pl.when(k > 0)
    def _():
        for a, p in zip(accs, parts):
            a[...] += p

    @pl.when(k == nk - 1)
    def _():
        epilogue([a[...] for a in accs], es, outs, pids)


def _mm(name, grid, xs, x_specs, ws, w_specs, es, e_specs, out_shapes, out_specs, groups, epilogue,
        nk=1, acc_block=None, aliases=None):
    body = functools.partial(_mm_body, n_x=len(xs), n_w=len(ws), n_e=len(es), n_o=len(out_shapes),
                             groups=groups, nk=nk, epilogue=epilogue, n_grid=len(grid))
    scratch = [pltpu.VMEM(acc_block, F32) for _ in groups] if nk > 1 else []
    res = pl.pallas_call(
        body, grid=grid, in_specs=list(x_specs) + list(w_specs) + list(e_specs),
        out_specs=list(out_specs), out_shape=list(out_shapes), scratch_shapes=scratch,
        input_output_aliases=aliases or {}, compiler_params=_params(len(grid)), name=name,
    )(*xs, *ws, *es)
    return res


def _store0(parts, es, outs, pids):
    outs[0][...] = parts[0].astype(outs[0].dtype)


def _linear(name, x, w, out_dtype, *, n_cols=None, tm_pref=1024, tn_pref=512):
    m, k = x.shape
    n = n_cols or w.shape[1]
    tm, tn = _tile(m, tm_pref), _tile(n, tn_pref)
    return _mm(name, (m // tm, n // tn),
               [x], [pl.BlockSpec((tm, k), lambda i, j: (i, 0))],
               [w], [pl.BlockSpec((k, tn), lambda i, j: (0, j))],
               [], [], [jax.ShapeDtypeStruct((m, n), out_dtype)],
               [pl.BlockSpec((tm, tn), lambda i, j: (i, j))], [[(0, 0)]], _store0)[0]


def _mod_body(c_ref, w_ref, b_ref, o_ref):
    o_ref[...] = _dot(_silu(c_ref[...]), w_ref[...]) + b_ref[...]


def _modulation(c_all, mod_w, mod_b):
    depth, d, n = mod_w.shape
    tn = _tile(n, 512)
    return pl.pallas_call(
        _mod_body, grid=(depth, n // tn),
        in_specs=[pl.BlockSpec((MOD_ROWS, d), lambda l, j: (0, 0)),
                  pl.BlockSpec((None, d, tn), lambda l, j: (l, 0, j)),
                  pl.BlockSpec((None, 1, tn), lambda l, j: (l, 0, j))],
        out_specs=pl.BlockSpec((None, MOD_ROWS, tn), lambda l, j: (l, 0, j)),
        out_shape=jax.ShapeDtypeStruct((depth, MOD_ROWS, n), F32),
        compiler_params=_params(2), name="modulation",
    )(c_all, mod_w, mod_b.reshape(depth, 1, n))


def _norm_body(*refs, modulate, router, n_experts):
    it = iter(refs)
    x_ref, g_ref = next(it), next(it)
    x = x_ref[...]
    y = x * lax.rsqrt(jnp.mean(x * x, axis=-1, keepdims=True) + EPS) * g_ref[...]
    if modulate:
        sh_ref, sc_ref = next(it), next(it)
        y = y * (1.0 + sc_ref[...]) + sh_ref[...]
    if router:
        rw_ref, rb_ref = next(it), next(it)
    o_ref = next(it)
    o_ref[...] = y.astype(o_ref.dtype)
    if router:
        comb_ref = next(it)
        logits = _dot3(y, rw_ref[...]) + rb_ref[...]
        lane = lax.broadcasted_iota(jnp.int32, logits.shape, 1).astype(F32)
        neg = jnp.float32(-jnp.inf)
        logits = jnp.where(lane < n_experts, logits, neg)
        m1 = jnp.max(logits, axis=-1, keepdims=True)
        i1 = jnp.min(jnp.where(logits == m1, lane, float(LANES)), axis=-1, keepdims=True)
        rest = jnp.where(lane == i1, neg, logits)
        m2 = jnp.max(rest, axis=-1, keepdims=True)
        i2 = jnp.min(jnp.where(rest == m2, lane, float(LANES)), axis=-1, keepdims=True)
        e2 = jnp.exp(m2 - m1)
        g1 = 1.0 / (1.0 + e2)
        g2 = e2 / (1.0 + e2)
        comb_ref[...] = jnp.where(lane == i1, g1, 0.0) + jnp.where(lane == i2, g2, 0.0)


def _rmsnorm(name, x, gain, out_dtype, *, rows=None, width=None, col_block=0, mod=None, seg_of_tile=None,
             router=None, tr=256):
    rows = rows or x.shape[0]
    width = width or x.shape[1]
    tr = _tile(rows, tr)
    in_specs = [pl.BlockSpec((tr, width), lambda i: (i, col_block)),
                pl.BlockSpec((1, width), lambda i: (0, 0))]
    args = [x, gain.reshape(1, width)]
    if mod is not None:
        seg = seg_of_tile(tr)
        for t in mod:
            in_specs.append(pl.BlockSpec((None, 1, width), lambda i: (seg(i), 0, 0)))
            args.append(t)
    out_shape = [jax.ShapeDtypeStruct((rows, width), out_dtype)]
    out_specs = [pl.BlockSpec((tr, width), lambda i: (i, 0))]
    n_experts = 0
    if router is not None:
        rw, rb = router
        n_experts = rw.shape[1]
        rw_p = jnp.pad(rw, ((0, 0), (0, LANES - n_experts)))
        rb_p = jnp.pad(rb, (0, LANES - n_experts)).reshape(1, LANES)
        in_specs += [pl.BlockSpec((width, LANES), lambda i: (0, 0)), pl.BlockSpec((1, LANES), lambda i: (0, 0))]
        args += [rw_p, rb_p]
        out_shape.append(jax.ShapeDtypeStruct((rows, LANES), F32))
        out_specs.append(pl.BlockSpec((tr, LANES), lambda i: (i, 0)))
    body = functools.partial(_norm_body, modulate=mod is not None, router=router is not None, n_experts=n_experts)
    return pl.pallas_call(body, grid=(rows // tr,), in_specs=in_specs, out_specs=out_specs, out_shape=out_shape,
                          compiler_params=_params(1), name=name)(*args)


def _group_dft_body(p_ref, cs_ref, o_ref, *, groups):
    cs = cs_ref[...]
    fw = groups * HEAD_DIM
    for g in range(groups):
        r = _dot(p_ref[:, g * HEAD_DIM:(g + 1) * HEAD_DIM], cs)
        o_ref[:, g * HEAD_DIM:(g + 1) * HEAD_DIM] = r[:, :HEAD_DIM].astype(o_ref.dtype)
        o_ref[:, fw + g * HEAD_DIM:fw + (g + 1) * HEAD_DIM] = r[:, HEAD_DIM:].astype(o_ref.dtype)


def _group_dft(proj, fw):
    rows = proj.shape[0]
    k = np.arange(HEAD_DIM)
    ang = (2.0 * np.pi / HEAD_DIM) * ((k[:, None] * k[None, :]) % HEAD_DIM)
    cs = jnp.asarray(np.concatenate([np.cos(ang), np.sin(ang)], axis=1), dtype=BF16)
    tm = _tile(rows, 512)
    body = functools.partial(_group_dft_body, groups=fw // HEAD_DIM)
    return pl.pallas_call(
        body, grid=(rows // tm,),
        in_specs=[pl.BlockSpec((tm, fw), lambda i: (i, 0)),
                  pl.BlockSpec((HEAD_DIM, 2 * HEAD_DIM), lambda i: (0, 0))],
        out_specs=pl.BlockSpec((tm, 2 * fw), lambda i: (i, 0)),
        out_shape=jax.ShapeDtypeStruct((rows, 2 * fw), BF16),
        compiler_params=_params(1), name="group_dft",
    )(proj, cs)


def _angles(num, den):
    return (num % den).astype(F32) * (2.0 * math.pi / den)


def _position_dft(name, ab, row0, nb, length, fw):
    f = jnp.arange(length, dtype=jnp.int32)
    ang = _angles(f[:, None] * f[None, :], length)
    norm = 1.0 / math.sqrt(length * HEAD_DIM)
    cs = (jnp.concatenate([jnp.cos(ang), -jnp.sin(ang)], axis=1) * norm).astype(BF16)
    tm, tn = _tile(length, 1024), _tile(fw, 512)
    nm, nn = length // tm, fw // tn
    rb0 = row0 // length
    return _mm(name, (nb, nm, nn, 2),
               [cs], [pl.BlockSpec((tm, length), lambda b, i, j, k: (i, k))],
               [ab], [pl.BlockSpec((length, tn), lambda b, i, j, k: (rb0 + b, k * nn + j))],
               [], [], [jax.ShapeDtypeStruct((nb * length, fw), BF16)],
               [pl.BlockSpec((tm, tn), lambda b, i, j, k: (b * nm + i, j))],
               [[(0, 0)]], _store0, nk=2, acc_block=(tm, tn))[0]


def _short_conv_body(p_ref, w_ref, b_ref, o_ref):
    u = p_ref[...]
    n = u.shape[0]
    row = lax.broadcasted_iota(jnp.int32, u.shape, 0)
    prev = jnp.where(row == 0, 0.0, pltpu.roll(u, 1, 0))
    nxt = jnp.where(row == n - 1, 0.0, pltpu.roll(u, n - 1, 0))
    w = w_ref[...]
    o_ref[...] = w[0:1] * prev + w[1:2] * u + w[2:3] * nxt + b_ref[...]


def _short_conv(name, proj, col0, width, row0, nb, length, conv_w, conv_b):
    tc = _tile(math.gcd(col0, width), 512)
    cb0, rb0 = col0 // tc, row0 // length
    return pl.pallas_call(
        _short_conv_body, grid=(nb, width // tc),
        in_specs=[pl.BlockSpec((length, tc), lambda b, j: (rb0 + b, cb0 + j)),
                  pl.BlockSpec((3, tc), lambda b, j: (0, j)),
                  pl.BlockSpec((1, tc), lambda b, j: (0, j))],
        out_specs=pl.BlockSpec((length, tc), lambda b, j: (b, j)),
        out_shape=jax.ShapeDtypeStruct((nb * length, width), F32),
        compiler_params=_params(2), name=name,
    )(proj, conv_w, conv_b.reshape(1, width))


def _filter_mlp_body(z_ref, pw_ref, pb_ref, w1_ref, b1_ref, w2_ref, b2_ref, fr_ref, o_ref):
    fr = fr_ref[...]
    h = jnp.sin(fr * (_dot3(z_ref[...], pw_ref[...]) + pb_ref[...]))
    h = jnp.sin(fr * (_dot3(h, w1_ref[...]) + b1_ref[...]))
    o_ref[...] = jnp.sin(fr * (_dot3(h, w2_ref[...]) + b2_ref[...]))


def _filter_out_body(h_ref, wf_ref, wb_ref, d_ref, o_ref):
    h = h_ref[...]
    d = d_ref[...]
    fwd = _dot3(h, wf_ref[...]) * d
    bwd = _dot3(h, wb_ref[...]) * d
    row = lax.broadcasted_iota(jnp.int32, bwd.shape, 0)
    bwd = jnp.where(row == 0, 0.0, bwd)
    o_ref[0] = fwd + bwd
    o_ref[1] = fwd - bwd


def _hyena_filter_sums(length, hw, pe_w, pe_b, w1, b1, w2, b2, w_out, freq):
    t = jnp.linspace(0.0, 1.0, length, dtype=F32)[:, None]
    w = (2.0 * math.pi / length) * jnp.arange(length, dtype=F32)[:, None]
    bands = jnp.linspace(1e-4, HY_POS_BANDS - 1, HY_POS_BANDS, dtype=F32)[None, :]
    z = jnp.concatenate([t, jnp.cos(bands * w), -jnp.sin(bands * w)], axis=-1)
    pos_dim, fwid = pe_w.shape
    z = jnp.pad(z, ((0, 0), (0, LANES - pos_dim)))
    pe_w = jnp.pad(pe_w, ((0, LANES - pos_dim), (0, 0)))
    row = lambda v: v.reshape(1, fwid)
    full = lambda a: pl.BlockSpec(a.shape, lambda i: (0,) * a.ndim)
    args = [z, pe_w, row(pe_b), w1, row(b1), w2, row(b2), row(freq)]
    h = pl.pallas_call(
        _filter_mlp_body, grid=(1,), in_specs=[full(a) for a in args],
        out_specs=pl.BlockSpec((length, fwid), lambda i: (0, 0)),
        out_shape=jax.ShapeDtypeStruct((length, fwid), F32),
        compiler_params=_params(1), name="hyena_filter_mlp",
    )(*args)
    max_decay = math.log(HY_DECAY_TARGET) / HY_FAST_DECAY
    min_decay = math.log(HY_DECAY_TARGET) / HY_SLOW_DECAY
    deltas = jnp.abs(jnp.linspace(min_decay, max_decay, hw, dtype=F32))
    decay = jnp.exp(-t * deltas)
    tc = _tile(hw, 256)
    nc = hw // tc
    return pl.pallas_call(
        _filter_out_body, grid=(HY_ORDER, nc),
        in_specs=[pl.BlockSpec((length, fwid), lambda n, j: (0, 0)),
                  pl.BlockSpec((fwid, tc), lambda n, j: (0, (2 * n) * nc + j)),
                  pl.BlockSpec((fwid, tc), lambda n, j: (0, (2 * n + 1) * nc + j)),
                  pl.BlockSpec((length, tc), lambda n, j: (0, j))],
        out_specs=pl.BlockSpec((2, length, tc), lambda n, j: (0, 0, n * nc + j)),
        out_shape=jax.ShapeDtypeStruct((2, length, HY_ORDER * hw), F32),
        compiler_params=_params(2), name="hyena_filter_out",
    )(h, w_out, w_out, decay)


def _odd_dft_matrix(length):
    f = jnp.arange(length, dtype=jnp.int32)
    ang = _angles((2 * f[:, None] + 1) * f[None, :], 4 * length)
    return jnp.concatenate([jnp.cos(ang), jnp.sin(ang)], axis=0)


def _hyena_spectrum_epilogue(parts, es, outs, pids):
    ur, us = parts
    kr, ks = es[0][0], es[0][1]
    outs[0][0] = (ur * kr - us * ks).astype(outs[0].dtype)
    outs[0][1] = (ur * ks + us * kr).astype(outs[0].dtype)


def _hyena_gate_epilogue(parts, es, outs, pids):
    gate_ref, z_ref, skip_ref = es
    outs[0][...] = (gate_ref[...] * (parts[0] + z_ref[...] * skip_ref[...])).astype(outs[0].dtype)


def _hyena_stream(tag, uc, nb, length, hw, filt, skip):
    fo = _odd_dft_matrix(length)
    fo_b = fo.astype(BF16)
    go_b = (fo.T * (1.0 / length)).astype(BF16)
    tm, tn = _tile(length, 1024), _tile(hw, 512)
    nm, nn = length // tm, hw // tn
    kspec = _mm(f"hyena_kspec_{tag}", (2 * nm, HY_ORDER * nn),
                [fo_b], [pl.BlockSpec((tm, length), lambda i, j: (i, 0))],
                [filt.reshape(2 * length, HY_ORDER * hw)],
                [pl.BlockSpec((length, tn), lambda i, j: (i // nm, j))],
                [], [], [jax.ShapeDtypeStruct((2 * length, HY_ORDER * hw), F32)],
                [pl.BlockSpec((tm, tn), lambda i, j: (i, j))], [[(0, 0)]], _store0)[0]
    kspec = kspec.reshape(2, length, HY_ORDER * hw)
    z = uc
    for n in range(HY_ORDER):
        z = _hyena_order(tag, n, uc, z, nb, length, hw, fo_b, go_b, kspec, skip, (tm, tn, nm, nn))
    return z


def _hyena_order(tag, n, uc, z, nb, length, hw, fo_b, go_b, kspec, skip, tiles):
    tm, tn, nm, nn = tiles
    gate_cb = (n + 1) * nn
    spec = _mm(f"hyena_fwd{n}_{tag}", (nm, nb, nn),
               [fo_b, fo_b], [pl.BlockSpec((tm, length), lambda i, b, j: (i, 0)),
                              pl.BlockSpec((tm, length), lambda i, b, j: (nm + i, 0))],
               [z], [pl.BlockSpec((length, tn), lambda i, b, j: (b, j))],
               [kspec], [pl.BlockSpec((2, tm, tn), lambda i, b, j: (0, i, n * nn + j))],
               [jax.ShapeDtypeStruct((nb, 2, length, hw), BF16)],
               [pl.BlockSpec((None, 2, tm, tn), lambda i, b, j: (b, 0, i, j))],
               [[(0, 0)], [(1, 0)]], _hyena_spectrum_epilogue)[0]
    last = n == HY_ORDER - 1
    return _mm(f"hyena_inv{n}_{tag}", (nb, nm, nn),
               [go_b], [pl.BlockSpec((tm, 2 * length), lambda b, i, j: (i, 0))],
               [spec.reshape(nb * 2 * length, hw)],
               [pl.BlockSpec((2 * length, tn), lambda b, i, j: (b, j))],
               [uc, z, skip.reshape(HY_ORDER, 1, hw)],
               [pl.BlockSpec((tm, tn), lambda b, i, j: (b * nm + i, gate_cb + j)),
                pl.BlockSpec((tm, tn), lambda b, i, j: (b * nm + i, j)),
                pl.BlockSpec((None, 1, tn), lambda b, i, j: (n, 0, j))],
               [jax.ShapeDtypeStruct((nb * length, hw), BF16 if last else F32)],
               [pl.BlockSpec((tm, tn), lambda b, i, j: (b * nm + i, j))],
               [[(0, 0)]], _hyena_gate_epilogue)[0]


def _resgate_epilogue(parts, es, outs, pids):
    res_ref, gate_ref = es
    outs[0][...] = res_ref[...] + gate_ref[...] * parts[0]


def _mixer_out(name, x_res, row0, rows, za, zb, w_out, gate, seg_of_tile):
    d = x_res.shape[1]
    half = za.shape[1]
    tm, tn = _tile(rows, 1024), _tile(d, 512)
    rb0 = row0 // tm
    seg = seg_of_tile(tm)
    return _mm(name, (rows // tm, d // tn),
               [za, zb], [pl.BlockSpec((tm, half), lambda i, j: (i, 0))] * 2,
               [w_out, w_out], [pl.BlockSpec((half, tn), lambda i, j: (0, j)),
                                pl.BlockSpec((half, tn), lambda i, j: (1, j))],
               [x_res, gate], [pl.BlockSpec((tm, tn), lambda i, j: (rb0 + i, j)),
                               pl.BlockSpec((None, 1, tn), lambda i, j: (seg(rb0 + i), 0, j))],
               [jax.ShapeDtypeStruct(x_res.shape, F32)],
               [pl.BlockSpec((tm, tn), lambda i, j: (rb0 + i, j))],
               [[(0, 0), (1, 1)]], _resgate_epilogue, aliases={4: 0})[0]


def _swiglu_epilogue(parts, es, outs, pids):
    outs[0][...] = (_silu(parts[0]) * parts[1]).astype(outs[0].dtype)


def _ffn_up(h, w1, w3):
    m, k = h.shape
    n = w1.shape[1]
    tm, tn = _tile(m, 1024), _tile(n, 256)
    return _mm("ffn_up", (m // tm, n // tn),
               [h], [pl.BlockSpec((tm, k), lambda i, j: (i, 0))],
               [w1, w3], [pl.BlockSpec((k, tn), lambda i, j: (0, j))] * 2,
               [], [], [jax.ShapeDtypeStruct((m, n), BF16)],
               [pl.BlockSpec((tm, tn), lambda i, j: (i, j))],
               [[(0, 0)], [(0, 1)]], _swiglu_epilogue)[0]


def _down_residual(name, x_res, rows, u, w2, gate, seg_of_tile, tm_pref, tk_pref, tn_pref):
    d = x_res.shape[1]
    kdim = u.shape[1]
    tm, tn, tk = _tile(rows, tm_pref), _tile(d, tn_pref), _tile(kdim, tk_pref)
    nk = kdim // tk
    seg = seg_of_tile(tm)
    return _mm(name, (rows // tm, d // tn, nk),
               [u], [pl.BlockSpec((tm, tk), lambda i, j, k: (i, k))],
               [w2], [pl.BlockSpec((tk, tn), lambda i, j, k: (k, j))],
               [x_res, gate], [pl.BlockSpec((tm, tn), lambda i, j, k: (i, j)),
                               pl.BlockSpec((None, 1, tn), lambda i, j, k: (seg(i), 0, j))],
               [jax.ShapeDtypeStruct(x_res.shape, F32)],
               [pl.BlockSpec((tm, tn), lambda i, j, k: (i, j))],
               [[(0, 0)]], _resgate_epilogue, nk=nk, acc_block=(tm, tn), aliases={2: 0})[0]


def _moe_up_epilogue(parts, es, outs, pids):
    comb = es[0][...]
    lane = lax.broadcasted_iota(jnp.int32, comb.shape, 1)
    gate = jnp.sum(jnp.where(lane == pids[1], comb, 0.0), axis=-1, keepdims=True)
    outs[0][...] = (gate * (_silu(parts[0]) * parts[1])).astype(outs[0].dtype)


def _moe_up(h, rows, comb, w1, w3):
    ne, k, f = w1.shape
    tm, tn = _tile(rows, 1024), _tile(f, 256)
    nn = f // tn
    return _mm("moe_up", (rows // tm, ne, nn),
               [h], [pl.BlockSpec((tm, k), lambda i, e, j: (i, 0))],
               [w1, w3], [pl.BlockSpec((None, k, tn), lambda i, e, j: (e, 0, j))] * 2,
               [comb], [pl.BlockSpec((tm, LANES), lambda i, e, j: (i, 0))],
               [jax.ShapeDtypeStruct((rows, ne * f), BF16)],
               [pl.BlockSpec((tm, tn), lambda i, e, j: (i, e * nn + j))],
               [[(0, 0)], [(0, 1)]], _moe_up_epilogue)[0]


NA_BLOCK_ROWS = 4


def _na_body(q_ref, k_ref, v_ref, kc_ref, vc_ref, bias_ref, o_ref, va_ref, *, seq, ctx_len, blocks, win):
    qscale = HEAD_DIM ** -0.5 * math.log2(math.e)
    bq = NA_BLOCK_ROWS * GRID_W
    va_ref[0:seq, 0:HEAD_DIM] = v_ref[...].astype(BF16)
    va_ref[seq:seq + ctx_len, 0:HEAD_DIM] = vc_ref[...].astype(BF16)
    lane = lax.broadcasted_iota(jnp.int32, (seq + ctx_len, va_ref.shape[1] - HEAD_DIM), 1)
    va_ref[:, HEAD_DIM:] = jnp.where(lane == 0, 1.0, 0.0).astype(BF16)
    kc = kc_ref[...].astype(BF16)
    for kb, (case, ws) in enumerate(blocks):
        q = (q_ref[kb * bq:(kb + 1) * bq, :] * qscale).astype(BF16)
        k0 = ws * GRID_W
        s_loc = _dot_nt(q, k_ref[k0:k0 + win, :]) + bias_ref[case]
        s_ctx = _dot_nt(q, kc)
        m = jnp.maximum(jnp.max(s_loc, axis=-1, keepdims=True), jnp.max(s_ctx, axis=-1, keepdims=True))
        acc = _dot(jnp.exp2(s_loc - m), va_ref[k0:k0 + win, :]) + _dot(jnp.exp2(s_ctx - m), va_ref[seq:seq + ctx_len, :])
        o_ref[kb * bq:(kb + 1) * bq, :] = (acc[:, :HEAD_DIM] / acc[:, HEAD_DIM:HEAD_DIM + 1]).astype(o_ref.dtype)


def _na_bias_slabs(rpb, grid_rows, win_rows):
    heads = rpb.shape[0]
    cols = np.arange(GRID_W)
    start = np.clip(cols - NA_WIN_C // 2, 0, GRID_W - NA_WIN_C)
    kc = cols[None, :]
    inside = (kc >= start[:, None]) & (kc < start[:, None] + NA_WIN_C)
    dc = np.clip(kc - cols[:, None] + (NA_WIN_C - 1), 0, 2 * NA_WIN_C - 2)
    band = jnp.where(jnp.asarray(inside)[None, None], rpb[:, :, dc], NEG_BIG)
    masked = band.shape[1]
    band = jnp.concatenate([band, jnp.full((heads, 1, GRID_W, GRID_W), NEG_BIG, F32)], axis=1)
    win = min(win_rows + NA_BLOCK_ROWS, grid_rows)
    tables, blocks = [], []
    for r0 in range(0, grid_rows, NA_BLOCK_ROWS):
        ws = int(np.clip(r0 - win_rows // 2, 0, grid_rows - win))
        idx = np.full((NA_BLOCK_ROWS, win), masked)
        for ri in range(NA_BLOCK_ROWS):
            r = r0 + ri
            rs = int(np.clip(r - win_rows // 2, 0, grid_rows - win_rows))
            assert ws <= rs and rs + win_rows <= ws + win
            for key_row in range(rs, rs + win_rows):
                idx[ri, key_row - ws] = key_row - r + (NA_WIN_R - 1)
        for case, t in enumerate(tables):
            if np.array_equal(t, idx):
                break
        else:
            case = len(tables)
            tables.append(idx)
        blocks.append((case, ws))
    slabs = band[:, np.stack(tables)]
    slabs = jnp.transpose(slabs, (0, 1, 2, 4, 3, 5))
    slabs = slabs.reshape(heads, len(tables), NA_BLOCK_ROWS * GRID_W, win * GRID_W) * math.log2(math.e)
    return slabs, tuple(blocks), win * GRID_W


def _neighbourhood_attention(proj, nb, seq, ctx_len, heads, rpb):
    grid_rows = seq // GRID_W
    assert grid_rows % NA_BLOCK_ROWS == 0
    win_rows = min(NA_WIN_R, grid_rows)
    bias, blocks, win = _na_bias_slabs(rpb, grid_rows, win_rows)
    cr0 = nb * seq // ctx_len
    body = functools.partial(_na_body, seq=seq, ctx_len=ctx_len, blocks=blocks, win=win)
    return pl.pallas_call(
        body, grid=(nb, heads),
        in_specs=[pl.BlockSpec((seq, HEAD_DIM), lambda b, h: (b, h)),
                  pl.BlockSpec((seq, HEAD_DIM), lambda b, h: (b, heads + h)),
                  pl.BlockSpec((seq, HEAD_DIM), lambda b, h: (b, 2 * heads + h)),
                  pl.BlockSpec((ctx_len, HEAD_DIM), lambda b, h: (cr0 + b, heads + h)),
                  pl.BlockSpec((ctx_len, HEAD_DIM), lambda b, h: (cr0 + b, 2 * heads + h)),
                  pl.BlockSpec((None,) + bias.shape[1:], lambda b, h: (h, 0, 0, 0))],
        out_specs=pl.BlockSpec((seq, HEAD_DIM), lambda b, h: (b, h)),
        out_shape=jax.ShapeDtypeStruct((nb * seq, heads * HEAD_DIM), BF16),
        scratch_shapes=[pltpu.VMEM((seq + ctx_len, HEAD_DIM + LANES), BF16)],
        compiler_params=_params(2), name="neighbourhood_attention",
    )(proj, proj, proj, proj, proj, bias)


def _mla_body(q_ref, kvl_ref, kvc_ref, krl_ref, krc_ref, cq_ref, sq_ref, ck_ref, sk_ref, o_ref,
              qf_ref, kf_ref, va_ref, *, nope, vdim, seq, ctx_len, chunks):
    qscale = (nope + MLA_ROPE) ** -0.5 * math.log2(math.e)
    rope_end = nope + MLA_ROPE

    @pl.when(pl.program_id(2) == 0)
    def _():
        krl = krl_ref[...]
        kf_ref[0:seq, 0:nope] = kvl_ref[:, 0:nope].astype(BF16)
        kf_ref[0:seq, nope:rope_end] = (krl[:, :MLA_ROPE] * ck_ref[...] + krl[:, MLA_ROPE:] * sk_ref[...]).astype(BF16)
        kf_ref[seq:seq + ctx_len, 0:nope] = kvc_ref[:, 0:nope].astype(BF16)
        kf_ref[seq:seq + ctx_len, nope:rope_end] = krc_ref[:, 0:MLA_ROPE].astype(BF16)
        kf_ref[:, rope_end:] = jnp.zeros((seq + ctx_len, kf_ref.shape[1] - rope_end), BF16)
        va_ref[0:seq, 0:vdim] = kvl_ref[:, nope:nope + vdim].astype(BF16)
        va_ref[seq:seq + ctx_len, 0:vdim] = kvc_ref[:, nope:nope + vdim].astype(BF16)
        lane = lax.broadcasted_iota(jnp.int32, (seq + ctx_len, va_ref.shape[1] - vdim), 1)
        va_ref[:, vdim:] = jnp.where(lane == 0, 1.0, 0.0).astype(BF16)

    q = q_ref[...]
    qf_ref[:, 0:nope] = (q[:, :nope] * qscale).astype(BF16)
    qr = q[:, nope:rope_end] * cq_ref[...] + q[:, rope_end:] * sq_ref[...]
    qf_ref[:, nope:rope_end] = (qr * qscale).astype(BF16)
    qf_ref[:, rope_end:] = jnp.zeros((q.shape[0], qf_ref.shape[1] - rope_end), BF16)
    qf = qf_ref[...]
    m = jnp.full((q.shape[0], 1), -jnp.inf, F32)
    acc = jnp.zeros((q.shape[0], va_ref.shape[1]), F32)
    for start, size in chunks:
        s = _dot_nt(qf, kf_ref[start:start + size, :])
        m_new = jnp.maximum(m, jnp.max(s, axis=-1, keepdims=True))
        p = jnp.exp2(s - m_new)
        acc = acc * jnp.exp2(m - m_new) + _dot(p, va_ref[start:start + size, :])
        m = m_new
    o_ref[...] = (acc[:, :vdim] / acc[:, vdim:vdim + 1]).astype(o_ref.dtype)


def _rope_tables(seq):
    pos = np.arange(seq)
    row = (pos // GRID_W).astype(np.float32)
    col = (pos % GRID_W).astype(np.float32)
    half = MLA_ROPE // 2
    inv = jnp.asarray(ROPE_BASE, F32) ** (-jnp.arange(0, half, 2, dtype=F32) / half)
    ang_r = jnp.asarray(row)[:, None] * inv
    ang_c = jnp.asarray(col)[:, None] * inv
    ang = jnp.concatenate([ang_r, ang_r, ang_c, ang_c], axis=-1)
    return jnp.cos(ang), jnp.sin(ang)


def _rotate_partner_columns(w):
    q = MLA_ROPE // 4
    parts = [w[:, i * q:(i + 1) * q] for i in range(4)]
    return jnp.concatenate([-parts[1], parts[0], -parts[3], parts[2]], axis=1)


def _latent_attention(q, kv, kr, nb, seq, ctx_len, heads, nope, vdim):
    cos, sin = _rope_tables(seq)
    tq = _tile(seq, 512)
    nq = seq // tq
    cr0 = nb * seq // ctx_len
    qw = nope + 2 * MLA_ROPE
    keys = seq + ctx_len
    chunk = _tile(seq, 1024)
    chunks = tuple((s, min(chunk, keys - s)) for s in range(0, keys, chunk))
    kw = nope + 2 * MLA_ROPE
    vw = vdim + LANES
    body = functools.partial(_mla_body, nope=nope, vdim=vdim, seq=seq, ctx_len=ctx_len, chunks=chunks)
    return pl.pallas_call(
        body, grid=(nb, heads, nq),
        scratch_shapes=[pltpu.VMEM((tq, kw), BF16), pltpu.VMEM((keys, kw), BF16), pltpu.VMEM((keys, vw), BF16)],
        in_specs=[pl.BlockSpec((tq, qw), lambda b, h, i: (b * nq + i, h)),
                  pl.BlockSpec((seq, nope + vdim), lambda b, h, i: (b, h)),
                  pl.BlockSpec((ctx_len, nope + vdim), lambda b, h, i: (cr0 + b, h)),
                  pl.BlockSpec((seq, 2 * MLA_ROPE), lambda b, h, i: (b, 0)),
                  pl.BlockSpec((ctx_len, 2 * MLA_ROPE), lambda b, h, i: (cr0 + b, 0)),
                  pl.BlockSpec((tq, MLA_ROPE), lambda b, h, i: (i, 0)),
                  pl.BlockSpec((tq, MLA_ROPE), lambda b, h, i: (i, 0)),
                  pl.BlockSpec((seq, MLA_ROPE), lambda b, h, i: (0, 0)),
                  pl.BlockSpec((seq, MLA_ROPE), lambda b, h, i: (0, 0))],
        out_specs=pl.BlockSpec((tq, vdim), lambda b, h, i: (b * nq + i, h)),
        out_shape=jax.ShapeDtypeStruct((nb * seq, heads * vdim), BF16),
        compiler_params=_params(3), name="latent_attention",
    )(q, kv, kv, kr, kr, cos, sin, cos, sin)


def kernel(x, c, ctx, c_ctx, mod_w, mod_b, norm_mix_g, norm_ffn_g, ev_w_in, ev_w_out, hy_conv_w, hy_conv_b, hy_pe_w, hy_pe_b, hy_w1, hy_b1, hy_w2, hy_b2, hy_w_out, hy_freq, hy_skip, ffn_w1, ffn_w3, ffn_w2, od_w_in, od_w_out, na_rpb, mla_q_g, mla_w_uq, mla_kv_g, mla_w_ukv, moe_router_w, moe_router_b, moe_w1, moe_w3, moe_w2, final_g):
    nb, seq, d = x.shape
    ctx_len = ctx.shape[1]
    depth = mod_w.shape[0]
    assert depth == 2 and ev_w_in.shape[0] == 1 and od_w_in.shape[0] == 1, "even layer then odd (last) layer"
    lat_rows, ctx_rows = nb * seq, nb * ctx_len
    rows = lat_rows + ctx_rows
    assert nb + 1 <= MOD_ROWS

    def seg_of_tile(tile_rows):
        assert seq % tile_rows == 0
        return lambda i: jnp.minimum((i * tile_rows) // seq, nb)

    xr = jnp.concatenate([x.reshape(lat_rows, d), ctx.reshape(ctx_rows, d)], axis=0)
    c_all = jnp.concatenate([c, c_ctx[None, :], jnp.zeros((MOD_ROWS - nb - 1, d), F32)], axis=0)
    mod = _modulation(c_all, mod_w, mod_b).reshape(depth, MOD_ROWS, N_MOD, 1, d)
    mods = [[mod[i, :, m] for m in range(N_MOD)] for i in range(depth)]

    sh1, sc1, g1, sh2, sc2, g2 = mods[0]
    fw = (ev_w_out.shape[1]) // 2
    hw = ev_w_out.shape[1] - fw
    h = _rmsnorm("norm_mix0", xr, norm_mix_g[0], BF16, mod=(sh1, sc1), seg_of_tile=seg_of_tile)[0]
    proj = _linear("even_in", h, ev_w_in[0], F32)
    ab = _group_dft(proj, fw)
    filt_args = (hy_pe_w[0], hy_pe_b[0], hy_w1[0], hy_b1[0], hy_w2[0], hy_b2[0], hy_w_out[0], hy_freq[0])
    for tag, row0, length in (("lat", 0, seq), ("ctx", lat_rows, ctx_len)):
        zf = _position_dft(f"fourier_{tag}", ab, row0, nb, length, fw)
        uc = _short_conv(f"short_conv_{tag}", proj, fw, (HY_ORDER + 1) * hw, row0, nb, length,
                         hy_conv_w[0], hy_conv_b[0])
        filt = _hyena_filter_sums(length, hw, *filt_args)
        zh = _hyena_stream(tag, uc, nb, length, hw, filt, hy_skip[0])
        xr = _mixer_out(f"even_out_{tag}", xr, row0, nb * length, zf, zh, ev_w_out[0], g1, seg_of_tile)
    h2 = _rmsnorm("norm_ffn0", xr, norm_ffn_g[0], BF16, mod=(sh2, sc2), seg_of_tile=seg_of_tile)[0]
    u = _ffn_up(h2, ffn_w1[0], ffn_w3[0])
    xr = _down_residual("ffn_down", xr, rows, u, ffn_w2[0].astype(BF16), g2, seg_of_tile,
                        tm_pref=512, tk_pref=u.shape[1], tn_pref=512)

    sh1, sc1, g1, sh2, sc2, g2 = mods[1]
    w_in = od_w_in[0]
    na_w = od_w_out.shape[1] // 2
    heads = na_w // HEAD_DIM
    q_rank = mla_w_uq.shape[1]
    kv_rank = mla_w_ukv.shape[1]
    mla_heads = (od_w_out.shape[1] - na_w) // HEAD_DIM
    nope = mla_w_uq.shape[2] // mla_heads - MLA_ROPE
    vdim = mla_w_ukv.shape[2] // mla_heads - nope
    main_cols = 3 * na_w + q_rank + kv_rank
    h = _rmsnorm("norm_mix1", xr, norm_mix_g[1], BF16, mod=(sh1, sc1), seg_of_tile=seg_of_tile)[0]
    proj = _linear("odd_in", h, w_in, F32, n_cols=main_cols)
    w_kr = w_in[:, main_cols:main_cols + MLA_ROPE]
    kr = _linear("odd_in_rope", h, jnp.concatenate([w_kr, _rotate_partner_columns(w_kr)], axis=1), F32)
    cq = _rmsnorm("norm_q", proj, mla_q_g[0], BF16, rows=lat_rows, width=q_rank, col_block=3 * na_w // q_rank)[0]
    ckv = _rmsnorm("norm_kv", proj, mla_kv_g[0], BF16, width=kv_rank, col_block=(3 * na_w + q_rank) // kv_rank)[0]
    w_uq = mla_w_uq[0].reshape(q_rank, mla_heads, nope + MLA_ROPE)
    w_uq = jnp.concatenate([w_uq, jnp.stack([_rotate_partner_columns(w_uq[:, hh, nope:])
                                             for hh in range(mla_heads)], axis=1)], axis=-1)
    q = _linear("mla_q", cq, w_uq.reshape(q_rank, mla_heads * (nope + 2 * MLA_ROPE)), F32)
    kv = _linear("mla_kv", ckv, mla_w_ukv[0], F32)
    z_na = _neighbourhood_attention(proj, nb, seq, ctx_len, heads, na_rpb[0])
    z_mla = _latent_attention(q, kv, kr, nb, seq, ctx_len, mla_heads, nope, vdim)
    xr = _mixer_out("odd_out", xr, 0, lat_rows, z_na, z_mla, od_w_out[0], g1, seg_of_tile)
    h2, comb = _rmsnorm("norm_ffn1", xr, norm_ffn_g[1], BF16, rows=lat_rows, mod=(sh2, sc2),
                        seg_of_tile=seg_of_tile, router=(moe_router_w[0], moe_router_b[0]))
    ne, _, f_exp = moe_w1.shape[1:]
    u = _moe_up(h2, lat_rows, comb, moe_w1[0], moe_w3[0])
    xr = _down_residual("moe_down", xr, lat_rows, u, moe_w2[0].reshape(ne * f_exp, d), g2, seg_of_tile,
                        tm_pref=1024, tk_pref=2048, tn_pref=512)
    out = _rmsnorm("norm_final", xr, final_g, F32, rows=lat_rows)[0]
    return out.reshape(nb, seq, d)
```

```python
import functools
import math

import jax
import jax.numpy as jnp
import numpy as np
from jax import lax
from jax.experimental import pallas as pl
from jax.experimental.pallas import tpu as pltpu

F32 = jnp.float32
BF16 = jnp.bfloat16

GRID_W = 64
HEAD_DIM = 128
N_MOD = 6
HY_ORDER = 2
HY_POS_BANDS = 16
HY_DECAY_TARGET = 1e-2
HY_FAST_DECAY = 0.3
HY_SLOW_DECAY = 1.5
NA_WIN_R = 8
NA_WIN_C = 16
MLA_ROPE = 64
ROPE_BASE = 10000.0
TOP_K = 2
EPS = 1e-6

LANES = 128
MOD_ROWS = 8
V7X_VMEM_LIMIT = 58 * 2 ** 20
NEG_BIG = -1e30
ROUTE_EXPERT = 0
ROUTE_GATE = TOP_K
MOE_TILE = 512
MOE_COMBINE_TILE = 256


def _tile(n, pref):
    if n <= pref:
        return n
    t = (pref // LANES) * LANES
    while t >= LANES:
        if n % t == 0:
            return t
        t -= LANES
    raise ValueError(f"no lane-aligned tile for {n} <= {pref}")


def _params(n_grid):
    return pltpu.CompilerParams(dimension_semantics=("arbitrary",) * n_grid,
                                vmem_limit_bytes=V7X_VMEM_LIMIT)


def _bf(a):
    return a if a.dtype == BF16 else a.astype(BF16)


def _dot(a, b):
    return jnp.dot(_bf(a), _bf(b), preferred_element_type=F32)


def _dot_nt(a, b):
    return lax.dot_general(_bf(a), _bf(b), (((1,), (1,)), ((), ())), preferred_element_type=F32)


def _split(a):
    hi = a.astype(BF16)
    lo = (a - hi.astype(F32)).astype(BF16)
    return hi, lo


def _dot3(a, b):
    ah, al = _split(a)
    bh, bl = _split(b)
    return (jnp.dot(ah, bh, preferred_element_type=F32) + jnp.dot(ah, bl, preferred_element_type=F32)
            + jnp.dot(al, bh, preferred_element_type=F32))


def _silu(a):
    return a * (1.0 / (1.0 + jnp.exp(-a)))


def _mm_body(*refs, n_x, n_w, n_e, n_o, groups, nk, epilogue, n_grid):
    xs = refs[:n_x]
    ws = refs[n_x:n_x + n_w]
    es = refs[n_x + n_w:n_x + n_w + n_e]
    outs = refs[n_x + n_w + n_e:n_x + n_w + n_e + n_o]
    accs = refs[n_x + n_w + n_e + n_o:]
    pids = [pl.program_id(a) for a in range(n_grid)]
    xv = [_bf(x[...]) for x in xs]
    wv = [_bf(w[...]) for w in ws]
    parts = []
    for group in groups:
        p = None
        for xi, wi in group:
            d = jnp.dot(xv[xi], wv[wi], preferred_element_type=F32)
            p = d if p is None else p + d
        parts.append(p)
    if nk == 1:
        epilogue(parts, es, outs, pids)
        return
    k = pids[-1]

    @pl.when(k == 0)
    def _():
        for a, p in zip(accs, parts):
            a[...] = p

    @pl.when(k > 0)
    def _():
        for a, p in zip(accs, parts):
            a[...] += p

    @pl.when(k == nk - 1)
    def _():
        epilogue([a[...] for a in accs], es, outs, pids)


def _mm(name, grid, xs, x_specs, ws, w_specs, es, e_specs, out_shapes, out_specs, groups, epilogue,
        nk=1, acc_block=None, aliases=None):
    body = functools.partial(_mm_body, n_x=len(xs), n_w=len(ws), n_e=len(es), n_o=len(out_shapes),
                             groups=groups, nk=nk, epilogue=epilogue, n_grid=len(grid))
    scratch = [pltpu.VMEM(acc_block, F32) for _ in groups] if nk > 1 else []
    res = pl.pallas_call(
        body, grid=grid, in_specs=list(x_specs) + list(w_specs) + list(e_specs),
        out_specs=list(out_specs), out_shape=list(out_shapes), scratch_shapes=scratch,
        input_output_aliases=aliases or {}, compiler_params=_params(len(grid)), name=name,
    )(*xs, *ws, *es)
    return res


def _store0(parts, es, outs, pids):
    outs[0][...] = parts[0].astype(outs[0].dtype)


def _linear(name, x, w, out_dtype, *, n_cols=None, tm_pref=1024, tn_pref=512):
    m, k = x.shape
    n = n_cols or w.shape[1]
    tm, tn = _tile(m, tm_pref), _tile(n, tn_pref)
    return _mm(name, (m // tm, n // tn),
               [x], [pl.BlockSpec((tm, k), lambda i, j: (i, 0))],
               [w], [pl.BlockSpec((k, tn), lambda i, j: (0, j))],
               [], [], [jax.ShapeDtypeStruct((m, n), out_dtype)],
               [pl.BlockSpec((tm, tn), lambda i, j: (i, j))], [[(0, 0)]], _store0)[0]


def _mod_body(c_ref, w_ref, b_ref, o_ref):
    o_ref[...] = _dot(_silu(c_ref[...]), w_ref[...]) + b_ref[...]


def _modulation(c_all, mod_w, mod_b):
    depth, d, n = mod_w.shape
    tn = _tile(n, 512)
    return pl.pallas_call(
        _mod_body, grid=(depth, n // tn),
        in_specs=[pl.BlockSpec((MOD_ROWS, d), lambda l, j: (0, 0)),
                  pl.BlockSpec((None, d, tn), lambda l, j: (l, 0, j)),
                  pl.BlockSpec((None, 1, tn), lambda l, j: (l, 0, j))],
        out_specs=pl.BlockSpec((None, MOD_ROWS, tn), lambda l, j: (l, 0, j)),
        out_shape=jax.ShapeDtypeStruct((depth, MOD_ROWS, n), F32),
        compiler_params=_params(2), name="modulation",
    )(c_all, mod_w, mod_b.reshape(depth, 1, n))


def _norm_body(*refs, modulate, router, n_experts):
    it = iter(refs)
    x_ref, g_ref = next(it), next(it)
    x = x_ref[...]
    y = x * lax.rsqrt(jnp.mean(x * x, axis=-1, keepdims=True) + EPS) * g_ref[...]
    if modulate:
        sh_ref, sc_ref = next(it), next(it)
        y = y * (1.0 + sc_ref[...]) + sh_ref[...]
    if router:
        rw_ref, rb_ref = next(it), next(it)
    o_ref = next(it)
    o_ref[...] = y.astype(o_ref.dtype)
    if router:
        comb_ref = next(it)
        logits = _dot3(y, rw_ref[...]) + rb_ref[...]
        lane = lax.broadcasted_iota(jnp.int32, logits.shape, 1).astype(F32)
        neg = jnp.float32(-jnp.inf)
        logits = jnp.where(lane < n_experts, logits, neg)
        m1 = jnp.max(logits, axis=-1, keepdims=True)
        i1 = jnp.min(jnp.where(logits == m1, lane, float(LANES)), axis=-1, keepdims=True)
        rest = jnp.where(lane == i1, neg, logits)
        m2 = jnp.max(rest, axis=-1, keepdims=True)
        i2 = jnp.min(jnp.where(rest == m2, lane, float(LANES)), axis=-1, keepdims=True)
        e2 = jnp.exp(m2 - m1)
        g1 = 1.0 / (1.0 + e2)
        g2 = e2 / (1.0 + e2)
        comb_ref[...] = (jnp.where(lane == ROUTE_EXPERT, i1, 0.0) + jnp.where(lane == ROUTE_EXPERT + 1, i2, 0.0)
                         + jnp.where(lane == ROUTE_GATE, g1, 0.0) + jnp.where(lane == ROUTE_GATE + 1, g2, 0.0))


def _rmsnorm(name, x, gain, out_dtype, *, rows=None, width=None, col_block=0, mod=None, seg_of_tile=None,
             router=None, tr=256):
    rows = rows or x.shape[0]
    width = width or x.shape[1]
    tr = _tile(rows, tr)
    in_specs = [pl.BlockSpec((tr, width), lambda i: (i, col_block)),
                pl.BlockSpec((1, width), lambda i: (0, 0))]
    args = [x, gain.reshape(1, width)]
    if mod is not None:
        seg = seg_of_tile(tr)
        for t in mod:
            in_specs.append(pl.BlockSpec((None, 1, width), lambda i: (seg(i), 0, 0)))
            args.append(t)
    out_shape = [jax.ShapeDtypeStruct((rows, width), out_dtype)]
    out_specs = [pl.BlockSpec((tr, width), lambda i: (i, 0))]
    n_experts = 0
    if router is not None:
        rw, rb = router
        n_experts = rw.shape[1]
        rw_p = jnp.pad(rw, ((0, 0), (0, LANES - n_experts)))
        rb_p = jnp.pad(rb, (0, LANES - n_experts)).reshape(1, LANES)
        in_specs += [pl.BlockSpec((width, LANES), lambda i: (0, 0)), pl.BlockSpec((1, LANES), lambda i: (0, 0))]
        args += [rw_p, rb_p]
        out_shape.append(jax.ShapeDtypeStruct((rows, LANES), F32))
        out_specs.append(pl.BlockSpec((tr, LANES), lambda i: (i, 0)))
    body = functools.partial(_norm_body, modulate=mod is not None, router=router is not None, n_experts=n_experts)
    return pl.pallas_call(body, grid=(rows // tr,), in_specs=in_specs, out_specs=out_specs, out_shape=out_shape,
                          compiler_params=_params(1), name=name)(*args)


def _group_dft_body(p_ref, cs_ref, o_ref, *, groups):
    cs = cs_ref[...]
    fw = groups * HEAD_DIM
    for g in range(groups):
        r = _dot(p_ref[:, g * HEAD_DIM:(g + 1) * HEAD_DIM], cs)
        o_ref[:, g * HEAD_DIM:(g + 1) * HEAD_DIM] = r[:, :HEAD_DIM].astype(o_ref.dtype)
        o_ref[:, fw + g * HEAD_DIM:fw + (g + 1) * HEAD_DIM] = r[:, HEAD_DIM:].astype(o_ref.dtype)


def _group_dft(proj, fw):
    rows = proj.shape[0]
    k = np.arange(HEAD_DIM)
    ang = (2.0 * np.pi / HEAD_DIM) * ((k[:, None] * k[None, :]) % HEAD_DIM)
    cs = jnp.asarray(np.concatenate([np.cos(ang), np.sin(ang)], axis=1), dtype=BF16)
    tm = _tile(rows, 512)
    body = functools.partial(_group_dft_body, groups=fw // HEAD_DIM)
    return pl.pallas_call(
        body, grid=(rows // tm,),
        in_specs=[pl.BlockSpec((tm, fw), lambda i: (i, 0)),
                  pl.BlockSpec((HEAD_DIM, 2 * HEAD_DIM), lambda i: (0, 0))],
        out_specs=pl.BlockSpec((tm, 2 * fw), lambda i: (i, 0)),
        out_shape=jax.ShapeDtypeStruct((rows, 2 * fw), BF16),
        compiler_params=_params(1), name="group_dft",
    )(proj, cs)


def _angles(num, den):
    return (num % den).astype(F32) * (2.0 * math.pi / den)


def _position_dft(name, ab, row0, nb, length, fw):
    f = jnp.arange(length, dtype=jnp.int32)
    ang = _angles(f[:, None] * f[None, :], length)
    norm = 1.0 / math.sqrt(length * HEAD_DIM)
    cs = (jnp.concatenate([jnp.cos(ang), -jnp.sin(ang)], axis=1) * norm).astype(BF16)
    tm, tn = _tile(length, 1024), _tile(fw, 512)
    nm, nn = length // tm, fw // tn
    rb0 = row0 // length
    return _mm(name, (nb, nm, nn, 2),
               [cs], [pl.BlockSpec((tm, length), lambda b, i, j, k: (i, k))],
               [ab], [pl.BlockSpec((length, tn), lambda b, i, j, k: (rb0 + b, k * nn + j))],
               [], [], [jax.ShapeDtypeStruct((nb * length, fw), BF16)],
               [pl.BlockSpec((tm, tn), lambda b, i, j, k: (b * nm + i, j))],
               [[(0, 0)]], _store0, nk=2, acc_block=(tm, tn))[0]


def _short_conv_body(p_ref, w_ref, b_ref, o_ref):
    u = p_ref[...]
    n = u.shape[0]
    row = lax.broadcasted_iota(jnp.int32, u.shape, 0)
    prev = jnp.where(row == 0, 0.0, pltpu.roll(u, 1, 0))
    nxt = jnp.where(row == n - 1, 0.0, pltpu.roll(u, n - 1, 0))
    w = w_ref[...]
    o_ref[...] = w[0:1] * prev + w[1:2] * u + w[2:3] * nxt + b_ref[...]


def _short_conv(name, proj, col0, width, row0, nb, length, conv_w, conv_b):
    tc = _tile(math.gcd(col0, width), 512)
    cb0, rb0 = col0 // tc, row0 // length
    return pl.pallas_call(
        _short_conv_body, grid=(nb, width // tc),
        in_specs=[pl.BlockSpec((length, tc), lambda b, j: (rb0 + b, cb0 + j)),
                  pl.BlockSpec((3, tc), lambda b, j: (0, j)),
                  pl.BlockSpec((1, tc), lambda b, j: (0, j))],
        out_specs=pl.BlockSpec((length, tc), lambda b, j: (b, j)),
        out_shape=jax.ShapeDtypeStruct((nb * length, width), F32),
        compiler_params=_params(2), name=name,
    )(proj, conv_w, conv_b.reshape(1, width))


def _filter_mlp_body(z_ref, pw_ref, pb_ref, w1_ref, b1_ref, w2_ref, b2_ref, fr_ref, o_ref):
    fr = fr_ref[...]
    h = jnp.sin(fr * (_dot3(z_ref[...], pw_ref[...]) + pb_ref[...]))
    h = jnp.sin(fr * (_dot3(h, w1_ref[...]) + b1_ref[...]))
    o_ref[...] = jnp.sin(fr * (_dot3(h, w2_ref[...]) + b2_ref[...]))


def _filter_out_body(h_ref, wf_ref, wb_ref, d_ref, o_ref):
    h = h_ref[...]
    d = d_ref[...]
    fwd = _dot3(h, wf_ref[...]) * d
    bwd = _dot3(h, wb_ref[...]) * d
    row = lax.broadcasted_iota(jnp.int32, bwd.shape, 0)
    bwd = jnp.where(row == 0, 0.0, bwd)
    o_ref[0] = fwd + bwd
    o_ref[1] = fwd - bwd


def _hyena_filter_sums(length, hw, pe_w, pe_b, w1, b1, w2, b2, w_out, freq):
    t = jnp.linspace(0.0, 1.0, length, dtype=F32)[:, None]
    w = (2.0 * math.pi / length) * jnp.arange(length, dtype=F32)[:, None]
    bands = jnp.linspace(1e-4, HY_POS_BANDS - 1, HY_POS_BANDS, dtype=F32)[None, :]
    z = jnp.concatenate([t, jnp.cos(bands * w), -jnp.sin(bands * w)], axis=-1)
    pos_dim, fwid = pe_w.shape
    z = jnp.pad(z, ((0, 0), (0, LANES - pos_dim)))
    pe_w = jnp.pad(pe_w, ((0, LANES - pos_dim), (0, 0)))
    row = lambda v: v.reshape(1, fwid)
    full = lambda a: pl.BlockSpec(a.shape, lambda i: (0,) * a.ndim)
    args = [z, pe_w, row(pe_b), w1, row(b1), w2, row(b2), row(freq)]
    h = pl.pallas_call(
        _filter_mlp_body, grid=(1,), in_specs=[full(a) for a in args],
        out_specs=pl.BlockSpec((length, fwid), lambda i: (0, 0)),
        out_shape=jax.ShapeDtypeStruct((length, fwid), F32),
        compiler_params=_params(1), name="hyena_filter_mlp",
    )(*args)
    max_decay = math.log(HY_DECAY_TARGET) / HY_FAST_DECAY
    min_decay = math.log(HY_DECAY_TARGET) / HY_SLOW_DECAY
    deltas = jnp.abs(jnp.linspace(min_decay, max_decay, hw, dtype=F32))
    decay = jnp.exp(-t * deltas)
    tc = _tile(hw, 256)
    nc = hw // tc
    return pl.pallas_call(
        _filter_out_body, grid=(HY_ORDER, nc),
        in_specs=[pl.BlockSpec((length, fwid), lambda n, j: (0, 0)),
                  pl.BlockSpec((fwid, tc), lambda n, j: (0, (2 * n) * nc + j)),
                  pl.BlockSpec((fwid, tc), lambda n, j: (0, (2 * n + 1) * nc + j)),
                  pl.BlockSpec((length, tc), lambda n, j: (0, j))],
        out_specs=pl.BlockSpec((2, length, tc), lambda n, j: (0, 0, n * nc + j)),
        out_shape=jax.ShapeDtypeStruct((2, length, HY_ORDER * hw), F32),
        compiler_params=_params(2), name="hyena_filter_out",
    )(h, w_out, w_out, decay)


def _odd_dft_matrix(length):
    f = jnp.arange(length, dtype=jnp.int32)
    ang = _angles((2 * f[:, None] + 1) * f[None, :], 4 * length)
    return jnp.concatenate([jnp.cos(ang), jnp.sin(ang)], axis=0)


def _hyena_spectrum_epilogue(parts, es, outs, pids):
    ur, us = parts
    kr, ks = es[0][0], es[0][1]
    outs[0][0] = (ur * kr - us * ks).astype(outs[0].dtype)
    outs[0][1] = (ur * ks + us * kr).astype(outs[0].dtype)


def _hyena_gate_epilogue(parts, es, outs, pids):
    gate_ref, z_ref, skip_ref = es
    outs[0][...] = (gate_ref[...] * (parts[0] + z_ref[...] * skip_ref[...])).astype(outs[0].dtype)


def _hyena_stream(tag, uc, nb, length, hw, filt, skip):
    fo = _odd_dft_matrix(length)
    fo_b = fo.astype(BF16)
    go_b = (fo.T * (1.0 / length)).astype(BF16)
    tm, tn = _tile(length, 1024), _tile(hw, 512)
    nm, nn = length // tm, hw // tn
    kspec = _mm(f"hyena_kspec_{tag}", (2 * nm, HY_ORDER * nn),
                [fo_b], [pl.BlockSpec((tm, length), lambda i, j: (i, 0))],
                [filt.reshape(2 * length, HY_ORDER * hw)],
                [pl.BlockSpec((length, tn), lambda i, j: (i // nm, j))],
                [], [], [jax.ShapeDtypeStruct((2 * length, HY_ORDER * hw), F32)],
                [pl.BlockSpec((tm, tn), lambda i, j: (i, j))], [[(0, 0)]], _store0)[0]
    kspec = kspec.reshape(2, length, HY_ORDER * hw)
    z = uc
    for n in range(HY_ORDER):
        z = _hyena_order(tag, n, uc, z, nb, length, hw, fo_b, go_b, kspec, skip, (tm, tn, nm, nn))
    return z


def _hyena_order(tag, n, uc, z, nb, length, hw, fo_b, go_b, kspec, skip, tiles):
    tm, tn, nm, nn = tiles
    gate_cb = (n + 1) * nn
    spec = _mm(f"hyena_fwd{n}_{tag}", (nm, nb, nn),
               [fo_b, fo_b], [pl.BlockSpec((tm, length), lambda i, b, j: (i, 0)),
                              pl.BlockSpec((tm, length), lambda i, b, j: (nm + i, 0))],
               [z], [pl.BlockSpec((length, tn), lambda i, b, j: (b, j))],
               [kspec], [pl.BlockSpec((2, tm, tn), lambda i, b, j: (0, i, n * nn + j))],
               [jax.ShapeDtypeStruct((nb, 2, length, hw), BF16)],
               [pl.BlockSpec((None, 2, tm, tn), lambda i, b, j: (b, 0, i, j))],
               [[(0, 0)], [(1, 0)]], _hyena_spectrum_epilogue)[0]
    last = n == HY_ORDER - 1
    return _mm(f"hyena_inv{n}_{tag}", (nb, nm, nn),
               [go_b], [pl.BlockSpec((tm, 2 * length), lambda b, i, j: (i, 0))],
               [spec.reshape(nb * 2 * length, hw)],
               [pl.BlockSpec((2 * length, tn), lambda b, i, j: (b, j))],
               [uc, z, skip.reshape(HY_ORDER, 1, hw)],
               [pl.BlockSpec((tm, tn), lambda b, i, j: (b * nm + i, gate_cb + j)),
                pl.BlockSpec((tm, tn), lambda b, i, j: (b * nm + i, j)),
                pl.BlockSpec((None, 1, tn), lambda b, i, j: (n, 0, j))],
               [jax.ShapeDtypeStruct((nb * length, hw), BF16 if last else F32)],
               [pl.BlockSpec((tm, tn), lambda b, i, j: (b * nm + i, j))],
               [[(0, 0)]], _hyena_gate_epilogue)[0]


def _resgate_epilogue(parts, es, outs, pids):
    res_ref, gate_ref = es
    outs[0][...] = res_ref[...] + gate_ref[...] * parts[0]


def _mixer_out(name, x_res, row0, rows, za, zb, w_out, gate, seg_of_tile):
    d = x_res.shape[1]
    half = za.shape[1]
    tm, tn = _tile(rows, 1024), _tile(d, 512)
    rb0 = row0 // tm
    seg = seg_of_tile(tm)
    return _mm(name, (rows // tm, d // tn),
               [za, zb], [pl.BlockSpec((tm, half), lambda i, j: (i, 0))] * 2,
               [w_out, w_out], [pl.BlockSpec((half, tn), lambda i, j: (0, j)),
                                pl.BlockSpec((half, tn), lambda i, j: (1, j))],
               [x_res, gate], [pl.BlockSpec((tm, tn), lambda i, j: (rb0 + i, j)),
                               pl.BlockSpec((None, 1, tn), lambda i, j: (seg(rb0 + i), 0, j))],
               [jax.ShapeDtypeStruct(x_res.shape, F32)],
               [pl.BlockSpec((tm, tn), lambda i, j: (rb0 + i, j))],
               [[(0, 0), (1, 1)]], _resgate_epilogue, aliases={4: 0})[0]


def _swiglu_epilogue(parts, es, outs, pids):
    outs[0][...] = (_silu(parts[0]) * parts[1]).astype(outs[0].dtype)


def _ffn_up(h, w1, w3):
    m, k = h.shape
    n = w1.shape[1]
    tm, tn = _tile(m, 1024), _tile(n, 256)
    return _mm("ffn_up", (m // tm, n // tn),
               [h], [pl.BlockSpec((tm, k), lambda i, j: (i, 0))],
               [w1, w3], [pl.BlockSpec((k, tn), lambda i, j: (0, j))] * 2,
               [], [], [jax.ShapeDtypeStruct((m, n), BF16)],
               [pl.BlockSpec((tm, tn), lambda i, j: (i, j))],
               [[(0, 0)], [(0, 1)]], _swiglu_epilogue)[0]


def _down_residual(name, x_res, rows, u, w2, gate, seg_of_tile, tm_pref, tk_pref, tn_pref):
    d = x_res.shape[1]
    kdim = u.shape[1]
    tm, tn, tk = _tile(rows, tm_pref), _tile(d, tn_pref), _tile(kdim, tk_pref)
    nk = kdim // tk
    seg = seg_of_tile(tm)
    return _mm(name, (rows // tm, d // tn, nk),
               [u], [pl.BlockSpec((tm, tk), lambda i, j, k: (i, k))],
               [w2], [pl.BlockSpec((tk, tn), lambda i, j, k: (k, j))],
               [x_res, gate], [pl.BlockSpec((tm, tn), lambda i, j, k: (i, j)),
                               pl.BlockSpec((None, 1, tn), lambda i, j, k: (seg(i), 0, j))],
               [jax.ShapeDtypeStruct(x_res.shape, F32)],
               [pl.BlockSpec((tm, tn), lambda i, j, k: (i, j))],
               [[(0, 0)]], _resgate_epilogue, nk=nk, acc_block=(tm, tn), aliases={2: 0})[0]


def _moe_plan(route, n_experts):
    t = route.shape[0]
    n_tiles = TOP_K * t // MOE_TILE + n_experts
    expert = route[:, ROUTE_EXPERT:ROUTE_EXPERT + TOP_K].astype(jnp.int32).T.reshape(-1)
    onehot = (expert[:, None] == jnp.arange(n_experts, dtype=jnp.int32)[None, :]).astype(jnp.int32)
    running = jnp.cumsum(onehot, axis=0)
    rank = jnp.take_along_axis(running, expert[:, None], axis=1)[:, 0] - 1
    tiles_per_expert = (running[-1] + MOE_TILE - 1) // MOE_TILE
    tile_end = jnp.cumsum(tiles_per_expert)
    row = (tile_end - tiles_per_expert)[expert] * MOE_TILE + rank
    token = jnp.tile(jnp.arange(t, dtype=jnp.int32), TOP_K)
    row_token = jnp.zeros((n_tiles * MOE_TILE,), jnp.int32).at[row].set(token)
    tile_expert = jnp.searchsorted(tile_end, jnp.arange(n_tiles, dtype=jnp.int32), side="right")
    tile_expert = jnp.minimum(tile_expert, n_experts - 1).astype(jnp.int32)
    pos = row.reshape(TOP_K, t // MOE_COMBINE_TILE, MOE_COMBINE_TILE).transpose(1, 0, 2)
    return row_token.reshape(n_tiles, 1, MOE_TILE), tile_expert, tile_end[-1:].astype(jnp.int32), pos


def _row_copy(src_hbm, row, dst_ref, dst_row, sem):
    return pltpu.make_async_copy(src_hbm.at[pl.ds(row, 1), :], dst_ref.at[pl.ds(dst_row, 1), :], sem)


def _moe_gather_body(tok_ref, src_hbm, o_ref, buf_ref, sem):
    rows = buf_ref.shape[0]

    def issue(r, carry):
        _row_copy(src_hbm, tok_ref[0, r], buf_ref, r, sem).start()
        return carry

    def drain(r, carry):
        _row_copy(src_hbm, 0, buf_ref, r, sem).wait()
        return carry

    lax.fori_loop(0, rows, issue, 0)
    lax.fori_loop(0, rows, drain, 0)
    o_ref[...] = buf_ref[...].astype(o_ref.dtype)


def _moe_gather(h, row_token):
    n_tiles = row_token.shape[0]
    d = h.shape[1]
    return pl.pallas_call(
        _moe_gather_body, grid=(n_tiles,),
        in_specs=[pl.BlockSpec((None, 1, MOE_TILE), lambda i: (i, 0, 0), memory_space=pltpu.SMEM),
                  pl.BlockSpec(memory_space=pl.ANY)],
        out_specs=pl.BlockSpec((MOE_TILE, d), lambda i: (i, 0)),
        out_shape=jax.ShapeDtypeStruct((n_tiles * MOE_TILE, d), BF16),
        scratch_shapes=[pltpu.VMEM((MOE_TILE, d), F32), pltpu.SemaphoreType.DMA(())],
        compiler_params=_params(1), name="moe_gather",
    )(row_token, h)


def _moe_up_body(te_ref, na_ref, x_ref, w1_ref, w3_ref, o_ref):
    active = pl.program_id(1) < na_ref[0]

    @pl.when(active)
    def _():
        x = x_ref[...]
        o_ref[...] = (_silu(_dot(x, w1_ref[...])) * _dot(x, w3_ref[...])).astype(o_ref.dtype)

    @pl.when(jnp.logical_not(active))
    def _():
        o_ref[...] = jnp.zeros(o_ref.shape, o_ref.dtype)


def _moe_down_body(te_ref, na_ref, u_ref, w2_ref, o_ref):
    active = pl.program_id(1) < na_ref[0]

    @pl.when(active)
    def _():
        o_ref[...] = _dot(u_ref[...], w2_ref[...])

    @pl.when(jnp.logical_not(active))
    def _():
        o_ref[...] = jnp.zeros(o_ref.shape, o_ref.dtype)


def _moe_experts(xs, tile_expert, n_active, w1, w3, w2):
    ne, d, f = w1.shape
    n_tiles = xs.shape[0] // MOE_TILE
    tn = _tile(f, 512)
    up = pl.pallas_call(
        _moe_up_body,
        grid_spec=pltpu.PrefetchScalarGridSpec(
            num_scalar_prefetch=2, grid=(f // tn, n_tiles),
            in_specs=[pl.BlockSpec((MOE_TILE, d), lambda j, i, te, na: (i, 0)),
                      pl.BlockSpec((None, d, tn), lambda j, i, te, na: (te[i], 0, j)),
                      pl.BlockSpec((None, d, tn), lambda j, i, te, na: (te[i], 0, j))],
            out_specs=pl.BlockSpec((MOE_TILE, tn), lambda j, i, te, na: (i, j))),
        out_shape=jax.ShapeDtypeStruct((xs.shape[0], f), BF16),
        compiler_params=_params(2), name="moe_up",
    )(tile_expert, n_active, xs, w1, w3)
    tn = _tile(d, 1024)
    return pl.pallas_call(
        _moe_down_body,
        grid_spec=pltpu.PrefetchScalarGridSpec(
            num_scalar_prefetch=2, grid=(d // tn, n_tiles),
            in_specs=[pl.BlockSpec((MOE_TILE, f), lambda j, i, te, na: (i, 0)),
                      pl.BlockSpec((None, f, tn), lambda j, i, te, na: (te[i], 0, j))],
            out_specs=pl.BlockSpec((MOE_TILE, tn), lambda j, i, te, na: (i, j))),
        out_shape=jax.ShapeDtypeStruct((xs.shape[0], d), F32),
        compiler_params=_params(2), name="moe_down",
    )(tile_expert, n_active, up, w2)


def _moe_combine_body(pos_ref, ys_hbm, res_ref, route_ref, gate_ref, o_ref, buf_ref, sem):
    rows = res_ref.shape[0]

    def issue(r, carry):
        for k in range(TOP_K):
            _row_copy(ys_hbm, pos_ref[k, r], buf_ref.at[k], r, sem).start()
        return carry

    def drain(r, carry):
        for k in range(TOP_K):
            _row_copy(ys_hbm, 0, buf_ref.at[k], r, sem).wait()
        return carry

    lax.fori_loop(0, rows, issue, 0)
    lax.fori_loop(0, rows, drain, 0)
    route = route_ref[...]
    lane = lax.broadcasted_iota(jnp.int32, route.shape, 1)
    moe = None
    for k in range(TOP_K):
        g = jnp.sum(jnp.where(lane == ROUTE_GATE + k, route, 0.0), axis=-1, keepdims=True)
        term = g * buf_ref[k]
        moe = term if moe is None else moe + term
    o_ref[...] = res_ref[...] + gate_ref[...] * moe


def _moe_combine(x_res, rows, ys, pos, route, gate, seg_of_tile):
    d = x_res.shape[1]
    tm = MOE_COMBINE_TILE
    seg = seg_of_tile(tm)
    return pl.pallas_call(
        _moe_combine_body, grid=(rows // tm,),
        in_specs=[pl.BlockSpec((None, TOP_K, tm), lambda i: (i, 0, 0), memory_space=pltpu.SMEM),
                  pl.BlockSpec(memory_space=pl.ANY),
                  pl.BlockSpec((tm, d), lambda i: (i, 0)),
                  pl.BlockSpec((tm, LANES), lambda i: (i, 0)),
                  pl.BlockSpec((None, 1, d), lambda i: (seg(i), 0, 0))],
        out_specs=pl.BlockSpec((tm, d), lambda i: (i, 0)),
        out_shape=jax.ShapeDtypeStruct(x_res.shape, F32),
        scratch_shapes=[pltpu.VMEM((TOP_K, tm, d), F32), pltpu.SemaphoreType.DMA(())],
        input_output_aliases={2: 0},
        compiler_params=_params(1), name="moe_combine",
    )(pos, ys, x_res, route, gate)


NA_BLOCK_ROWS = 4


def _na_body(q_ref, k_ref, v_ref, kc_ref, vc_ref, bias_ref, o_ref, va_ref, *, seq, ctx_len, blocks, win):
    qscale = HEAD_DIM ** -0.5 * math.log2(math.e)
    bq = NA_BLOCK_ROWS * GRID_W
    va_ref[0:seq, 0:HEAD_DIM] = v_ref[...].astype(BF16)
    va_ref[seq:seq + ctx_len, 0:HEAD_DIM] = vc_ref[...].astype(BF16)
    lane = lax.broadcasted_iota(jnp.int32, (seq + ctx_len, va_ref.shape[1] - HEAD_DIM), 1)
    va_ref[:, HEAD_DIM:] = jnp.where(lane == 0, 1.0, 0.0).astype(BF16)
    kc = kc_ref[...].astype(BF16)
    for kb, (case, ws) in enumerate(blocks):
        q = (q_ref[kb * bq:(kb + 1) * bq, :] * qscale).astype(BF16)
        k0 = ws * GRID_W
        s_loc = _dot_nt(q, k_ref[k0:k0 + win, :]) + bias_ref[case]
        s_ctx = _dot_nt(q, kc)
        m = jnp.maximum(jnp.max(s_loc, axis=-1, keepdims=True), jnp.max(s_ctx, axis=-1, keepdims=True))
        acc = _dot(jnp.exp2(s_loc - m), va_ref[k0:k0 + win, :]) + _dot(jnp.exp2(s_ctx - m), va_ref[seq:seq + ctx_len, :])
        o_ref[kb * bq:(kb + 1) * bq, :] = (acc[:, :HEAD_DIM] / acc[:, HEAD_DIM:HEAD_DIM + 1]).astype(o_ref.dtype)


def _na_bias_slabs(rpb, grid_rows, win_rows):
    heads = rpb.shape[0]
    cols = np.arange(GRID_W)
    start = np.clip(cols - NA_WIN_C // 2, 0, GRID_W - NA_WIN_C)
    kc = cols[None, :]
    inside = (kc >= start[:, None]) & (kc < start[:, None] + NA_WIN_C)
    dc = np.clip(kc - cols[:, None] + (NA_WIN_C - 1), 0, 2 * NA_WIN_C - 2)
    band = jnp.where(jnp.asarray(inside)[None, None], rpb[:, :, dc], NEG_BIG)
    masked = band.shape[1]
    band = jnp.concatenate([band, jnp.full((heads, 1, GRID_W, GRID_W), NEG_BIG, F32)], axis=1)
    win = min(win_rows + NA_BLOCK_ROWS, grid_rows)
    tables, blocks = [], []
    for r0 in range(0, grid_rows, NA_BLOCK_ROWS):
        ws = int(np.clip(r0 - win_rows // 2, 0, grid_rows - win))
        idx = np.full((NA_BLOCK_ROWS, win), masked)
        for ri in range(NA_BLOCK_ROWS):
            r = r0 + ri
            rs = int(np.clip(r - win_rows // 2, 0, grid_rows - win_rows))
            assert ws <= rs and rs + win_rows <= ws + win
            for key_row in range(rs, rs + win_rows):
                idx[ri, key_row - ws] = key_row - r + (NA_WIN_R - 1)
        for case, t in enumerate(tables):
            if np.array_equal(t, idx):
                break
        else:
            case = len(tables)
            tables.append(idx)
        blocks.append((case, ws))
    slabs = band[:, np.stack(tables)]
    slabs = jnp.transpose(slabs, (0, 1, 2, 4, 3, 5))
    slabs = slabs.reshape(heads, len(tables), NA_BLOCK_ROWS * GRID_W, win * GRID_W) * math.log2(math.e)
    return slabs, tuple(blocks), win * GRID_W


def _neighbourhood_attention(proj, nb, seq, ctx_len, heads, rpb):
    grid_rows = seq // GRID_W
    assert grid_rows % NA_BLOCK_ROWS == 0
    win_rows = min(NA_WIN_R, grid_rows)
    bias, blocks, win = _na_bias_slabs(rpb, grid_rows, win_rows)
    cr0 = nb * seq // ctx_len
    body = functools.partial(_na_body, seq=seq, ctx_len=ctx_len, blocks=blocks, win=win)
    return pl.pallas_call(
        body, grid=(nb, heads),
        in_specs=[pl.BlockSpec((seq, HEAD_DIM), lambda b, h: (b, h)),
                  pl.BlockSpec((seq, HEAD_DIM), lambda b, h: (b, heads + h)),
                  pl.BlockSpec((seq, HEAD_DIM), lambda b, h: (b, 2 * heads + h)),
                  pl.BlockSpec((ctx_len, HEAD_DIM), lambda b, h: (cr0 + b, heads + h)),
                  pl.BlockSpec((ctx_len, HEAD_DIM), lambda b, h: (cr0 + b, 2 * heads + h)),
                  pl.BlockSpec((None,) + bias.shape[1:], lambda b, h: (h, 0, 0, 0))],
        out_specs=pl.BlockSpec((seq, HEAD_DIM), lambda b, h: (b, h)),
        out_shape=jax.ShapeDtypeStruct((nb * seq, heads * HEAD_DIM), BF16),
        scratch_shapes=[pltpu.VMEM((seq + ctx_len, HEAD_DIM + LANES), BF16)],
        compiler_params=_params(2), name="neighbourhood_attention",
    )(proj, proj, proj, proj, proj, bias)


def _mla_body(q_ref, kvl_ref, kvc_ref, krl_ref, krc_ref, cq_ref, sq_ref, ck_ref, sk_ref, o_ref,
              qf_ref, kf_ref, va_ref, *, nope, vdim, seq, ctx_len, chunks):
    qscale = (nope + MLA_ROPE) ** -0.5 * math.log2(math.e)
    rope_end = nope + MLA_ROPE

    @pl.when(pl.program_id(2) == 0)
    def _():
        krl = krl_ref[...]
        kf_ref[0:seq, 0:nope] = kvl_ref[:, 0:nope].astype(BF16)
        kf_ref[0:seq, nope:rope_end] = (krl[:, :MLA_ROPE] * ck_ref[...] + krl[:, MLA_ROPE:] * sk_ref[...]).astype(BF16)
        kf_ref[seq:seq + ctx_len, 0:nope] = kvc_ref[:, 0:nope].astype(BF16)
        kf_ref[seq:seq + ctx_len, nope:rope_end] = krc_ref[:, 0:MLA_ROPE].astype(BF16)
        kf_ref[:, rope_end:] = jnp.zeros((seq + ctx_len, kf_ref.shape[1] - rope_end), BF16)
        va_ref[0:seq, 0:vdim] = kvl_ref[:, nope:nope + vdim].astype(BF16)
        va_ref[seq:seq + ctx_len, 0:vdim] = kvc_ref[:, nope:nope + vdim].astype(BF16)
        lane = lax.broadcasted_iota(jnp.int32, (seq + ctx_len, va_ref.shape[1] - vdim), 1)
        va_ref[:, vdim:] = jnp.where(lane == 0, 1.0, 0.0).astype(BF16)

    q = q_ref[...]
    qf_ref[:, 0:nope] = (q[:, :nope] * qscale).astype(BF16)
    qr = q[:, nope:rope_end] * cq_ref[...] + q[:, rope_end:] * sq_ref[...]
    qf_ref[:, nope:rope_end] = (qr * qscale).astype(BF16)
    qf_ref[:, rope_end:] = jnp.zeros((q.shape[0], qf_ref.shape[1] - rope_end), BF16)
    qf = qf_ref[...]
    m = jnp.full((q.shape[0], 1), -jnp.inf, F32)
    acc = jnp.zeros((q.shape[0], va_ref.shape[1]), F32)
    for start, size in chunks:
        s = _dot_nt(qf, kf_ref[start:start + size, :])
        m_new = jnp.maximum(m, jnp.max(s, axis=-1, keepdims=True))
        p = jnp.exp2(s - m_new)
        acc = acc * jnp.exp2(m - m_new) + _dot(p, va_ref[start:start + size, :])
        m = m_new
    o_ref[...] = (acc[:, :vdim] / acc[:, vdim:vdim + 1]).astype(o_ref.dtype)


def _rope_tables(seq):
    pos = np.arange(seq)
    row = (pos // GRID_W).astype(np.float32)
    col = (pos % GRID_W).astype(np.float32)
    half = MLA_ROPE // 2
    inv = jnp.asarray(ROPE_BASE, F32) ** (-jnp.arange(0, half, 2, dtype=F32) / half)
    ang_r = jnp.asarray(row)[:, None] * inv
    ang_c = jnp.asarray(col)[:, None] * inv
    ang = jnp.concatenate([ang_r, ang_r, ang_c, ang_c], axis=-1)
    return jnp.cos(ang), jnp.sin(ang)


def _rotate_partner_columns(w):
    q = MLA_ROPE // 4
    parts = [w[:, i * q:(i + 1) * q] for i in range(4)]
    return jnp.concatenate([-parts[1], parts[0], -parts[3], parts[2]], axis=1)


def _latent_attention(q, kv, kr, nb, seq, ctx_len, heads, nope, vdim):
    cos, sin = _rope_tables(seq)
    tq = _tile(seq, 512)
    nq = seq // tq
    cr0 = nb * seq // ctx_len
    qw = nope + 2 * MLA_ROPE
    keys = seq + ctx_len
    chunk = _tile(seq, 1024)
    chunks = tuple((s, min(chunk, keys - s)) for s in range(0, keys, chunk))
    kw = nope + 2 * MLA_ROPE
    vw = vdim + LANES
    body = functools.partial(_mla_body, nope=nope, vdim=vdim, seq=seq, ctx_len=ctx_len, chunks=chunks)
    return pl.pallas_call(
        body, grid=(nb, heads, nq),
        scratch_shapes=[pltpu.VMEM((tq, kw), BF16), pltpu.VMEM((keys, kw), BF16), pltpu.VMEM((keys, vw), BF16)],
        in_specs=[pl.BlockSpec((tq, qw), lambda b, h, i: (b * nq + i, h)),
                  pl.BlockSpec((seq, nope + vdim), lambda b, h, i: (b, h)),
                  pl.BlockSpec((ctx_len, nope + vdim), lambda b, h, i: (cr0 + b, h)),
                  pl.BlockSpec((seq, 2 * MLA_ROPE), lambda b, h, i: (b, 0)),
                  pl.BlockSpec((ctx_len, 2 * MLA_ROPE), lambda b, h, i: (cr0 + b, 0)),
                  pl.BlockSpec((tq, MLA_ROPE), lambda b, h, i: (i, 0)),
                  pl.BlockSpec((tq, MLA_ROPE), lambda b, h, i: (i, 0)),
                  pl.BlockSpec((seq, MLA_ROPE), lambda b, h, i: (0, 0)),
                  pl.BlockSpec((seq, MLA_ROPE), lambda b, h, i: (0, 0))],
        out_specs=pl.BlockSpec((tq, vdim), lambda b, h, i: (b * nq + i, h)),
        out_shape=jax.ShapeDtypeStruct((nb * seq, heads * vdim), BF16),
        compiler_params=_params(3), name="latent_attention",
    )(q, kv, kv, kr, kr, cos, sin, cos, sin)


def kernel(x, c, ctx, c_ctx, mod_w, mod_b, norm_mix_g, norm_ffn_g, ev_w_in, ev_w_out, hy_conv_w, hy_conv_b, hy_pe_w, hy_pe_b, hy_w1, hy_b1, hy_w2, hy_b2, hy_w_out, hy_freq, hy_skip, ffn_w1, ffn_w3, ffn_w2, od_w_in, od_w_out, na_rpb, mla_q_g, mla_w_uq, mla_kv_g, mla_w_ukv, moe_router_w, moe_router_b, moe_w1, moe_w3, moe_w2, final_g):
    nb, seq, d = x.shape
    ctx_len = ctx.shape[1]
    depth = mod_w.shape[0]
    assert depth == 2 and ev_w_in.shape[0] == 1 and od_w_in.shape[0] == 1, "even layer then odd (last) layer"
    lat_rows, ctx_rows = nb * seq, nb * ctx_len
    rows = lat_rows + ctx_rows
    assert nb + 1 <= MOD_ROWS

    def seg_of_tile(tile_rows):
        assert seq % tile_rows == 0
        return lambda i: jnp.minimum((i * tile_rows) // seq, nb)

    xr = jnp.concatenate([x.reshape(lat_rows, d), ctx.reshape(ctx_rows, d)], axis=0)
    c_all = jnp.concatenate([c, c_ctx[None, :], jnp.zeros((MOD_ROWS - nb - 1, d), F32)], axis=0)
    mod = _modulation(c_all, mod_w, mod_b).reshape(depth, MOD_ROWS, N_MOD, 1, d)
    mods = [[mod[i, :, m] for m in range(N_MOD)] for i in range(depth)]

    sh1, sc1, g1, sh2, sc2, g2 = mods[0]
    fw = (ev_w_out.shape[1]) // 2
    hw = ev_w_out.shape[1] - fw
    h = _rmsnorm("norm_mix0", xr, norm_mix_g[0], BF16, mod=(sh1, sc1), seg_of_tile=seg_of_tile)[0]
    proj = _linear("even_in", h, ev_w_in[0], F32)
    ab = _group_dft(proj, fw)
    filt_args = (hy_pe_w[0], hy_pe_b[0], hy_w1[0], hy_b1[0], hy_w2[0], hy_b2[0], hy_w_out[0], hy_freq[0])
    for tag, row0, length in (("lat", 0, seq), ("ctx", lat_rows, ctx_len)):
        zf = _position_dft(f"fourier_{tag}", ab, row0, nb, length, fw)
        uc = _short_conv(f"short_conv_{tag}", proj, fw, (HY_ORDER + 1) * hw, row0, nb, length,
                         hy_conv_w[0], hy_conv_b[0])
        filt = _hyena_filter_sums(length, hw, *filt_args)
        zh = _hyena_stream(tag, uc, nb, length, hw, filt, hy_skip[0])
        xr = _mixer_out(f"even_out_{tag}", xr, row0, nb * length, zf, zh, ev_w_out[0], g1, seg_of_tile)
    h2 = _rmsnorm("norm_ffn0", xr, norm_ffn_g[0], BF16, mod=(sh2, sc2), seg_of_tile=seg_of_tile)[0]
    u = _ffn_up(h2, ffn_w1[0], ffn_w3[0])
    xr = _down_residual("ffn_down", xr, rows, u, ffn_w2[0].astype(BF16), g2, seg_of_tile,
                        tm_pref=512, tk_pref=u.shape[1], tn_pref=512)

    sh1, sc1, g1, sh2, sc2, g2 = mods[1]
    w_in = od_w_in[0]
    na_w = od_w_out.shape[1] // 2
    heads = na_w // HEAD_DIM
    q_rank = mla_w_uq.shape[1]
    kv_rank = mla_w_ukv.shape[1]
    mla_heads = (od_w_out.shape[1] - na_w) // HEAD_DIM
    nope = mla_w_uq.shape[2] // mla_heads - MLA_ROPE
    vdim = mla_w_ukv.shape[2] // mla_heads - nope
    main_cols = 3 * na_w + q_rank + kv_rank
    h = _rmsnorm("norm_mix1", xr, norm_mix_g[1], BF16, mod=(sh1, sc1), seg_of_tile=seg_of_tile)[0]
    proj = _linear("odd_in", h, w_in, F32, n_cols=main_cols)
    w_kr = w_in[:, main_cols:main_cols + MLA_ROPE]
    kr = _linear("odd_in_rope", h, jnp.concatenate([w_kr, _rotate_partner_columns(w_kr)], axis=1), F32)
    cq = _rmsnorm("norm_q", proj, mla_q_g[0], BF16, rows=lat_rows, width=q_rank, col_block=3 * na_w // q_rank)[0]
    ckv = _rmsnorm("norm_kv", proj, mla_kv_g[0], BF16, width=kv_rank, col_block=(3 * na_w + q_rank) // kv_rank)[0]
    w_uq = mla_w_uq[0].reshape(q_rank, mla_heads, nope + MLA_ROPE)
    w_uq = jnp.concatenate([w_uq, jnp.stack([_rotate_partner_columns(w_uq[:, hh, nope:])
                                             for hh in range(mla_heads)], axis=1)], axis=-1)
    q = _linear("mla_q", cq, w_uq.reshape(q_rank, mla_heads * (nope + 2 * MLA_ROPE)), F32)
    kv = _linear("mla_kv", ckv, mla_w_ukv[0], F32)
    z_na = _neighbourhood_attention(proj, nb, seq, ctx_len, heads, na_rpb[0])
    z_mla = _latent_attention(q, kv, kr, nb, seq, ctx_len, mla_heads, nope, vdim)
    xr = _mixer_out("odd_out", xr, 0, lat_rows, z_na, z_mla, od_w_out[0], g1, seg_of_tile)
    h2, route = _rmsnorm("norm_ffn1", xr, norm_ffn_g[1], F32, rows=lat_rows, mod=(sh2, sc2),
                         seg_of_tile=seg_of_tile, router=(moe_router_w[0], moe_router_b[0]))
    assert lat_rows % MOE_TILE == 0 and lat_rows % MOE_COMBINE_TILE == 0
    row_token, tile_expert, n_active, pos = _moe_plan(route, moe_w1.shape[1])
    xs = _moe_gather(h2, row_token)
    ys = _moe_experts(xs, tile_expert, n_active, moe_w1[0], moe_w3[0], moe_w2[0])
    xr = _moe_combine(xr, lat_rows, ys, pos, route, g2, seg_of_tile)
    out = _rmsnorm("norm_final", xr, final_g, F32, rows=lat_rows)[0]
    return out.reshape(nb, seq, d)
```

```python
import functools
import math

import jax
import jax.numpy as jnp
import numpy as np
from jax import lax
from jax.experimental import pallas as pl
from jax.experimental.pallas import tpu as pltpu

F32 = jnp.float32
BF16 = jnp.bfloat16

GRID_W = 64
HEAD_DIM = 128
N_MOD = 6
HY_ORDER = 2
HY_POS_BANDS = 16
HY_DECAY_TARGET = 1e-2
HY_FAST_DECAY = 0.3
HY_SLOW_DECAY = 1.5
NA_WIN_R = 8
NA_WIN_C = 16
MLA_ROPE = 64
ROPE_BASE = 10000.0
TOP_K = 2
EPS = 1e-6

LANES = 128
MOD_ROWS = 8
V7X_VMEM_LIMIT = 58 * 2 ** 20
NEG_BIG = -1e30
ROUTE_EXPERT = 0
ROUTE_GATE = TOP_K
MOE_TILE = 512
MOE_COMBINE_TILE = 256
DMA_ISSUE_UNROLL = 8
TRIG_TILE = 128


def _tile(n, pref):
    if n <= pref:
        return n
    t = (pref // LANES) * LANES
    while t >= LANES:
        if n % t == 0:
            return t
        t -= LANES
    raise ValueError(f"no lane-aligned tile for {n} <= {pref}")


def _params(n_grid):
    return pltpu.CompilerParams(dimension_semantics=("arbitrary",) * n_grid,
                                vmem_limit_bytes=V7X_VMEM_LIMIT)


def _bf(a):
    return a if a.dtype == BF16 else a.astype(BF16)


def _dot(a, b):
    return jnp.dot(_bf(a), _bf(b), preferred_element_type=F32)


def _dot_nt(a, b):
    return lax.dot_general(_bf(a), _bf(b), (((1,), (1,)), ((), ())), preferred_element_type=F32)


def _split(a):
    hi = a.astype(BF16)
    lo = (a - hi.astype(F32)).astype(BF16)
    return hi, lo


def _dot3(a, b):
    ah, al = _split(a)
    bh, bl = _split(b)
    return (jnp.dot(ah, bh, preferred_element_type=F32) + jnp.dot(ah, bl, preferred_element_type=F32)
            + jnp.dot(al, bh, preferred_element_type=F32))


def _silu(a):
    return a * (1.0 / (1.0 + jnp.exp(-a)))


def _mm_body(*refs, n_x, n_w, n_e, n_o, groups, nk, epilogue, n_grid):
    xs = refs[:n_x]
    ws = refs[n_x:n_x + n_w]
    es = refs[n_x + n_w:n_x + n_w + n_e]
    outs = refs[n_x + n_w + n_e:n_x + n_w + n_e + n_o]
    accs = refs[n_x + n_w + n_e + n_o:]
    pids = [pl.program_id(a) for a in range(n_grid)]
    xv = [_bf(x[...]) for x in xs]
    wv = [_bf(w[...]) for w in ws]
    parts = []
    for group in groups:
        p = None
        for xi, wi in group:
            d = jnp.dot(xv[xi], wv[wi], preferred_element_type=F32)
            p = d if p is None else p + d
        parts.append(p)
    if nk == 1:
        epilogue(parts, es, outs, pids)
        return
    k = pids[-1]

    @pl.when(k == 0)
    def _():
        for a, p in zip(accs, parts):
            a[...] = p

    @pl.when(k > 0)
    def _():
        for a, p in zip(accs, parts):
            a[...] += p

    @pl.when(k == nk - 1)
    def _():
        epilogue([a[...] for a in accs], es, outs, pids)


def _mm(name, grid, xs, x_specs, ws, w_specs, es, e_specs, out_shapes, out_specs, groups, epilogue,
        nk=1, acc_block=None, aliases=None):
    body = functools.partial(_mm_body, n_x=len(xs), n_w=len(ws), n_e=len(es), n_o=len(out_shapes),
                             groups=groups, nk=nk, epilogue=epilogue, n_grid=len(grid))
    scratch = [pltpu.VMEM(acc_block, F32) for _ in groups] if nk > 1 else []
    res = pl.pallas_call(
        body, grid=grid, in_specs=list(x_specs) + list(w_specs) + list(e_specs),
        out_specs=list(out_specs), out_shape=list(out_shapes), scratch_shapes=scratch,
        input_output_aliases=aliases or {}, compiler_params=_params(len(grid)), name=name,
    )(*xs, *ws, *es)
    return res


def _store0(parts, es, outs, pids):
    outs[0][...] = parts[0].astype(outs[0].dtype)


def _linear(name, x, w, out_dtype, *, n_cols=None, tm_pref=1024, tn_pref=512):
    m, k = x.shape
    n = n_cols or w.shape[1]
    tm, tn = _tile(m, tm_pref), _tile(n, tn_pref)
    return _mm(name, (m // tm, n // tn),
               [x], [pl.BlockSpec((tm, k), lambda i, j: (i, 0))],
               [w], [pl.BlockSpec((k, tn), lambda i, j: (0, j))],
               [], [], [jax.ShapeDtypeStruct((m, n), out_dtype)],
               [pl.BlockSpec((tm, tn), lambda i, j: (i, j))], [[(0, 0)]], _store0)[0]


def _mod_body(c_ref, w_ref, b_ref, o_ref):
    o_ref[...] = _dot(_silu(c_ref[...]), w_ref[...]) + b_ref[...]


def _modulation(c_all, mod_w, mod_b):
    depth, d, n = mod_w.shape
    tn = _tile(n, 512)
    return pl.pallas_call(
        _mod_body, grid=(depth, n // tn),
        in_specs=[pl.BlockSpec((MOD_ROWS, d), lambda l, j: (0, 0)),
                  pl.BlockSpec((None, d, tn), lambda l, j: (l, 0, j)),
                  pl.BlockSpec((None, 1, tn), lambda l, j: (l, 0, j))],
        out_specs=pl.BlockSpec((None, MOD_ROWS, tn), lambda l, j: (l, 0, j)),
        out_shape=jax.ShapeDtypeStruct((depth, MOD_ROWS, n), F32),
        compiler_params=_params(2), name="modulation",
    )(c_all, mod_w, mod_b.reshape(depth, 1, n))


def _norm_body(*refs, modulate, router, n_experts):
    it = iter(refs)
    x_ref, g_ref = next(it), next(it)
    x = x_ref[...]
    y = x * lax.rsqrt(jnp.mean(x * x, axis=-1, keepdims=True) + EPS) * g_ref[...]
    if modulate:
        sh_ref, sc_ref = next(it), next(it)
        y = y * (1.0 + sc_ref[...]) + sh_ref[...]
    if router:
        rw_ref, rb_ref = next(it), next(it)
    o_ref = next(it)
    o_ref[...] = y.astype(o_ref.dtype)
    if router:
        comb_ref = next(it)
        logits = _dot3(y, rw_ref[...]) + rb_ref[...]
        lane = lax.broadcasted_iota(jnp.int32, logits.shape, 1).astype(F32)
        neg = jnp.float32(-jnp.inf)
        logits = jnp.where(lane < n_experts, logits, neg)
        m1 = jnp.max(logits, axis=-1, keepdims=True)
        i1 = jnp.min(jnp.where(logits == m1, lane, float(LANES)), axis=-1, keepdims=True)
        rest = jnp.where(lane == i1, neg, logits)
        m2 = jnp.max(rest, axis=-1, keepdims=True)
        i2 = jnp.min(jnp.where(rest == m2, lane, float(LANES)), axis=-1, keepdims=True)
        e2 = jnp.exp(m2 - m1)
        g1 = 1.0 / (1.0 + e2)
        g2 = e2 / (1.0 + e2)
        comb_ref[...] = (jnp.where(lane == ROUTE_EXPERT, i1, 0.0) + jnp.where(lane == ROUTE_EXPERT + 1, i2, 0.0)
                         + jnp.where(lane == ROUTE_GATE, g1, 0.0) + jnp.where(lane == ROUTE_GATE + 1, g2, 0.0))


def _rmsnorm(name, x, gain, out_dtype, *, rows=None, width=None, col_block=0, mod=None, seg_of_tile=None,
             router=None, tr=256):
    rows = rows or x.shape[0]
    width = width or x.shape[1]
    tr = _tile(rows, tr)
    in_specs = [pl.BlockSpec((tr, width), lambda i: (i, col_block)),
                pl.BlockSpec((1, width), lambda i: (0, 0))]
    args = [x, gain.reshape(1, width)]
    if mod is not None:
        seg = seg_of_tile(tr)
        for t in mod:
            in_specs.append(pl.BlockSpec((None, 1, width), lambda i: (seg(i), 0, 0)))
            args.append(t)
    out_shape = [jax.ShapeDtypeStruct((rows, width), out_dtype)]
    out_specs = [pl.BlockSpec((tr, width), lambda i: (i, 0))]
    n_experts = 0
    if router is not None:
        rw, rb = router
        n_experts = rw.shape[1]
        rw_p = jnp.pad(rw, ((0, 0), (0, LANES - n_experts)))
        rb_p = jnp.pad(rb, (0, LANES - n_experts)).reshape(1, LANES)
        in_specs += [pl.BlockSpec((width, LANES), lambda i: (0, 0)), pl.BlockSpec((1, LANES), lambda i: (0, 0))]
        args += [rw_p, rb_p]
        out_shape.append(jax.ShapeDtypeStruct((rows, LANES), F32))
        out_specs.append(pl.BlockSpec((tr, LANES), lambda i: (i, 0)))
    body = functools.partial(_norm_body, modulate=mod is not None, router=router is not None, n_experts=n_experts)
    return pl.pallas_call(body, grid=(rows // tr,), in_specs=in_specs, out_specs=out_specs, out_shape=out_shape,
                          compiler_params=_params(1), name=name)(*args)


def _group_dft_body(p_ref, cs_ref, o_ref, *, groups):
    cs = cs_ref[...]
    fw = groups * HEAD_DIM
    for g in range(groups):
        r = _dot(p_ref[:, g * HEAD_DIM:(g + 1) * HEAD_DIM], cs)
        o_ref[:, g * HEAD_DIM:(g + 1) * HEAD_DIM] = r[:, :HEAD_DIM].astype(o_ref.dtype)
        o_ref[:, fw + g * HEAD_DIM:fw + (g + 1) * HEAD_DIM] = r[:, HEAD_DIM:].astype(o_ref.dtype)


def _group_dft(proj, fw):
    rows = proj.shape[0]
    k = np.arange(HEAD_DIM)
    ang = (2.0 * np.pi / HEAD_DIM) * ((k[:, None] * k[None, :]) % HEAD_DIM)
    cs = jnp.asarray(np.concatenate([np.cos(ang), np.sin(ang)], axis=1), dtype=BF16)
    tm = _tile(rows, 512)
    body = functools.partial(_group_dft_body, groups=fw // HEAD_DIM)
    return pl.pallas_call(
        body, grid=(rows // tm,),
        in_specs=[pl.BlockSpec((tm, fw), lambda i: (i, 0)),
                  pl.BlockSpec((HEAD_DIM, 2 * HEAD_DIM), lambda i: (0, 0))],
        out_specs=pl.BlockSpec((tm, 2 * fw), lambda i: (i, 0)),
        out_shape=jax.ShapeDtypeStruct((rows, 2 * fw), BF16),
        compiler_params=_params(1), name="group_dft",
    )(proj, cs)


def _trig_body(ca_ref, sa_ref, cb_ref, sb_ref, o_ref, *, sin_sign, scale):
    ca, sa, cb, sb = ca_ref[...], sa_ref[...], cb_ref[...], sb_ref[...]
    cos = ca * cb - sa * sb
    sin = (sa * cb + ca * sb) * sin_sign
    o_ref[...] = (jnp.where(pl.program_id(0) == 0, cos, sin) * scale).astype(o_ref.dtype)


def _trig_matrix(name, phase, den, n_rows, n_cols, parts_on_rows, sin_sign=1.0, scale=1.0):
    tile = min(TRIG_TILE, n_rows)
    n_hi = n_rows // tile
    col = jnp.arange(n_cols, dtype=jnp.int32)[None, :]

    def tables(rows):
        ang = (phase(rows[:, None], col) % den).astype(F32) * (2.0 * math.pi / den)
        return jnp.cos(ang), jnp.sin(ang)

    zero = jnp.zeros((1,), jnp.int32)
    base_c, base_s = tables(zero)
    hi_c, hi_s = tables(jnp.arange(n_hi, dtype=jnp.int32) * tile)
    lo_c, lo_s = tables(jnp.arange(tile, dtype=jnp.int32))
    lo_c, lo_s = lo_c * base_c + lo_s * base_s, lo_s * base_c - lo_c * base_s
    if parts_on_rows:
        shape, out_map = (2 * n_rows, n_cols), (lambda p, h: (p * n_hi + h, 0))
    else:
        shape, out_map = (n_rows, 2 * n_cols), (lambda p, h: (h, p))
    body = functools.partial(_trig_body, sin_sign=sin_sign, scale=scale)
    hi_spec = pl.BlockSpec((None, 1, n_cols), lambda p, h: (h, 0, 0))
    lo_spec = pl.BlockSpec((tile, n_cols), lambda p, h: (0, 0))
    return pl.pallas_call(
        body, grid=(2, n_hi), in_specs=[hi_spec, hi_spec, lo_spec, lo_spec],
        out_specs=pl.BlockSpec((tile, n_cols), out_map),
        out_shape=jax.ShapeDtypeStruct(shape, BF16), compiler_params=_params(2), name=name,
    )(hi_c.reshape(n_hi, 1, n_cols), hi_s.reshape(n_hi, 1, n_cols), lo_c, lo_s)


def _position_dft(name, ab, row0, nb, length, fw):
    cs = _trig_matrix(f"{name}_matrix", lambda f, t: f * t, length, length, length, parts_on_rows=False,
                      sin_sign=-1.0, scale=1.0 / math.sqrt(length * HEAD_DIM))
    tm, tn = _tile(length, 1024), _tile(fw, 512)
    nm, nn = length // tm, fw // tn
    rb0 = row0 // length
    return _mm(name, (nb, nm, nn, 2),
               [cs], [pl.BlockSpec((tm, length), lambda b, i, j, k: (i, k))],
               [ab], [pl.BlockSpec((length, tn), lambda b, i, j, k: (rb0 + b, k * nn + j))],
               [], [], [jax.ShapeDtypeStruct((nb * length, fw), BF16)],
               [pl.BlockSpec((tm, tn), lambda b, i, j, k: (b * nm + i, j))],
               [[(0, 0)]], _store0, nk=2, acc_block=(tm, tn))[0]


def _short_conv_body(p_ref, w_ref, b_ref, o_ref):
    u = p_ref[...]
    n = u.shape[0]
    row = lax.broadcasted_iota(jnp.int32, u.shape, 0)
    prev = jnp.where(row == 0, 0.0, pltpu.roll(u, 1, 0))
    nxt = jnp.where(row == n - 1, 0.0, pltpu.roll(u, n - 1, 0))
    w = w_ref[...]
    o_ref[...] = w[0:1] * prev + w[1:2] * u + w[2:3] * nxt + b_ref[...]


def _short_conv(name, proj, col0, width, row0, nb, length, conv_w, conv_b):
    tc = _tile(math.gcd(col0, width), 512)
    cb0, rb0 = col0 // tc, row0 // length
    return pl.pallas_call(
        _short_conv_body, grid=(nb, width // tc),
        in_specs=[pl.BlockSpec((length, tc), lambda b, j: (rb0 + b, cb0 + j)),
                  pl.BlockSpec((3, tc), lambda b, j: (0, j)),
                  pl.BlockSpec((1, tc), lambda b, j: (0, j))],
        out_specs=pl.BlockSpec((length, tc), lambda b, j: (b, j)),
        out_shape=jax.ShapeDtypeStruct((nb * length, width), F32),
        compiler_params=_params(2), name=name,
    )(proj, conv_w, conv_b.reshape(1, width))


def _filter_mlp_body(z_ref, pw_ref, pb_ref, w1_ref, b1_ref, w2_ref, b2_ref, fr_ref, o_ref):
    fr = fr_ref[...]
    h = jnp.sin(fr * (_dot3(z_ref[...], pw_ref[...]) + pb_ref[...]))
    h = jnp.sin(fr * (_dot3(h, w1_ref[...]) + b1_ref[...]))
    o_ref[...] = jnp.sin(fr * (_dot3(h, w2_ref[...]) + b2_ref[...]))


def _filter_out_body(h_ref, wf_ref, wb_ref, d_ref, o_ref):
    h = h_ref[...]
    d = d_ref[...]
    fwd = _dot3(h, wf_ref[...]) * d
    bwd = _dot3(h, wb_ref[...]) * d
    row = lax.broadcasted_iota(jnp.int32, bwd.shape, 0)
    bwd = jnp.where(row == 0, 0.0, bwd)
    o_ref[0] = fwd + bwd
    o_ref[1] = fwd - bwd


def _hyena_filter_sums(length, hw, pe_w, pe_b, w1, b1, w2, b2, w_out, freq):
    t = jnp.linspace(0.0, 1.0, length, dtype=F32)[:, None]
    w = (2.0 * math.pi / length) * jnp.arange(length, dtype=F32)[:, None]
    bands = jnp.linspace(1e-4, HY_POS_BANDS - 1, HY_POS_BANDS, dtype=F32)[None, :]
    z = jnp.concatenate([t, jnp.cos(bands * w), -jnp.sin(bands * w)], axis=-1)
    pos_dim, fwid = pe_w.shape
    z = jnp.pad(z, ((0, 0), (0, LANES - pos_dim)))
    pe_w = jnp.pad(pe_w, ((0, LANES - pos_dim), (0, 0)))
    row = lambda v: v.reshape(1, fwid)
    full = lambda a: pl.BlockSpec(a.shape, lambda i: (0,) * a.ndim)
    args = [z, pe_w, row(pe_b), w1, row(b1), w2, row(b2), row(freq)]
    h = pl.pallas_call(
        _filter_mlp_body, grid=(1,), in_specs=[full(a) for a in args],
        out_specs=pl.BlockSpec((length, fwid), lambda i: (0, 0)),
        out_shape=jax.ShapeDtypeStruct((length, fwid), F32),
        compiler_params=_params(1), name="hyena_filter_mlp",
    )(*args)
    max_decay = math.log(HY_DECAY_TARGET) / HY_FAST_DECAY
    min_decay = math.log(HY_DECAY_TARGET) / HY_SLOW_DECAY
    deltas = jnp.abs(jnp.linspace(min_decay, max_decay, hw, dtype=F32))
    decay = jnp.exp(-t * deltas)
    tc = _tile(hw, 256)
    nc = hw // tc
    return pl.pallas_call(
        _filter_out_body, grid=(HY_ORDER, nc),
        in_specs=[pl.BlockSpec((length, fwid), lambda n, j: (0, 0)),
                  pl.BlockSpec((fwid, tc), lambda n, j: (0, (2 * n) * nc + j)),
                  pl.BlockSpec((fwid, tc), lambda n, j: (0, (2 * n + 1) * nc + j)),
                  pl.BlockSpec((length, tc), lambda n, j: (0, j))],
        out_specs=pl.BlockSpec((2, length, tc), lambda n, j: (0, 0, n * nc + j)),
        out_shape=jax.ShapeDtypeStruct((2, length, HY_ORDER * hw), F32),
        compiler_params=_params(2), name="hyena_filter_out",
    )(h, w_out, w_out, decay)


def _hyena_spectrum_epilogue(parts, es, outs, pids):
    ur, us = parts
    kr, ks = es[0][0], es[0][1]
    outs[0][0] = (ur * kr - us * ks).astype(outs[0].dtype)
    outs[0][1] = (ur * ks + us * kr).astype(outs[0].dtype)


def _hyena_gate_epilogue(parts, es, outs, pids):
    gate_ref, z_ref, skip_ref = es
    outs[0][...] = (gate_ref[...] * (parts[0] + z_ref[...] * skip_ref[...])).astype(outs[0].dtype)


def _hyena_stream(tag, uc, nb, length, hw, filt, skip):
    fo_b = _trig_matrix(f"hyena_dft_{tag}", lambda f, t: (2 * f + 1) * t, 4 * length, length, length,
                        parts_on_rows=True)
    go_b = _trig_matrix(f"hyena_idft_{tag}", lambda t, f: (2 * f + 1) * t, 4 * length, length, length,
                        parts_on_rows=False, scale=1.0 / length)
    tm, tn = _tile(length, 1024), _tile(hw, 512)
    nm, nn = length // tm, hw // tn
    kspec = _mm(f"hyena_kspec_{tag}", (2 * nm, HY_ORDER * nn),
                [fo_b], [pl.BlockSpec((tm, length), lambda i, j: (i, 0))],
                [filt.reshape(2 * length, HY_ORDER * hw)],
                [pl.BlockSpec((length, tn), lambda i, j: (i // nm, j))],
                [], [], [jax.ShapeDtypeStruct((2 * length, HY_ORDER * hw), F32)],
                [pl.BlockSpec((tm, tn), lambda i, j: (i, j))], [[(0, 0)]], _store0)[0]
    kspec = kspec.reshape(2, length, HY_ORDER * hw)
    z = uc
    for n in range(HY_ORDER):
        z = _hyena_order(tag, n, uc, z, nb, length, hw, fo_b, go_b, kspec, skip, (tm, tn, nm, nn))
    return z


def _hyena_order(tag, n, uc, z, nb, length, hw, fo_b, go_b, kspec, skip, tiles):
    tm, tn, nm, nn = tiles
    gate_cb = (n + 1) * nn
    spec = _mm(f"hyena_fwd{n}_{tag}", (nm, nb, nn),
               [fo_b, fo_b], [pl.BlockSpec((tm, length), lambda i, b, j: (i, 0)),
                              pl.BlockSpec((tm, length), lambda i, b, j: (nm + i, 0))],
               [z], [pl.BlockSpec((length, tn), lambda i, b, j: (b, j))],
               [kspec], [pl.BlockSpec((2, tm, tn), lambda i, b, j: (0, i, n * nn + j))],
               [jax.ShapeDtypeStruct((nb, 2, length, hw), BF16)],
               [pl.BlockSpec((None, 2, tm, tn), lambda i, b, j: (b, 0, i, j))],
               [[(0, 0)], [(1, 0)]], _hyena_spectrum_epilogue)[0]
    last = n == HY_ORDER - 1
    return _mm(f"hyena_inv{n}_{tag}", (nb, nm, nn),
               [go_b], [pl.BlockSpec((tm, 2 * length), lambda b, i, j: (i, 0))],
               [spec.reshape(nb * 2 * length, hw)],
               [pl.BlockSpec((2 * length, tn), lambda b, i, j: (b, j))],
               [uc, z, skip.reshape(HY_ORDER, 1, hw)],
               [pl.BlockSpec((tm, tn), lambda b, i, j: (b * nm + i, gate_cb + j)),
                pl.BlockSpec((tm, tn), lambda b, i, j: (b * nm + i, j)),
                pl.BlockSpec((None, 1, tn), lambda b, i, j: (n, 0, j))],
               [jax.ShapeDtypeStruct((nb * length, hw), BF16 if last else F32)],
               [pl.BlockSpec((tm, tn), lambda b, i, j: (b * nm + i, j))],
               [[(0, 0)]], _hyena_gate_epilogue)[0]


def _resgate_epilogue(parts, es, outs, pids):
    res_ref, gate_ref = es
    outs[0][...] = res_ref[...] + gate_ref[...] * parts[0]


def _mixer_out(name, x_res, row0, rows, za, zb, w_out, gate, seg_of_tile):
    d = x_res.shape[1]
    half = za.shape[1]
    tm, tn = _tile(rows, 1024), _tile(d, 512)
    rb0 = row0 // tm
    seg = seg_of_tile(tm)
    return _mm(name, (rows // tm, d // tn),
               [za, zb], [pl.BlockSpec((tm, half), lambda i, j: (i, 0))] * 2,
               [w_out, w_out], [pl.BlockSpec((half, tn), lambda i, j: (0, j)),
                                pl.BlockSpec((half, tn), lambda i, j: (1, j))],
               [x_res, gate], [pl.BlockSpec((tm, tn), lambda i, j: (rb0 + i, j)),
                               pl.BlockSpec((None, 1, tn), lambda i, j: (seg(rb0 + i), 0, j))],
               [jax.ShapeDtypeStruct(x_res.shape, F32)],
               [pl.BlockSpec((tm, tn), lambda i, j: (rb0 + i, j))],
               [[(0, 0), (1, 1)]], _resgate_epilogue, aliases={4: 0})[0]


def _swiglu_epilogue(parts, es, outs, pids):
    outs[0][...] = (_silu(parts[0]) * parts[1]).astype(outs[0].dtype)


def _ffn_up(h, w1, w3):
    m, k = h.shape
    n = w1.shape[1]
    tm, tn = _tile(m, 1024), _tile(n, 256)
    return _mm("ffn_up", (m // tm, n // tn),
               [h], [pl.BlockSpec((tm, k), lambda i, j: (i, 0))],
               [w1, w3], [pl.BlockSpec((k, tn), lambda i, j: (0, j))] * 2,
               [], [], [jax.ShapeDtypeStruct((m, n), BF16)],
               [pl.BlockSpec((tm, tn), lambda i, j: (i, j))],
               [[(0, 0)], [(0, 1)]], _swiglu_epilogue)[0]


def _down_residual(name, x_res, rows, u, w2, gate, seg_of_tile, tm_pref, tk_pref, tn_pref):
    d = x_res.shape[1]
    kdim = u.shape[1]
    tm, tn, tk = _tile(rows, tm_pref), _tile(d, tn_pref), _tile(kdim, tk_pref)
    nk = kdim // tk
    seg = seg_of_tile(tm)
    return _mm(name, (rows // tm, d // tn, nk),
               [u], [pl.BlockSpec((tm, tk), lambda i, j, k: (i, k))],
               [w2], [pl.BlockSpec((tk, tn), lambda i, j, k: (k, j))],
               [x_res, gate], [pl.BlockSpec((tm, tn), lambda i, j, k: (i, j)),
                               pl.BlockSpec((None, 1, tn), lambda i, j, k: (seg(i), 0, j))],
               [jax.ShapeDtypeStruct(x_res.shape, F32)],
               [pl.BlockSpec((tm, tn), lambda i, j, k: (i, j))],
               [[(0, 0)]], _resgate_epilogue, nk=nk, acc_block=(tm, tn), aliases={2: 0})[0]


def _moe_plan(route, n_experts):
    t = route.shape[0]
    n_tiles = TOP_K * t // MOE_TILE + n_experts
    expert = route[:, ROUTE_EXPERT:ROUTE_EXPERT + TOP_K].astype(jnp.int32).T.reshape(-1)
    onehot = (expert[:, None] == jnp.arange(n_experts, dtype=jnp.int32)[None, :]).astype(jnp.int32)
    running = jnp.cumsum(onehot, axis=0)
    rank = jnp.take_along_axis(running, expert[:, None], axis=1)[:, 0] - 1
    tiles_per_expert = (running[-1] + MOE_TILE - 1) // MOE_TILE
    tile_end = jnp.cumsum(tiles_per_expert)
    row = (tile_end - tiles_per_expert)[expert] * MOE_TILE + rank
    token = jnp.tile(jnp.arange(t, dtype=jnp.int32), TOP_K)
    row_token = jnp.zeros((n_tiles * MOE_TILE,), jnp.int32).at[row].set(token)
    tile_expert = jnp.searchsorted(tile_end, jnp.arange(n_tiles, dtype=jnp.int32), side="right")
    tile_expert = jnp.minimum(tile_expert, n_experts - 1).astype(jnp.int32)
    pos = row.reshape(TOP_K, t // MOE_COMBINE_TILE, MOE_COMBINE_TILE).transpose(1, 0, 2)
    return row_token.reshape(n_tiles, 1, MOE_TILE), tile_expert, tile_end[-1:].astype(jnp.int32), pos


def _row_copy(src_hbm, row, dst_ref, dst_row, sem):
    return pltpu.make_async_copy(src_hbm.at[pl.ds(row, 1), :], dst_ref.at[pl.ds(dst_row, 1), :], sem)


def _wait_rows(src_hbm, buf_ref, sem):
    pltpu.make_async_copy(src_hbm.at[pl.ds(0, buf_ref.shape[0]), :], buf_ref, sem).wait()


def _moe_gather_body(na_ref, tok_ref, src_hbm, o_ref, buf_ref, sem):
    rows = buf_ref.shape[0]
    active = pl.program_id(0) < na_ref[0]

    @pl.when(active)
    def _():
        def issue(r, carry):
            _row_copy(src_hbm, tok_ref[0, r], buf_ref, r, sem).start()
            return carry

        lax.fori_loop(0, rows, issue, 0, unroll=DMA_ISSUE_UNROLL)
        _wait_rows(src_hbm, buf_ref, sem)
        o_ref[...] = buf_ref[...].astype(o_ref.dtype)

    @pl.when(jnp.logical_not(active))
    def _():
        o_ref[...] = jnp.zeros(o_ref.shape, o_ref.dtype)


def _dma_params(n_grid):
    return pltpu.CompilerParams(dimension_semantics=("arbitrary",) * n_grid, vmem_limit_bytes=V7X_VMEM_LIMIT,
                                disable_bounds_checks=True)


def _moe_gather(h, row_token, n_active):
    n_tiles = row_token.shape[0]
    d = h.shape[1]
    return pl.pallas_call(
        _moe_gather_body, grid=(n_tiles,),
        in_specs=[pl.BlockSpec(memory_space=pltpu.SMEM),
                  pl.BlockSpec((None, 1, MOE_TILE), lambda i: (i, 0, 0), memory_space=pltpu.SMEM),
                  pl.BlockSpec(memory_space=pl.ANY)],
        out_specs=pl.BlockSpec((MOE_TILE, d), lambda i: (i, 0)),
        out_shape=jax.ShapeDtypeStruct((n_tiles * MOE_TILE, d), BF16),
        scratch_shapes=[pltpu.VMEM((MOE_TILE, d), F32), pltpu.SemaphoreType.DMA(())],
        compiler_params=_dma_params(1), name="moe_gather",
    )(n_active, row_token, h)


def _moe_up_body(te_ref, na_ref, x_ref, w1_ref, w3_ref, o_ref):
    active = pl.program_id(1) < na_ref[0]

    @pl.when(active)
    def _():
        x = x_ref[...]
        o_ref[...] = (_silu(_dot(x, w1_ref[...])) * _dot(x, w3_ref[...])).astype(o_ref.dtype)

    @pl.when(jnp.logical_not(active))
    def _():
        o_ref[...] = jnp.zeros(o_ref.shape, o_ref.dtype)


def _moe_down_body(te_ref, na_ref, u_ref, w2_ref, o_ref):
    active = pl.program_id(1) < na_ref[0]

    @pl.when(active)
    def _():
        o_ref[...] = _dot(u_ref[...], w2_ref[...])

    @pl.when(jnp.logical_not(active))
    def _():
        o_ref[...] = jnp.zeros(o_ref.shape, o_ref.dtype)


def _moe_experts(xs, tile_expert, n_active, w1, w3, w2):
    ne, d, f = w1.shape
    n_tiles = xs.shape[0] // MOE_TILE
    tn = _tile(f, 512)
    up = pl.pallas_call(
        _moe_up_body,
        grid_spec=pltpu.PrefetchScalarGridSpec(
            num_scalar_prefetch=2, grid=(f // tn, n_tiles),
            in_specs=[pl.BlockSpec((MOE_TILE, d), lambda j, i, te, na: (i, 0)),
                      pl.BlockSpec((None, d, tn), lambda j, i, te, na: (te[i], 0, j)),
                      pl.BlockSpec((None, d, tn), lambda j, i, te, na: (te[i], 0, j))],
            out_specs=pl.BlockSpec((MOE_TILE, tn), lambda j, i, te, na: (i, j))),
        out_shape=jax.ShapeDtypeStruct((xs.shape[0], f), BF16),
        compiler_params=_params(2), name="moe_up",
    )(tile_expert, n_active, xs, w1, w3)
    tn = _tile(d, 1024)
    return pl.pallas_call(
        _moe_down_body,
        grid_spec=pltpu.PrefetchScalarGridSpec(
            num_scalar_prefetch=2, grid=(d // tn, n_tiles),
            in_specs=[pl.BlockSpec((MOE_TILE, f), lambda j, i, te, na: (i, 0)),
                      pl.BlockSpec((None, f, tn), lambda j, i, te, na: (te[i], 0, j))],
            out_specs=pl.BlockSpec((MOE_TILE, tn), lambda j, i, te, na: (i, j))),
        out_shape=jax.ShapeDtypeStruct((xs.shape[0], d), F32),
        compiler_params=_params(2), name="moe_down",
    )(tile_expert, n_active, up, w2)


def _moe_combine_body(pos_ref, ys_hbm, res_ref, route_ref, gate_ref, o_ref, buf_ref, sem):
    rows = res_ref.shape[0]

    def issue(r, carry):
        for k in range(TOP_K):
            _row_copy(ys_hbm, pos_ref[k, r], buf_ref.at[k], r, sem).start()
        return carry

    lax.fori_loop(0, rows, issue, 0, unroll=DMA_ISSUE_UNROLL)
    for k in range(TOP_K):
        _wait_rows(ys_hbm, buf_ref.at[k], sem)
    route = route_ref[...]
    lane = lax.broadcasted_iota(jnp.int32, route.shape, 1)
    moe = None
    for k in range(TOP_K):
        g = jnp.sum(jnp.where(lane == ROUTE_GATE + k, route, 0.0), axis=-1, keepdims=True)
        term = g * buf_ref[k]
        moe = term if moe is None else moe + term
    o_ref[...] = res_ref[...] + gate_ref[...] * moe


def _moe_combine(x_res, rows, ys, pos, route, gate, seg_of_tile):
    d = x_res.shape[1]
    tm = MOE_COMBINE_TILE
    seg = seg_of_tile(tm)
    return pl.pallas_call(
        _moe_combine_body, grid=(rows // tm,),
        in_specs=[pl.BlockSpec((None, TOP_K, tm), lambda i: (i, 0, 0), memory_space=pltpu.SMEM),
                  pl.BlockSpec(memory_space=pl.ANY),
                  pl.BlockSpec((tm, d), lambda i: (i, 0)),
                  pl.BlockSpec((tm, LANES), lambda i: (i, 0)),
                  pl.BlockSpec((None, 1, d), lambda i: (seg(i), 0, 0))],
        out_specs=pl.BlockSpec((tm, d), lambda i: (i, 0)),
        out_shape=jax.ShapeDtypeStruct(x_res.shape, F32),
        scratch_shapes=[pltpu.VMEM((TOP_K, tm, d), F32), pltpu.SemaphoreType.DMA(())],
        input_output_aliases={2: 0},
        compiler_params=_dma_params(1), name="moe_combine",
    )(pos, ys, x_res, route, gate)


NA_BLOCK_ROWS = 4


def _na_body(q_ref, k_ref, v_ref, kc_ref, vc_ref, bias_ref, o_ref, va_ref, *, seq, ctx_len, blocks, win):
    qscale = HEAD_DIM ** -0.5 * math.log2(math.e)
    bq = NA_BLOCK_ROWS * GRID_W
    va_ref[0:seq, 0:HEAD_DIM] = v_ref[...].astype(BF16)
    va_ref[seq:seq + ctx_len, 0:HEAD_DIM] = vc_ref[...].astype(BF16)
    lane = lax.broadcasted_iota(jnp.int32, (seq + ctx_len, va_ref.shape[1] - HEAD_DIM), 1)
    va_ref[:, HEAD_DIM:] = jnp.where(lane == 0, 1.0, 0.0).astype(BF16)
    kc = kc_ref[...].astype(BF16)
    for kb, (case, ws) in enumerate(blocks):
        q = (q_ref[kb * bq:(kb + 1) * bq, :] * qscale).astype(BF16)
        k0 = ws * GRID_W
        s_loc = _dot_nt(q, k_ref[k0:k0 + win, :]) + bias_ref[case]
        s_ctx = _dot_nt(q, kc)
        m = jnp.maximum(jnp.max(s_loc, axis=-1, keepdims=True), jnp.max(s_ctx, axis=-1, keepdims=True))
        acc = _dot(jnp.exp2(s_loc - m), va_ref[k0:k0 + win, :]) + _dot(jnp.exp2(s_ctx - m), va_ref[seq:seq + ctx_len, :])
        o_ref[kb * bq:(kb + 1) * bq, :] = (acc[:, :HEAD_DIM] / acc[:, HEAD_DIM:HEAD_DIM + 1]).astype(o_ref.dtype)


def _na_bias_slabs(rpb, grid_rows, win_rows):
    heads = rpb.shape[0]
    cols = np.arange(GRID_W)
    start = np.clip(cols - NA_WIN_C // 2, 0, GRID_W - NA_WIN_C)
    kc = cols[None, :]
    inside = (kc >= start[:, None]) & (kc < start[:, None] + NA_WIN_C)
    dc = np.clip(kc - cols[:, None] + (NA_WIN_C - 1), 0, 2 * NA_WIN_C - 2)
    band = jnp.where(jnp.asarray(inside)[None, None], rpb[:, :, dc], NEG_BIG)
    masked = band.shape[1]
    band = jnp.concatenate([band, jnp.full((heads, 1, GRID_W, GRID_W), NEG_BIG, F32)], axis=1)
    win = min(win_rows + NA_BLOCK_ROWS, grid_rows)
    tables, blocks = [], []
    for r0 in range(0, grid_rows, NA_BLOCK_ROWS):
        ws = int(np.clip(r0 - win_rows // 2, 0, grid_rows - win))
        idx = np.full((NA_BLOCK_ROWS, win), masked)
        for ri in range(NA_BLOCK_ROWS):
            r = r0 + ri
            rs = int(np.clip(r - win_rows // 2, 0, grid_rows - win_rows))
            assert ws <= rs and rs + win_rows <= ws + win
            for key_row in range(rs, rs + win_rows):
                idx[ri, key_row - ws] = key_row - r + (NA_WIN_R - 1)
        for case, t in enumerate(tables):
            if np.array_equal(t, idx):
                break
        else:
            case = len(tables)
            tables.append(idx)
        blocks.append((case, ws))
    slabs = band[:, np.stack(tables)]
    slabs = jnp.transpose(slabs, (0, 1, 2, 4, 3, 5))
    slabs = slabs.reshape(heads, len(tables), NA_BLOCK_ROWS * GRID_W, win * GRID_W) * math.log2(math.e)
    return slabs, tuple(blocks), win * GRID_W


def _neighbourhood_attention(proj, nb, seq, ctx_len, heads, rpb):
    grid_rows = seq // GRID_W
    assert grid_rows % NA_BLOCK_ROWS == 0
    win_rows = min(NA_WIN_R, grid_rows)
    bias, blocks, win = _na_bias_slabs(rpb, grid_rows, win_rows)
    cr0 = nb * seq // ctx_len
    body = functools.partial(_na_body, seq=seq, ctx_len=ctx_len, blocks=blocks, win=win)
    return pl.pallas_call(
        body, grid=(nb, heads),
        in_specs=[pl.BlockSpec((seq, HEAD_DIM), lambda b, h: (b, h)),
                  pl.BlockSpec((seq, HEAD_DIM), lambda b, h: (b, heads + h)),
                  pl.BlockSpec((seq, HEAD_DIM), lambda b, h: (b, 2 * heads + h)),
                  pl.BlockSpec((ctx_len, HEAD_DIM), lambda b, h: (cr0 + b, heads + h)),
                  pl.BlockSpec((ctx_len, HEAD_DIM), lambda b, h: (cr0 + b, 2 * heads + h)),
                  pl.BlockSpec((None,) + bias.shape[1:], lambda b, h: (h, 0, 0, 0))],
        out_specs=pl.BlockSpec((seq, HEAD_DIM), lambda b, h: (b, h)),
        out_shape=jax.ShapeDtypeStruct((nb * seq, heads * HEAD_DIM), BF16),
        scratch_shapes=[pltpu.VMEM((seq + ctx_len, HEAD_DIM + LANES), BF16)],
        compiler_params=_params(2), name="neighbourhood_attention",
    )(proj, proj, proj, proj, proj, bias)


def _mla_body(q_ref, kvl_ref, kvc_ref, krl_ref, krc_ref, cq_ref, sq_ref, ck_ref, sk_ref, o_ref,
              qf_ref, kf_ref, va_ref, *, nope, vdim, seq, ctx_len, chunks):
    qscale = (nope + MLA_ROPE) ** -0.5 * math.log2(math.e)
    rope_end = nope + MLA_ROPE

    @pl.when(pl.program_id(2) == 0)
    def _():
        krl = krl_ref[...]
        kf_ref[0:seq, 0:nope] = kvl_ref[:, 0:nope].astype(BF16)
        kf_ref[0:seq, nope:rope_end] = (krl[:, :MLA_ROPE] * ck_ref[...] + krl[:, MLA_ROPE:] * sk_ref[...]).astype(BF16)
        kf_ref[seq:seq + ctx_len, 0:nope] = kvc_ref[:, 0:nope].astype(BF16)
        kf_ref[seq:seq + ctx_len, nope:rope_end] = krc_ref[:, 0:MLA_ROPE].astype(BF16)
        kf_ref[:, rope_end:] = jnp.zeros((seq + ctx_len, kf_ref.shape[1] - rope_end), BF16)
        va_ref[0:seq, 0:vdim] = kvl_ref[:, nope:nope + vdim].astype(BF16)
        va_ref[seq:seq + ctx_len, 0:vdim] = kvc_ref[:, nope:nope + vdim].astype(BF16)
        lane = lax.broadcasted_iota(jnp.int32, (seq + ctx_len, va_ref.shape[1] - vdim), 1)
        va_ref[:, vdim:] = jnp.where(lane == 0, 1.0, 0.0).astype(BF16)

    q = q_ref[...]
    qf_ref[:, 0:nope] = (q[:, :nope] * qscale).astype(BF16)
    qr = q[:, nope:rope_end] * cq_ref[...] + q[:, rope_end:] * sq_ref[...]
    qf_ref[:, nope:rope_end] = (qr * qscale).astype(BF16)
    qf_ref[:, rope_end:] = jnp.zeros((q.shape[0], qf_ref.shape[1] - rope_end), BF16)
    qf = qf_ref[...]
    m = jnp.full((q.shape[0], 1), -jnp.inf, F32)
    acc = jnp.zeros((q.shape[0], va_ref.shape[1]), F32)
    for start, size in chunks:
        s = _dot_nt(qf, kf_ref[start:start + size, :])
        m_new = jnp.maximum(m, jnp.max(s, axis=-1, keepdims=True))
        p = jnp.exp2(s - m_new)
        acc = acc * jnp.exp2(m - m_new) + _dot(p, va_ref[start:start + size, :])
        m = m_new
    o_ref[...] = (acc[:, :vdim] / acc[:, vdim:vdim + 1]).astype(o_ref.dtype)


def _rope_tables(seq):
    pos = np.arange(seq)
    row = (pos // GRID_W).astype(np.float32)
    col = (pos % GRID_W).astype(np.float32)
    half = MLA_ROPE // 2
    inv = jnp.asarray(ROPE_BASE, F32) ** (-jnp.arange(0, half, 2, dtype=F32) / half)
    ang_r = jnp.asarray(row)[:, None] * inv
    ang_c = jnp.asarray(col)[:, None] * inv
    ang = jnp.concatenate([ang_r, ang_r, ang_c, ang_c], axis=-1)
    return jnp.cos(ang), jnp.sin(ang)


def _rotate_partner_columns(w):
    q = MLA_ROPE // 4
    parts = [w[:, i * q:(i + 1) * q] for i in range(4)]
    return jnp.concatenate([-parts[1], parts[0], -parts[3], parts[2]], axis=1)


def _latent_attention(q, kv, kr, nb, seq, ctx_len, heads, nope, vdim):
    cos, sin = _rope_tables(seq)
    tq = _tile(seq, 1024)
    nq = seq // tq
    cr0 = nb * seq // ctx_len
    qw = nope + 2 * MLA_ROPE
    keys = seq + ctx_len
    n_chunks = -(-keys // 1024)
    while keys % (n_chunks * LANES):
        n_chunks += 1
    chunk = keys // n_chunks
    chunks = tuple((s, chunk) for s in range(0, keys, chunk))
    kw = nope + 2 * MLA_ROPE
    vw = vdim + LANES
    body = functools.partial(_mla_body, nope=nope, vdim=vdim, seq=seq, ctx_len=ctx_len, chunks=chunks)
    return pl.pallas_call(
        body, grid=(nb, heads, nq),
        scratch_shapes=[pltpu.VMEM((tq, kw), BF16), pltpu.VMEM((keys, kw), BF16), pltpu.VMEM((keys, vw), BF16)],
        in_specs=[pl.BlockSpec((tq, qw), lambda b, h, i: (b * nq + i, h)),
                  pl.BlockSpec((seq, nope + vdim), lambda b, h, i: (b, h)),
                  pl.BlockSpec((ctx_len, nope + vdim), lambda b, h, i: (cr0 + b, h)),
                  pl.BlockSpec((seq, 2 * MLA_ROPE), lambda b, h, i: (b, 0)),
                  pl.BlockSpec((ctx_len, 2 * MLA_ROPE), lambda b, h, i: (cr0 + b, 0)),
                  pl.BlockSpec((tq, MLA_ROPE), lambda b, h, i: (i, 0)),
                  pl.BlockSpec((tq, MLA_ROPE), lambda b, h, i: (i, 0)),
                  pl.BlockSpec((seq, MLA_ROPE), lambda b, h, i: (0, 0)),
                  pl.BlockSpec((seq, MLA_ROPE), lambda b, h, i: (0, 0))],
        out_specs=pl.BlockSpec((tq, vdim), lambda b, h, i: (b * nq + i, h)),
        out_shape=jax.ShapeDtypeStruct((nb * seq, heads * vdim), BF16),
        compiler_params=_params(3), name="latent_attention",
    )(q, kv, kv, kr, kr, cos, sin, cos, sin)


def kernel(x, c, ctx, c_ctx, mod_w, mod_b, norm_mix_g, norm_ffn_g, ev_w_in, ev_w_out, hy_conv_w, hy_conv_b, hy_pe_w, hy_pe_b, hy_w1, hy_b1, hy_w2, hy_b2, hy_w_out, hy_freq, hy_skip, ffn_w1, ffn_w3, ffn_w2, od_w_in, od_w_out, na_rpb, mla_q_g, mla_w_uq, mla_kv_g, mla_w_ukv, moe_router_w, moe_router_b, moe_w1, moe_w3, moe_w2, final_g):
    nb, seq, d = x.shape
    ctx_len = ctx.shape[1]
    depth = mod_w.shape[0]
    assert depth == 2 and ev_w_in.shape[0] == 1 and od_w_in.shape[0] == 1, "even layer then odd (last) layer"
    lat_rows, ctx_rows = nb * seq, nb * ctx_len
    rows = lat_rows + ctx_rows
    assert nb + 1 <= MOD_ROWS

    def seg_of_tile(tile_rows):
        assert seq % tile_rows == 0
        return lambda i: jnp.minimum((i * tile_rows) // seq, nb)

    xr = jnp.concatenate([x.reshape(lat_rows, d), ctx.reshape(ctx_rows, d)], axis=0)
    c_all = jnp.concatenate([c, c_ctx[None, :], jnp.zeros((MOD_ROWS - nb - 1, d), F32)], axis=0)
    mod = _modulation(c_all, mod_w, mod_b).reshape(depth, MOD_ROWS, N_MOD, 1, d)
    mods = [[mod[i, :, m] for m in range(N_MOD)] for i in range(depth)]

    sh1, sc1, g1, sh2, sc2, g2 = mods[0]
    fw = (ev_w_out.shape[1]) // 2
    hw = ev_w_out.shape[1] - fw
    h = _rmsnorm("norm_mix0", xr, norm_mix_g[0], BF16, mod=(sh1, sc1), seg_of_tile=seg_of_tile)[0]
    proj = _linear("even_in", h, ev_w_in[0], F32)
    ab = _group_dft(proj, fw)
    filt_args = (hy_pe_w[0], hy_pe_b[0], hy_w1[0], hy_b1[0], hy_w2[0], hy_b2[0], hy_w_out[0], hy_freq[0])
    for tag, row0, length in (("lat", 0, seq), ("ctx", lat_rows, ctx_len)):
        zf = _position_dft(f"fourier_{tag}", ab, row0, nb, length, fw)
        uc = _short_conv(f"short_conv_{tag}", proj, fw, (HY_ORDER + 1) * hw, row0, nb, length,
                         hy_conv_w[0], hy_conv_b[0])
        filt = _hyena_filter_sums(length, hw, *filt_args)
        zh = _hyena_stream(tag, uc, nb, length, hw, filt, hy_skip[0])
        xr = _mixer_out(f"even_out_{tag}", xr, row0, nb * length, zf, zh, ev_w_out[0], g1, seg_of_tile)
    h2 = _rmsnorm("norm_ffn0", xr, norm_ffn_g[0], BF16, mod=(sh2, sc2), seg_of_tile=seg_of_tile)[0]
    u = _ffn_up(h2, ffn_w1[0], ffn_w3[0])
    xr = _down_residual("ffn_down", xr, rows, u, ffn_w2[0].astype(BF16), g2, seg_of_tile,
                        tm_pref=512, tk_pref=u.shape[1], tn_pref=512)

    sh1, sc1, g1, sh2, sc2, g2 = mods[1]
    w_in = od_w_in[0]
    na_w = od_w_out.shape[1] // 2
    heads = na_w // HEAD_DIM
    q_rank = mla_w_uq.shape[1]
    kv_rank = mla_w_ukv.shape[1]
    mla_heads = (od_w_out.shape[1] - na_w) // HEAD_DIM
    nope = mla_w_uq.shape[2] // mla_heads - MLA_ROPE
    vdim = mla_w_ukv.shape[2] // mla_heads - nope
    main_cols = 3 * na_w + q_rank + kv_rank
    h = _rmsnorm("norm_mix1", xr, norm_mix_g[1], BF16, mod=(sh1, sc1), seg_of_tile=seg_of_tile)[0]
    proj = _linear("odd_in", h, w_in, F32, n_cols=main_cols)
    w_kr = w_in[:, main_cols:main_cols + MLA_ROPE]
    kr = _linear("odd_in_rope", h, jnp.concatenate([w_kr, _rotate_partner_columns(w_kr)], axis=1), F32)
    cq = _rmsnorm("norm_q", proj, mla_q_g[0], BF16, rows=lat_rows, width=q_rank, col_block=3 * na_w // q_rank)[0]
    ckv = _rmsnorm("norm_kv", proj, mla_kv_g[0], BF16, width=kv_rank, col_block=(3 * na_w + q_rank) // kv_rank)[0]
    w_uq = mla_w_uq[0].reshape(q_rank, mla_heads, nope + MLA_ROPE)
    w_uq = jnp.concatenate([w_uq, jnp.stack([_rotate_partner_columns(w_uq[:, hh, nope:])
                                             for hh in range(mla_heads)], axis=1)], axis=-1)
    q = _linear("mla_q", cq, w_uq.reshape(q_rank, mla_heads * (nope + 2 * MLA_ROPE)), F32, tn_pref=2048)
    kv = _linear("mla_kv", ckv, mla_w_ukv[0], BF16, tn_pref=2048)
    z_na = _neighbourhood_attention(proj, nb, seq, ctx_len, heads, na_rpb[0])
    z_mla = _latent_attention(q, kv, kr, nb, seq, ctx_len, mla_heads, nope, vdim)
    xr = _mixer_out("odd_out", xr, 0, lat_rows, z_na, z_mla, od_w_out[0], g1, seg_of_tile)
    h2, route = _rmsnorm("norm_ffn1", xr, norm_ffn_g[1], F32, rows=lat_rows, mod=(sh2, sc2),
                         seg_of_tile=seg_of_tile, router=(moe_router_w[0], moe_router_b[0]))
    assert lat_rows % MOE_TILE == 0 and lat_rows % MOE_COMBINE_TILE == 0
    row_token, tile_expert, n_active, pos = _moe_plan(route, moe_w1.shape[1])
    xs = _moe_gather(h2, row_token, n_active)
    ys = _moe_experts(xs, tile_expert, n_active, moe_w1[0], moe_w3[0], moe_w2[0])
    xr = _moe_combine(xr, lat_rows, ys, pos, route, g2, seg_of_tile)
    out = _rmsnorm("norm_final", xr, final_g, F32, rows=lat_rows)[0]
    return out.reshape(nb, seq, d)
```

```python
import functools
import math

import jax
import jax.numpy as jnp
import numpy as np
from jax import lax
from jax.experimental import pallas as pl
from jax.experimental.pallas import tpu as pltpu

F32 = jnp.float32
BF16 = jnp.bfloat16

GRID_W = 64
HEAD_DIM = 128
N_MOD = 6
HY_ORDER = 2
HY_POS_BANDS = 16
HY_DECAY_TARGET = 1e-2
HY_FAST_DECAY = 0.3
HY_SLOW_DECAY = 1.5
NA_WIN_R = 8
NA_WIN_C = 16
MLA_ROPE = 64
ROPE_BASE = 10000.0
TOP_K = 2
EPS = 1e-6

LANES = 128
MOD_ROWS = 8
V7X_VMEM_LIMIT = 58 * 2 ** 20
NEG_BIG = -1e30
ROUTE_EXPERT = 0
ROUTE_GATE = TOP_K
MOE_TILE = 512
MOE_COMBINE_TILE = 256
DMA_ISSUE_UNROLL = 8
TRIG_TILE = 128


def _tile(n, pref):
    if n <= pref:
        return n
    t = (pref // LANES) * LANES
    while t >= LANES:
        if n % t == 0:
            return t
        t -= LANES
    raise ValueError(f"no lane-aligned tile for {n} <= {pref}")


def _params(n_grid):
    return pltpu.CompilerParams(dimension_semantics=("arbitrary",) * n_grid,
                                vmem_limit_bytes=V7X_VMEM_LIMIT)


def _bf(a):
    return a if a.dtype == BF16 else a.astype(BF16)


def _dot(a, b):
    return jnp.dot(_bf(a), _bf(b), preferred_element_type=F32)


def _dot_nt(a, b):
    return lax.dot_general(_bf(a), _bf(b), (((1,), (1,)), ((), ())), preferred_element_type=F32)


def _split(a):
    hi = a.astype(BF16)
    lo = (a - hi.astype(F32)).astype(BF16)
    return hi, lo


def _dot3(a, b):
    ah, al = _split(a)
    bh, bl = _split(b)
    return (jnp.dot(ah, bh, preferred_element_type=F32) + jnp.dot(ah, bl, preferred_element_type=F32)
            + jnp.dot(al, bh, preferred_element_type=F32))


def _silu(a):
    return a * (1.0 / (1.0 + jnp.exp(-a)))


def _mm_body(*refs, n_x, n_w, n_e, n_o, groups, nk, epilogue, n_grid):
    xs = refs[:n_x]
    ws = refs[n_x:n_x + n_w]
    es = refs[n_x + n_w:n_x + n_w + n_e]
    outs = refs[n_x + n_w + n_e:n_x + n_w + n_e + n_o]
    accs = refs[n_x + n_w + n_e + n_o:]
    pids = [pl.program_id(a) for a in range(n_grid)]
    xv = [_bf(x[...]) for x in xs]
    wv = [_bf(w[...]) for w in ws]
    parts = []
    for group in groups:
        p = None
        for xi, wi in group:
            d = jnp.dot(xv[xi], wv[wi], preferred_element_type=F32)
            p = d if p is None else p + d
        parts.append(p)
    if nk == 1:
        epilogue(parts, es, outs, pids)
        return
    k = pids[-1]

    @pl.when(k == 0)
    def _():
        for a, p in zip(accs, parts):
            a[...] = p

    @pl.when(k > 0)
    def _():
        for a, p in zip(accs, parts):
            a[...] += p

    @pl.when(k == nk - 1)
    def _():
        epilogue([a[...] for a in accs], es, outs, pids)


def _mm(name, grid, xs, x_specs, ws, w_specs, es, e_specs, out_shapes, out_specs, groups, epilogue,
        nk=1, acc_block=None, aliases=None):
    body = functools.partial(_mm_body, n_x=len(xs), n_w=len(ws), n_e=len(es), n_o=len(out_shapes),
                             groups=groups, nk=nk, epilogue=epilogue, n_grid=len(grid))
    scratch = [pltpu.VMEM(acc_block, F32) for _ in groups] if nk > 1 else []
    res = pl.pallas_call(
        body, grid=grid, in_specs=list(x_specs) + list(w_specs) + list(e_specs),
        out_specs=list(out_specs), out_shape=list(out_shapes), scratch_shapes=scratch,
        input_output_aliases=aliases or {}, compiler_params=_params(len(grid)), name=name,
    )(*xs, *ws, *es)
    return res


def _store0(parts, es, outs, pids):
    outs[0][...] = parts[0].astype(outs[0].dtype)


def _linear(name, x, w, out_dtype, *, n_cols=None, tm_pref=1024, tn_pref=512):
    m, k = x.shape
    n = n_cols or w.shape[1]
    tm, tn = _tile(m, tm_pref), _tile(n, tn_pref)
    return _mm(name, (m // tm, n // tn),
               [x], [pl.BlockSpec((tm, k), lambda i, j: (i, 0))],
               [w], [pl.BlockSpec((k, tn), lambda i, j: (0, j))],
               [], [], [jax.ShapeDtypeStruct((m, n), out_dtype)],
               [pl.BlockSpec((tm, tn), lambda i, j: (i, j))], [[(0, 0)]], _store0)[0]


def _mod_body(c_ref, w_ref, b_ref, o_ref):
    o_ref[...] = _dot(_silu(c_ref[...]), w_ref[...]) + b_ref[...]


def _modulation(c_all, mod_w, mod_b):
    depth, d, n = mod_w.shape
    tn = _tile(n, 512)
    return pl.pallas_call(
        _mod_body, grid=(depth, n // tn),
        in_specs=[pl.BlockSpec((MOD_ROWS, d), lambda l, j: (0, 0)),
                  pl.BlockSpec((None, d, tn), lambda l, j: (l, 0, j)),
                  pl.BlockSpec((None, 1, tn), lambda l, j: (l, 0, j))],
        out_specs=pl.BlockSpec((None, MOD_ROWS, tn), lambda l, j: (l, 0, j)),
        out_shape=jax.ShapeDtypeStruct((depth, MOD_ROWS, n), F32),
        compiler_params=_params(2), name="modulation",
    )(c_all, mod_w, mod_b.reshape(depth, 1, n))


def _norm_body(*refs, modulate, router, n_experts):
    it = iter(refs)
    x_ref, g_ref = next(it), next(it)
    x = x_ref[...]
    y = x * lax.rsqrt(jnp.mean(x * x, axis=-1, keepdims=True) + EPS) * g_ref[...]
    if modulate:
        sh_ref, sc_ref = next(it), next(it)
        y = y * (1.0 + sc_ref[...]) + sh_ref[...]
    if router:
        rw_ref, rb_ref = next(it), next(it)
    o_ref = next(it)
    o_ref[...] = y.astype(o_ref.dtype)
    if router:
        comb_ref = next(it)
        logits = _dot3(y, rw_ref[...]) + rb_ref[...]
        lane = lax.broadcasted_iota(jnp.int32, logits.shape, 1).astype(F32)
        neg = jnp.float32(-jnp.inf)
        logits = jnp.where(lane < n_experts, logits, neg)
        m1 = jnp.max(logits, axis=-1, keepdims=True)
        i1 = jnp.min(jnp.where(logits == m1, lane, float(LANES)), axis=-1, keepdims=True)
        rest = jnp.where(lane == i1, neg, logits)
        m2 = jnp.max(rest, axis=-1, keepdims=True)
        i2 = jnp.min(jnp.where(rest == m2, lane, float(LANES)), axis=-1, keepdims=True)
        e2 = jnp.exp(m2 - m1)
        g1 = 1.0 / (1.0 + e2)
        g2 = e2 / (1.0 + e2)
        comb_ref[...] = (jnp.where(lane == ROUTE_EXPERT, i1, 0.0) + jnp.where(lane == ROUTE_EXPERT + 1, i2, 0.0)
                         + jnp.where(lane == ROUTE_GATE, g1, 0.0) + jnp.where(lane == ROUTE_GATE + 1, g2, 0.0))


def _rmsnorm(name, x, gain, out_dtype, *, rows=None, width=None, col_block=0, mod=None, seg_of_tile=None,
             router=None, tr=256):
    rows = rows or x.shape[0]
    width = width or x.shape[1]
    tr = _tile(rows, tr)
    in_specs = [pl.BlockSpec((tr, width), lambda i: (i, col_block)),
                pl.BlockSpec((1, width), lambda i: (0, 0))]
    args = [x, gain.reshape(1, width)]
    if mod is not None:
        seg = seg_of_tile(tr)
        for t in mod:
            in_specs.append(pl.BlockSpec((None, 1, width), lambda i: (seg(i), 0, 0)))
            args.append(t)
    out_shape = [jax.ShapeDtypeStruct((rows, width), out_dtype)]
    out_specs = [pl.BlockSpec((tr, width), lambda i: (i, 0))]
    n_experts = 0
    if router is not None:
        rw, rb = router
        n_experts = rw.shape[1]
        rw_p = jnp.pad(rw, ((0, 0), (0, LANES - n_experts)))
        rb_p = jnp.pad(rb, (0, LANES - n_experts)).reshape(1, LANES)
        in_specs += [pl.BlockSpec((width, LANES), lambda i: (0, 0)), pl.BlockSpec((1, LANES), lambda i: (0, 0))]
        args += [rw_p, rb_p]
        out_shape.append(jax.ShapeDtypeStruct((rows, LANES), F32))
        out_specs.append(pl.BlockSpec((tr, LANES), lambda i: (i, 0)))
    body = functools.partial(_norm_body, modulate=mod is not None, router=router is not None, n_experts=n_experts)
    return pl.pallas_call(body, grid=(rows // tr,), in_specs=in_specs, out_specs=out_specs, out_shape=out_shape,
                          compiler_params=_params(1), name=name)(*args)


def _group_dft_body(p_ref, cs_ref, o_ref, *, groups):
    cs = cs_ref[...]
    fw = groups * HEAD_DIM
    for g in range(groups):
        r = _dot(p_ref[:, g * HEAD_DIM:(g + 1) * HEAD_DIM], cs)
        o_ref[:, g * HEAD_DIM:(g + 1) * HEAD_DIM] = r[:, :HEAD_DIM].astype(o_ref.dtype)
        o_ref[:, fw + g * HEAD_DIM:fw + (g + 1) * HEAD_DIM] = r[:, HEAD_DIM:].astype(o_ref.dtype)


def _group_dft(proj, fw):
    rows = proj.shape[0]
    k = np.arange(HEAD_DIM)
    ang = (2.0 * np.pi / HEAD_DIM) * ((k[:, None] * k[None, :]) % HEAD_DIM)
    cs = jnp.asarray(np.concatenate([np.cos(ang), np.sin(ang)], axis=1), dtype=BF16)
    tm = _tile(rows, 512)
    body = functools.partial(_group_dft_body, groups=fw // HEAD_DIM)
    return pl.pallas_call(
        body, grid=(rows // tm,),
        in_specs=[pl.BlockSpec((tm, fw), lambda i: (i, 0)),
                  pl.BlockSpec((HEAD_DIM, 2 * HEAD_DIM), lambda i: (0, 0))],
        out_specs=pl.BlockSpec((tm, 2 * fw), lambda i: (i, 0)),
        out_shape=jax.ShapeDtypeStruct((rows, 2 * fw), BF16),
        compiler_params=_params(1), name="group_dft",
    )(proj, cs)


def _trig_body(ca_ref, sa_ref, cb_ref, sb_ref, o_ref, *, sin_sign, scale):
    ca, sa, cb, sb = ca_ref[...], sa_ref[...], cb_ref[...], sb_ref[...]
    cos = ca * cb - sa * sb
    sin = (sa * cb + ca * sb) * sin_sign
    o_ref[...] = (jnp.where(pl.program_id(0) == 0, cos, sin) * scale).astype(o_ref.dtype)


def _trig_matrix(name, phase, den, n_rows, n_cols, parts_on_rows, sin_sign=1.0, scale=1.0):
    tile = min(TRIG_TILE, n_rows)
    n_hi = n_rows // tile
    col = jnp.arange(n_cols, dtype=jnp.int32)[None, :]

    def tables(rows):
        ang = (phase(rows[:, None], col) % den).astype(F32) * (2.0 * math.pi / den)
        return jnp.cos(ang), jnp.sin(ang)

    zero = jnp.zeros((1,), jnp.int32)
    base_c, base_s = tables(zero)
    hi_c, hi_s = tables(jnp.arange(n_hi, dtype=jnp.int32) * tile)
    lo_c, lo_s = tables(jnp.arange(tile, dtype=jnp.int32))
    lo_c, lo_s = lo_c * base_c + lo_s * base_s, lo_s * base_c - lo_c * base_s
    if parts_on_rows:
        shape, out_map = (2 * n_rows, n_cols), (lambda p, h: (p * n_hi + h, 0))
    else:
        shape, out_map = (n_rows, 2 * n_cols), (lambda p, h: (h, p))
    body = functools.partial(_trig_body, sin_sign=sin_sign, scale=scale)
    hi_spec = pl.BlockSpec((None, 1, n_cols), lambda p, h: (h, 0, 0))
    lo_spec = pl.BlockSpec((tile, n_cols), lambda p, h: (0, 0))
    return pl.pallas_call(
        body, grid=(2, n_hi), in_specs=[hi_spec, hi_spec, lo_spec, lo_spec],
        out_specs=pl.BlockSpec((tile, n_cols), out_map),
        out_shape=jax.ShapeDtypeStruct(shape, BF16), compiler_params=_params(2), name=name,
    )(hi_c.reshape(n_hi, 1, n_cols), hi_s.reshape(n_hi, 1, n_cols), lo_c, lo_s)


def _position_dft(name, ab, row0, nb, length, fw):
    cs = _trig_matrix(f"{name}_matrix", lambda f, t: f * t, length, length, length, parts_on_rows=False,
                      sin_sign=-1.0, scale=1.0 / math.sqrt(length * HEAD_DIM))
    tm, tn = _tile(length, 1024), _tile(fw, 512)
    nm, nn = length // tm, fw // tn
    rb0 = row0 // length
    return _mm(name, (nb, nm, nn, 2),
               [cs], [pl.BlockSpec((tm, length), lambda b, i, j, k: (i, k))],
               [ab], [pl.BlockSpec((length, tn), lambda b, i, j, k: (rb0 + b, k * nn + j))],
               [], [], [jax.ShapeDtypeStruct((nb * length, fw), BF16)],
               [pl.BlockSpec((tm, tn), lambda b, i, j, k: (b * nm + i, j))],
               [[(0, 0)]], _store0, nk=2, acc_block=(tm, tn))[0]


def _short_conv_body(p_ref, w_ref, b_ref, o_ref):
    u = p_ref[...].astype(F32)
    n = u.shape[0]
    row = lax.broadcasted_iota(jnp.int32, u.shape, 0)
    prev = jnp.where(row == 0, 0.0, pltpu.roll(u, 1, 0))
    nxt = jnp.where(row == n - 1, 0.0, pltpu.roll(u, n - 1, 0))
    w = w_ref[...]
    o_ref[...] = (w[0:1] * prev + w[1:2] * u + w[2:3] * nxt + b_ref[...]).astype(o_ref.dtype)


def _short_conv(name, proj, col0, width, row0, nb, length, conv_w, conv_b):
    tc = _tile(math.gcd(col0, width), 512)
    cb0, rb0 = col0 // tc, row0 // length
    return pl.pallas_call(
        _short_conv_body, grid=(nb, width // tc),
        in_specs=[pl.BlockSpec((length, tc), lambda b, j: (rb0 + b, cb0 + j)),
                  pl.BlockSpec((3, tc), lambda b, j: (0, j)),
                  pl.BlockSpec((1, tc), lambda b, j: (0, j))],
        out_specs=pl.BlockSpec((length, tc), lambda b, j: (b, j)),
        out_shape=jax.ShapeDtypeStruct((nb * length, width), BF16),
        compiler_params=_params(2), name=name,
    )(proj, conv_w, conv_b.reshape(1, width))


def _filter_mlp_body(z_ref, pw_ref, pb_ref, w1_ref, b1_ref, w2_ref, b2_ref, fr_ref, o_ref):
    fr = fr_ref[...]
    h = jnp.sin(fr * (_dot3(z_ref[...], pw_ref[...]) + pb_ref[...]))
    h = jnp.sin(fr * (_dot3(h, w1_ref[...]) + b1_ref[...]))
    o_ref[...] = jnp.sin(fr * (_dot3(h, w2_ref[...]) + b2_ref[...]))


def _filter_out_body(h_ref, wf_ref, wb_ref, d_ref, o_ref):
    h = h_ref[...]
    d = d_ref[...]
    fwd = _dot3(h, wf_ref[...]) * d
    bwd = _dot3(h, wb_ref[...]) * d
    row = lax.broadcasted_iota(jnp.int32, bwd.shape, 0)
    bwd = jnp.where(row == 0, 0.0, bwd)
    o_ref[0] = fwd + bwd
    o_ref[1] = fwd - bwd


def _hyena_filter_sums(length, hw, pe_w, pe_b, w1, b1, w2, b2, w_out, freq):
    t = jnp.linspace(0.0, 1.0, length, dtype=F32)[:, None]
    w = (2.0 * math.pi / length) * jnp.arange(length, dtype=F32)[:, None]
    bands = jnp.linspace(1e-4, HY_POS_BANDS - 1, HY_POS_BANDS, dtype=F32)[None, :]
    z = jnp.concatenate([t, jnp.cos(bands * w), -jnp.sin(bands * w)], axis=-1)
    pos_dim, fwid = pe_w.shape
    z = jnp.pad(z, ((0, 0), (0, LANES - pos_dim)))
    pe_w = jnp.pad(pe_w, ((0, LANES - pos_dim), (0, 0)))
    row = lambda v: v.reshape(1, fwid)
    full = lambda a: pl.BlockSpec(a.shape, lambda i: (0,) * a.ndim)
    args = [z, pe_w, row(pe_b), w1, row(b1), w2, row(b2), row(freq)]
    h = pl.pallas_call(
        _filter_mlp_body, grid=(1,), in_specs=[full(a) for a in args],
        out_specs=pl.BlockSpec((length, fwid), lambda i: (0, 0)),
        out_shape=jax.ShapeDtypeStruct((length, fwid), F32),
        compiler_params=_params(1), name="hyena_filter_mlp",
    )(*args)
    max_decay = math.log(HY_DECAY_TARGET) / HY_FAST_DECAY
    min_decay = math.log(HY_DECAY_TARGET) / HY_SLOW_DECAY
    deltas = jnp.abs(jnp.linspace(min_decay, max_decay, hw, dtype=F32))
    decay = jnp.exp(-t * deltas)
    tc = _tile(hw, 256)
    nc = hw // tc
    return pl.pallas_call(
        _filter_out_body, grid=(HY_ORDER, nc),
        in_specs=[pl.BlockSpec((length, fwid), lambda n, j: (0, 0)),
                  pl.BlockSpec((fwid, tc), lambda n, j: (0, (2 * n) * nc + j)),
                  pl.BlockSpec((fwid, tc), lambda n, j: (0, (2 * n + 1) * nc + j)),
                  pl.BlockSpec((length, tc), lambda n, j: (0, j))],
        out_specs=pl.BlockSpec((2, length, tc), lambda n, j: (0, 0, n * nc + j)),
        out_shape=jax.ShapeDtypeStruct((2, length, HY_ORDER * hw), F32),
        compiler_params=_params(2), name="hyena_filter_out",
    )(h, w_out, w_out, decay)


def _hyena_spectrum_epilogue(parts, es, outs, pids):
    ur, us = parts
    kr, ks = es[0][0], es[0][1]
    outs[0][0] = (ur * kr - us * ks).astype(outs[0].dtype)
    outs[0][1] = (ur * ks + us * kr).astype(outs[0].dtype)


def _hyena_gate_epilogue(parts, es, outs, pids):
    gate_ref, z_ref, skip_ref = es
    gate, z = gate_ref[...].astype(F32), z_ref[...].astype(F32)
    outs[0][...] = (gate * (parts[0] + z * skip_ref[...])).astype(outs[0].dtype)


def _hyena_stream(tag, uc, nb, length, hw, filt, skip):
    fo_b = _trig_matrix(f"hyena_dft_{tag}", lambda f, t: (2 * f + 1) * t, 4 * length, length, length,
                        parts_on_rows=True)
    go_b = _trig_matrix(f"hyena_idft_{tag}", lambda t, f: (2 * f + 1) * t, 4 * length, length, length,
                        parts_on_rows=False, scale=1.0 / length)
    tm, tn = _tile(length, 1024), _tile(hw, 512)
    nm, nn = length // tm, hw // tn
    kspec = _mm(f"hyena_kspec_{tag}", (2 * nm, HY_ORDER * nn),
                [fo_b], [pl.BlockSpec((tm, length), lambda i, j: (i, 0))],
                [filt.reshape(2 * length, HY_ORDER * hw)],
                [pl.BlockSpec((length, tn), lambda i, j: (i // nm, j))],
                [], [], [jax.ShapeDtypeStruct((2 * length, HY_ORDER * hw), F32)],
                [pl.BlockSpec((tm, tn), lambda i, j: (i, j))], [[(0, 0)]], _store0)[0]
    kspec = kspec.reshape(2, length, HY_ORDER * hw)
    z = uc
    for n in range(HY_ORDER):
        z = _hyena_order(tag, n, uc, z, nb, length, hw, fo_b, go_b, kspec, skip, (tm, tn, nm, nn))
    return z


def _hyena_order(tag, n, uc, z, nb, length, hw, fo_b, go_b, kspec, skip, tiles):
    tm, tn, nm, nn = tiles
    gate_cb = (n + 1) * nn
    spec = _mm(f"hyena_fwd{n}_{tag}", (nm, nb, nn),
               [fo_b, fo_b], [pl.BlockSpec((tm, length), lambda i, b, j: (i, 0)),
                              pl.BlockSpec((tm, length), lambda i, b, j: (nm + i, 0))],
               [z], [pl.BlockSpec((length, tn), lambda i, b, j: (b, j))],
               [kspec], [pl.BlockSpec((2, tm, tn), lambda i, b, j: (0, i, n * nn + j))],
               [jax.ShapeDtypeStruct((nb, 2, length, hw), BF16)],
               [pl.BlockSpec((None, 2, tm, tn), lambda i, b, j: (b, 0, i, j))],
               [[(0, 0)], [(1, 0)]], _hyena_spectrum_epilogue)[0]
    return _mm(f"hyena_inv{n}_{tag}", (nb, nm, nn),
               [go_b], [pl.BlockSpec((tm, 2 * length), lambda b, i, j: (i, 0))],
               [spec.reshape(nb * 2 * length, hw)],
               [pl.BlockSpec((2 * length, tn), lambda b, i, j: (b, j))],
               [uc, z, skip.reshape(HY_ORDER, 1, hw)],
               [pl.BlockSpec((tm, tn), lambda b, i, j: (b * nm + i, gate_cb + j)),
                pl.BlockSpec((tm, tn), lambda b, i, j: (b * nm + i, j)),
                pl.BlockSpec((None, 1, tn), lambda b, i, j: (n, 0, j))],
               [jax.ShapeDtypeStruct((nb * length, hw), BF16)],
               [pl.BlockSpec((tm, tn), lambda b, i, j: (b * nm + i, j))],
               [[(0, 0)]], _hyena_gate_epilogue)[0]


def _resgate_epilogue(parts, es, outs, pids):
    res_ref, gate_ref = es[0], es[1]
    outs[0][...] = res_ref[...] + gate_ref[...] * parts[0]


def _mixer_out(name, res, za, zb, w_out, gate, seg_of_tile, in_place):
    rows, half = za.shape
    d = res.shape[1]
    tm, tn = _tile(rows, 1024), _tile(d, 512)
    seg = seg_of_tile(tm)
    return _mm(name, (rows // tm, d // tn),
               [za, zb], [pl.BlockSpec((tm, half), lambda i, j: (i, 0))] * 2,
               [w_out, w_out], [pl.BlockSpec((half, tn), lambda i, j: (0, j)),
                                pl.BlockSpec((half, tn), lambda i, j: (1, j))],
               [res, gate], [pl.BlockSpec((tm, tn), lambda i, j: (i, j)),
                             pl.BlockSpec((None, 1, tn), lambda i, j: (seg(i), 0, j))],
               [jax.ShapeDtypeStruct(res.shape, F32)],
               [pl.BlockSpec((tm, tn), lambda i, j: (i, j))],
               [[(0, 0), (1, 1)]], _resgate_epilogue, aliases={4: 0} if in_place else {})[0]


def _swiglu_epilogue(parts, es, outs, pids):
    outs[0][...] = (_silu(parts[0]) * parts[1]).astype(outs[0].dtype)


def _ffn_up(name, h, w1, w3):
    m, k = h.shape
    n = w1.shape[1]
    tm, tn = _tile(m, 1024), _tile(n, 256)
    return _mm(name, (m // tm, n // tn),
               [h], [pl.BlockSpec((tm, k), lambda i, j: (i, 0))],
               [w1, w3], [pl.BlockSpec((k, tn), lambda i, j: (0, j))] * 2,
               [], [], [jax.ShapeDtypeStruct((m, n), BF16)],
               [pl.BlockSpec((tm, tn), lambda i, j: (i, j))],
               [[(0, 0)], [(0, 1)]], _swiglu_epilogue)[0]


def _down_residual(name, x_res, rows, u, w2, gate, seg_of_tile, tm_pref, tk_pref, tn_pref):
    d = x_res.shape[1]
    kdim = u.shape[1]
    tm, tn, tk = _tile(rows, tm_pref), _tile(d, tn_pref), _tile(kdim, tk_pref)
    nk = kdim // tk
    seg = seg_of_tile(tm)
    return _mm(name, (rows // tm, d // tn, nk),
               [u], [pl.BlockSpec((tm, tk), lambda i, j, k: (i, k))],
               [w2], [pl.BlockSpec((tk, tn), lambda i, j, k: (k, j))],
               [x_res, gate], [pl.BlockSpec((tm, tn), lambda i, j, k: (i, j)),
                               pl.BlockSpec((None, 1, tn), lambda i, j, k: (seg(i), 0, j))],
               [jax.ShapeDtypeStruct(x_res.shape, F32)],
               [pl.BlockSpec((tm, tn), lambda i, j, k: (i, j))],
               [[(0, 0)]], _resgate_epilogue, nk=nk, acc_block=(tm, tn), aliases={2: 0})[0]


def _moe_plan(route, n_experts):
    t = route.shape[0]
    n_tiles = TOP_K * t // MOE_TILE + n_experts
    expert = route[:, ROUTE_EXPERT:ROUTE_EXPERT + TOP_K].astype(jnp.int32).T.reshape(-1)
    onehot = (expert[:, None] == jnp.arange(n_experts, dtype=jnp.int32)[None, :]).astype(jnp.int32)
    running = jnp.cumsum(onehot, axis=0)
    rank = jnp.take_along_axis(running, expert[:, None], axis=1)[:, 0] - 1
    tiles_per_expert = (running[-1] + MOE_TILE - 1) // MOE_TILE
    tile_end = jnp.cumsum(tiles_per_expert)
    row = (tile_end - tiles_per_expert)[expert] * MOE_TILE + rank
    token = jnp.tile(jnp.arange(t, dtype=jnp.int32), TOP_K)
    row_token = jnp.zeros((n_tiles * MOE_TILE,), jnp.int32).at[row].set(token)
    tile_expert = jnp.searchsorted(tile_end, jnp.arange(n_tiles, dtype=jnp.int32), side="right")
    tile_expert = jnp.minimum(tile_expert, n_experts - 1).astype(jnp.int32)
    pos = row.reshape(TOP_K, t // MOE_COMBINE_TILE, MOE_COMBINE_TILE).transpose(1, 0, 2)
    return row_token.reshape(n_tiles, 1, MOE_TILE), tile_expert, tile_end[-1:].astype(jnp.int32), pos


def _row_copy(src_hbm, row, dst_ref, dst_row, sem):
    return pltpu.make_async_copy(src_hbm.at[pl.ds(row, 1), :], dst_ref.at[pl.ds(dst_row, 1), :], sem)


def _wait_rows(src_hbm, buf_ref, sem):
    pltpu.make_async_copy(src_hbm.at[pl.ds(0, buf_ref.shape[0]), :], buf_ref, sem).wait()


def _moe_gather_body(na_ref, tok_ref, src_hbm, o_ref, buf_ref, sem):
    rows = buf_ref.shape[0]
    active = pl.program_id(0) < na_ref[0]

    @pl.when(active)
    def _():
        def issue(r, carry):
            _row_copy(src_hbm, tok_ref[0, r], buf_ref, r, sem).start()
            return carry

        lax.fori_loop(0, rows, issue, 0, unroll=DMA_ISSUE_UNROLL)
        _wait_rows(src_hbm, buf_ref, sem)
        o_ref[...] = buf_ref[...].astype(o_ref.dtype)

    @pl.when(jnp.logical_not(active))
    def _():
        o_ref[...] = jnp.zeros(o_ref.shape, o_ref.dtype)


def _dma_params(n_grid):
    return pltpu.CompilerParams(dimension_semantics=("arbitrary",) * n_grid, vmem_limit_bytes=V7X_VMEM_LIMIT,
                                disable_bounds_checks=True)


def _moe_gather(h, row_token, n_active):
    n_tiles = row_token.shape[0]
    d = h.shape[1]
    return pl.pallas_call(
        _moe_gather_body, grid=(n_tiles,),
        in_specs=[pl.BlockSpec(memory_space=pltpu.SMEM),
                  pl.BlockSpec((None, 1, MOE_TILE), lambda i: (i, 0, 0), memory_space=pltpu.SMEM),
                  pl.BlockSpec(memory_space=pl.ANY)],
        out_specs=pl.BlockSpec((MOE_TILE, d), lambda i: (i, 0)),
        out_shape=jax.ShapeDtypeStruct((n_tiles * MOE_TILE, d), BF16),
        scratch_shapes=[pltpu.VMEM((MOE_TILE, d), F32), pltpu.SemaphoreType.DMA(())],
        compiler_params=_dma_params(1), name="moe_gather",
    )(n_active, row_token, h)


def _moe_up_body(te_ref, na_ref, x_ref, w1_ref, w3_ref, o_ref):
    active = pl.program_id(1) < na_ref[0]

    @pl.when(active)
    def _():
        x = x_ref[...]
        o_ref[...] = (_silu(_dot(x, w1_ref[...])) * _dot(x, w3_ref[...])).astype(o_ref.dtype)

    @pl.when(jnp.logical_not(active))
    def _():
        o_ref[...] = jnp.zeros(o_ref.shape, o_ref.dtype)


def _moe_down_body(te_ref, na_ref, u_ref, w2_ref, o_ref):
    active = pl.program_id(1) < na_ref[0]

    @pl.when(active)
    def _():
        o_ref[...] = _dot(u_ref[...], w2_ref[...])

    @pl.when(jnp.logical_not(active))
    def _():
        o_ref[...] = jnp.zeros(o_ref.shape, o_ref.dtype)


def _moe_experts(xs, tile_expert, n_active, w1, w3, w2):
    ne, d, f = w1.shape
    n_tiles = xs.shape[0] // MOE_TILE
    tn = _tile(f, 512)
    up = pl.pallas_call(
        _moe_up_body,
        grid_spec=pltpu.PrefetchScalarGridSpec(
            num_scalar_prefetch=2, grid=(f // tn, n_tiles),
            in_specs=[pl.BlockSpec((MOE_TILE, d), lambda j, i, te, na: (i, 0)),
                      pl.BlockSpec((None, d, tn), lambda j, i, te, na: (te[i], 0, j)),
                      pl.BlockSpec((None, d, tn), lambda j, i, te, na: (te[i], 0, j))],
            out_specs=pl.BlockSpec((MOE_TILE, tn), lambda j, i, te, na: (i, j))),
        out_shape=jax.ShapeDtypeStruct((xs.shape[0], f), BF16),
        compiler_params=_params(2), name="moe_up",
    )(tile_expert, n_active, xs, w1, w3)
    tn = _tile(d, 1024)
    return pl.pallas_call(
        _moe_down_body,
        grid_spec=pltpu.PrefetchScalarGridSpec(
            num_scalar_prefetch=2, grid=(d // tn, n_tiles),
            in_specs=[pl.BlockSpec((MOE_TILE, f), lambda j, i, te, na: (i, 0)),
                      pl.BlockSpec((None, f, tn), lambda j, i, te, na: (te[i], 0, j))],
            out_specs=pl.BlockSpec((MOE_TILE, tn), lambda j, i, te, na: (i, j))),
        out_shape=jax.ShapeDtypeStruct((xs.shape[0], d), F32),
        compiler_params=_params(2), name="moe_down",
    )(tile_expert, n_active, up, w2)


def _moe_combine_body(pos_ref, ys_hbm, res_ref, route_ref, gate_ref, o_ref, buf_ref, sem):
    rows = res_ref.shape[0]

    def issue(r, carry):
        for k in range(TOP_K):
            _row_copy(ys_hbm, pos_ref[k, r], buf_ref.at[k], r, sem).start()
        return carry

    lax.fori_loop(0, rows, issue, 0, unroll=DMA_ISSUE_UNROLL)
    for k in range(TOP_K):
        _wait_rows(ys_hbm, buf_ref.at[k], sem)
    route = route_ref[...]
    lane = lax.broadcasted_iota(jnp.int32, route.shape, 1)
    moe = None
    for k in range(TOP_K):
        g = jnp.sum(jnp.where(lane == ROUTE_GATE + k, route, 0.0), axis=-1, keepdims=True)
        term = g * buf_ref[k]
        moe = term if moe is None else moe + term
    o_ref[...] = res_ref[...] + gate_ref[...] * moe


def _moe_combine(x_res, rows, ys, pos, route, gate, seg_of_tile):
    d = x_res.shape[1]
    tm = MOE_COMBINE_TILE
    seg = seg_of_tile(tm)
    return pl.pallas_call(
        _moe_combine_body, grid=(rows // tm,),
        in_specs=[pl.BlockSpec((None, TOP_K, tm), lambda i: (i, 0, 0), memory_space=pltpu.SMEM),
                  pl.BlockSpec(memory_space=pl.ANY),
                  pl.BlockSpec((tm, d), lambda i: (i, 0)),
                  pl.BlockSpec((tm, LANES), lambda i: (i, 0)),
                  pl.BlockSpec((None, 1, d), lambda i: (seg(i), 0, 0))],
        out_specs=pl.BlockSpec((tm, d), lambda i: (i, 0)),
        out_shape=jax.ShapeDtypeStruct(x_res.shape, F32),
        scratch_shapes=[pltpu.VMEM((TOP_K, tm, d), F32), pltpu.SemaphoreType.DMA(())],
        input_output_aliases={2: 0},
        compiler_params=_dma_params(1), name="moe_combine",
    )(pos, ys, x_res, route, gate)


NA_BLOCK_ROWS = 4


def _na_body(q_ref, k_ref, v_ref, kc_ref, vc_ref, band_ref, o_ref, va_ref, bias_ref, *, seq, ctx_len, tables,
             blocks, win):
    qscale = HEAD_DIM ** -0.5 * math.log2(math.e)
    bq = NA_BLOCK_ROWS * GRID_W

    @pl.when(pl.program_id(1) == 0)
    def _():
        for case, table in enumerate(tables):
            for ri, row in enumerate(table):
                for rj, entry in enumerate(row):
                    bias_ref[case, ri * GRID_W:(ri + 1) * GRID_W, rj * GRID_W:(rj + 1) * GRID_W] = band_ref[int(entry)]

    va_ref[0:seq, 0:HEAD_DIM] = v_ref[...].astype(BF16)
    va_ref[seq:seq + ctx_len, 0:HEAD_DIM] = vc_ref[...].astype(BF16)
    lane = lax.broadcasted_iota(jnp.int32, (seq + ctx_len, va_ref.shape[1] - HEAD_DIM), 1)
    va_ref[:, HEAD_DIM:] = jnp.where(lane == 0, 1.0, 0.0).astype(BF16)
    kc = kc_ref[...].astype(BF16)
    for kb, (case, ws) in enumerate(blocks):
        q = (q_ref[kb * bq:(kb + 1) * bq, :] * qscale).astype(BF16)
        k0 = ws * GRID_W
        s_loc = _dot_nt(q, k_ref[k0:k0 + win, :]) + bias_ref[case]
        s_ctx = _dot_nt(q, kc)
        m = jnp.maximum(jnp.max(s_loc, axis=-1, keepdims=True), jnp.max(s_ctx, axis=-1, keepdims=True))
        acc = _dot(jnp.exp2(s_loc - m), va_ref[k0:k0 + win, :]) + _dot(jnp.exp2(s_ctx - m), va_ref[seq:seq + ctx_len, :])
        o_ref[kb * bq:(kb + 1) * bq, :] = (acc[:, :HEAD_DIM] / acc[:, HEAD_DIM:HEAD_DIM + 1]).astype(o_ref.dtype)


def _na_bias_slabs(rpb, grid_rows, win_rows):
    heads = rpb.shape[0]
    cols = np.arange(GRID_W)
    start = np.clip(cols - NA_WIN_C // 2, 0, GRID_W - NA_WIN_C)
    kc = cols[None, :]
    inside = (kc >= start[:, None]) & (kc < start[:, None] + NA_WIN_C)
    dc = np.clip(kc - cols[:, None] + (NA_WIN_C - 1), 0, 2 * NA_WIN_C - 2)
    band = jnp.where(jnp.asarray(inside)[None, None], rpb[:, :, dc], NEG_BIG)
    masked = band.shape[1]
    band = jnp.concatenate([band, jnp.full((heads, 1, GRID_W, GRID_W), NEG_BIG, F32)], axis=1)
    win = min(win_rows + NA_BLOCK_ROWS, grid_rows)
    tables, blocks = [], []
    for r0 in range(0, grid_rows, NA_BLOCK_ROWS):
        ws = int(np.clip(r0 - win_rows // 2, 0, grid_rows - win))
        idx = np.full((NA_BLOCK_ROWS, win), masked)
        for ri in range(NA_BLOCK_ROWS):
            r = r0 + ri
            rs = int(np.clip(r - win_rows // 2, 0, grid_rows - win_rows))
            assert ws <= rs and rs + win_rows <= ws + win
            for key_row in range(rs, rs + win_rows):
                idx[ri, key_row - ws] = key_row - r + (NA_WIN_R - 1)
        for case, t in enumerate(tables):
            if np.array_equal(t, idx):
                break
        else:
            case = len(tables)
            tables.append(idx)
        blocks.append((case, ws))
    return band * math.log2(math.e), tuple(tuple(map(tuple, t)) for t in tables), tuple(blocks), win * GRID_W


def _neighbourhood_attention(proj, proj_ctx, nb, seq, ctx_len, heads, rpb):
    grid_rows = seq // GRID_W
    assert grid_rows % NA_BLOCK_ROWS == 0
    win_rows = min(NA_WIN_R, grid_rows)
    band, tables, blocks, win = _na_bias_slabs(rpb, grid_rows, win_rows)
    body = functools.partial(_na_body, seq=seq, ctx_len=ctx_len, tables=tables, blocks=blocks, win=win)
    return pl.pallas_call(
        body, grid=(heads, nb),
        in_specs=[pl.BlockSpec((seq, HEAD_DIM), lambda h, b: (b, h)),
                  pl.BlockSpec((seq, HEAD_DIM), lambda h, b: (b, heads + h)),
                  pl.BlockSpec((seq, HEAD_DIM), lambda h, b: (b, 2 * heads + h)),
                  pl.BlockSpec((ctx_len, HEAD_DIM), lambda h, b: (b, heads + h)),
                  pl.BlockSpec((ctx_len, HEAD_DIM), lambda h, b: (b, 2 * heads + h)),
                  pl.BlockSpec((None,) + band.shape[1:], lambda h, b: (h, 0, 0, 0))],
        out_specs=pl.BlockSpec((seq, HEAD_DIM), lambda h, b: (b, h)),
        out_shape=jax.ShapeDtypeStruct((nb * seq, heads * HEAD_DIM), BF16),
        scratch_shapes=[pltpu.VMEM((seq + ctx_len, HEAD_DIM + LANES), BF16),
                        pltpu.VMEM((len(tables), NA_BLOCK_ROWS * GRID_W, win), F32)],
        compiler_params=_params(2), name="neighbourhood_attention",
    )(proj, proj, proj, proj_ctx, proj_ctx, band)


def _mla_body(q_ref, kvl_ref, kvc_ref, krl_ref, krc_ref, cq_ref, sq_ref, ck_ref, sk_ref, o_ref,
              qf_ref, kf_ref, va_ref, *, nope, vdim, seq, ctx_len, chunks):
    qscale = (nope + MLA_ROPE) ** -0.5 * math.log2(math.e)
    rope_end = nope + MLA_ROPE

    @pl.when(pl.program_id(2) == 0)
    def _():
        krl = krl_ref[...]
        kf_ref[0:seq, 0:nope] = kvl_ref[:, 0:nope].astype(BF16)
        kf_ref[0:seq, nope:rope_end] = (krl[:, :MLA_ROPE] * ck_ref[...] + krl[:, MLA_ROPE:] * sk_ref[...]).astype(BF16)
        kf_ref[seq:seq + ctx_len, 0:nope] = kvc_ref[:, 0:nope].astype(BF16)
        kf_ref[seq:seq + ctx_len, nope:rope_end] = krc_ref[:, 0:MLA_ROPE].astype(BF16)
        kf_ref[:, rope_end:] = jnp.zeros((seq + ctx_len, kf_ref.shape[1] - rope_end), BF16)
        va_ref[0:seq, 0:vdim] = kvl_ref[:, nope:nope + vdim].astype(BF16)
        va_ref[seq:seq + ctx_len, 0:vdim] = kvc_ref[:, nope:nope + vdim].astype(BF16)
        lane = lax.broadcasted_iota(jnp.int32, (seq + ctx_len, va_ref.shape[1] - vdim), 1)
        va_ref[:, vdim:] = jnp.where(lane == 0, 1.0, 0.0).astype(BF16)

    q = q_ref[...]
    qf_ref[:, 0:nope] = (q[:, :nope] * qscale).astype(BF16)
    qr = q[:, nope:rope_end] * cq_ref[...] + q[:, rope_end:] * sq_ref[...]
    qf_ref[:, nope:rope_end] = (qr * qscale).astype(BF16)
    qf_ref[:, rope_end:] = jnp.zeros((q.shape[0], qf_ref.shape[1] - rope_end), BF16)
    qf = qf_ref[...]
    m = jnp.full((q.shape[0], 1), -jnp.inf, F32)
    acc = jnp.zeros((q.shape[0], va_ref.shape[1]), F32)
    for start, size in chunks:
        s = _dot_nt(qf, kf_ref[start:start + size, :])
        m_new = jnp.maximum(m, jnp.max(s, axis=-1, keepdims=True))
        p = jnp.exp2(s - m_new)
        acc = acc * jnp.exp2(m - m_new) + _dot(p, va_ref[start:start + size, :])
        m = m_new
    o_ref[...] = (acc[:, :vdim] / acc[:, vdim:vdim + 1]).astype(o_ref.dtype)


def _rope_tables(seq):
    pos = np.arange(seq)
    row = (pos // GRID_W).astype(np.float32)
    col = (pos % GRID_W).astype(np.float32)
    half = MLA_ROPE // 2
    inv = jnp.asarray(ROPE_BASE, F32) ** (-jnp.arange(0, half, 2, dtype=F32) / half)
    ang_r = jnp.asarray(row)[:, None] * inv
    ang_c = jnp.asarray(col)[:, None] * inv
    ang = jnp.concatenate([ang_r, ang_r, ang_c, ang_c], axis=-1)
    return jnp.cos(ang), jnp.sin(ang)


def _rotate_partner_columns(w):
    q = MLA_ROPE // 4
    parts = [w[:, i * q:(i + 1) * q] for i in range(4)]
    return jnp.concatenate([-parts[1], parts[0], -parts[3], parts[2]], axis=1)


def _latent_attention(q, kv, kv_ctx, kr, kr_ctx, nb, seq, ctx_len, heads, nope, vdim):
    cos, sin = _rope_tables(seq)
    tq = _tile(seq, 1024)
    nq = seq // tq
    qw = nope + 2 * MLA_ROPE
    keys = seq + ctx_len
    n_chunks = -(-keys // 1024)
    while keys % (n_chunks * LANES):
        n_chunks += 1
    chunk = keys // n_chunks
    chunks = tuple((s, chunk) for s in range(0, keys, chunk))
    kw = nope + 2 * MLA_ROPE
    vw = vdim + LANES
    body = functools.partial(_mla_body, nope=nope, vdim=vdim, seq=seq, ctx_len=ctx_len, chunks=chunks)
    return pl.pallas_call(
        body, grid=(nb, heads, nq),
        scratch_shapes=[pltpu.VMEM((tq, kw), BF16), pltpu.VMEM((keys, kw), BF16), pltpu.VMEM((keys, vw), BF16)],
        in_specs=[pl.BlockSpec((tq, qw), lambda b, h, i: (b * nq + i, h)),
                  pl.BlockSpec((seq, nope + vdim), lambda b, h, i: (b, h)),
                  pl.BlockSpec((ctx_len, nope + vdim), lambda b, h, i: (b, h)),
                  pl.BlockSpec((seq, 2 * MLA_ROPE), lambda b, h, i: (b, 0)),
                  pl.BlockSpec((ctx_len, 2 * MLA_ROPE), lambda b, h, i: (b, 0)),
                  pl.BlockSpec((tq, MLA_ROPE), lambda b, h, i: (i, 0)),
                  pl.BlockSpec((tq, MLA_ROPE), lambda b, h, i: (i, 0)),
                  pl.BlockSpec((seq, MLA_ROPE), lambda b, h, i: (0, 0)),
                  pl.BlockSpec((seq, MLA_ROPE), lambda b, h, i: (0, 0))],
        out_specs=pl.BlockSpec((tq, vdim), lambda b, h, i: (b * nq + i, h)),
        out_shape=jax.ShapeDtypeStruct((nb * seq, heads * vdim), BF16),
        compiler_params=_params(3), name="latent_attention",
    )(q, kv, kv_ctx, kr, kr_ctx, cos, sin, cos, sin)


def kernel(x, c, ctx, c_ctx, mod_w, mod_b, norm_mix_g, norm_ffn_g, ev_w_in, ev_w_out, hy_conv_w, hy_conv_b, hy_pe_w, hy_pe_b, hy_w1, hy_b1, hy_w2, hy_b2, hy_w_out, hy_freq, hy_skip, ffn_w1, ffn_w3, ffn_w2, od_w_in, od_w_out, na_rpb, mla_q_g, mla_w_uq, mla_kv_g, mla_w_ukv, moe_router_w, moe_router_b, moe_w1, moe_w3, moe_w2, final_g):
    nb, seq, d = x.shape
    ctx_len = ctx.shape[1]
    depth = mod_w.shape[0]
    assert depth == 2 and ev_w_in.shape[0] == 1 and od_w_in.shape[0] == 1, "even layer then odd (last) layer"
    lat_rows, ctx_rows = nb * seq, nb * ctx_len
    assert nb + 1 <= MOD_ROWS

    def lat_seg(tile_rows):
        assert seq % tile_rows == 0
        return lambda i: (i * tile_rows) // seq

    def ctx_seg(tile_rows):
        return lambda i: nb

    c_all = jnp.concatenate([c, c_ctx[None, :], jnp.zeros((MOD_ROWS - nb - 1, d), F32)], axis=0)
    mod = _modulation(c_all, mod_w, mod_b).reshape(depth, MOD_ROWS, N_MOD, 1, d)
    mods = [[mod[i, :, m] for m in range(N_MOD)] for i in range(depth)]

    sh1, sc1, g1, sh2, sc2, g2 = mods[0]
    fw = (ev_w_out.shape[1]) // 2
    hw = ev_w_out.shape[1] - fw
    filt_args = (hy_pe_w[0], hy_pe_b[0], hy_w1[0], hy_b1[0], hy_w2[0], hy_b2[0], hy_w_out[0], hy_freq[0])
    w2_b = ffn_w2[0].astype(BF16)
    streams = {}
    for tag, src, length, seg_fn in (("lat", x.reshape(lat_rows, d), seq, lat_seg),
                                     ("ctx", ctx.reshape(ctx_rows, d), ctx_len, ctx_seg)):
        h = _rmsnorm(f"norm_mix0_{tag}", src, norm_mix_g[0], BF16, mod=(sh1, sc1), seg_of_tile=seg_fn)[0]
        proj = _linear(f"even_in_{tag}", h, ev_w_in[0], BF16)
        ab = _group_dft(proj, fw)
        zf = _position_dft(f"fourier_{tag}", ab, 0, nb, length, fw)
        uc = _short_conv(f"short_conv_{tag}", proj, fw, (HY_ORDER + 1) * hw, 0, nb, length,
                         hy_conv_w[0], hy_conv_b[0])
        filt = _hyena_filter_sums(length, hw, *filt_args)
        zh = _hyena_stream(tag, uc, nb, length, hw, filt, hy_skip[0])
        xs_ = _mixer_out(f"even_out_{tag}", src, zf, zh, ev_w_out[0], g1, seg_fn, in_place=False)
        h2 = _rmsnorm(f"norm_ffn0_{tag}", xs_, norm_ffn_g[0], BF16, mod=(sh2, sc2), seg_of_tile=seg_fn)[0]
        u = _ffn_up(f"ffn_up_{tag}", h2, ffn_w1[0], ffn_w3[0])
        streams[tag] = _down_residual(f"ffn_down_{tag}", xs_, xs_.shape[0], u, w2_b, g2, seg_fn,
                                      tm_pref=512, tk_pref=u.shape[1], tn_pref=512)
    xl, xc = streams["lat"], streams["ctx"]

    sh1, sc1, g1, sh2, sc2, g2 = mods[1]
    w_in = od_w_in[0]
    na_w = od_w_out.shape[1] // 2
    heads = na_w // HEAD_DIM
    q_rank = mla_w_uq.shape[1]
    kv_rank = mla_w_ukv.shape[1]
    mla_heads = (od_w_out.shape[1] - na_w) // HEAD_DIM
    nope = mla_w_uq.shape[2] // mla_heads - MLA_ROPE
    vdim = mla_w_ukv.shape[2] // mla_heads - nope
    main_cols = 3 * na_w + q_rank + kv_rank
    w_kr = w_in[:, main_cols:main_cols + MLA_ROPE]
    w_kr = jnp.concatenate([w_kr, _rotate_partner_columns(w_kr)], axis=1)
    kv_col_block = (3 * na_w + q_rank) // kv_rank
    proj, kr, kv = {}, {}, {}
    for tag, xs_, seg_fn in (("lat", xl, lat_seg), ("ctx", xc, ctx_seg)):
        h = _rmsnorm(f"norm_mix1_{tag}", xs_, norm_mix_g[1], BF16, mod=(sh1, sc1), seg_of_tile=seg_fn)[0]
        proj[tag] = _linear(f"odd_in_{tag}", h, w_in, F32, n_cols=main_cols)
        kr[tag] = _linear(f"odd_in_rope_{tag}", h, w_kr, F32)
        ckv = _rmsnorm(f"norm_kv_{tag}", proj[tag], mla_kv_g[0], BF16, width=kv_rank, col_block=kv_col_block)[0]
        kv[tag] = _linear(f"mla_kv_{tag}", ckv, mla_w_ukv[0], BF16, tn_pref=2048)
    cq = _rmsnorm("norm_q", proj["lat"], mla_q_g[0], BF16, width=q_rank, col_block=3 * na_w // q_rank)[0]
    w_uq = mla_w_uq[0].reshape(q_rank, mla_heads, nope + MLA_ROPE)
    w_uq = jnp.concatenate([w_uq, jnp.stack([_rotate_partner_columns(w_uq[:, hh, nope:])
                                             for hh in range(mla_heads)], axis=1)], axis=-1)
    q = _linear("mla_q", cq, w_uq.reshape(q_rank, mla_heads * (nope + 2 * MLA_ROPE)), F32, tn_pref=2048)
    z_na = _neighbourhood_attention(proj["lat"], proj["ctx"], nb, seq, ctx_len, heads, na_rpb[0])
    z_mla = _latent_attention(q, kv["lat"], kv["ctx"], kr["lat"], kr["ctx"], nb, seq, ctx_len, mla_heads, nope, vdim)
    xl = _mixer_out("odd_out", xl, z_na, z_mla, od_w_out[0], g1, lat_seg, in_place=True)
    h2, route = _rmsnorm("norm_ffn1", xl, norm_ffn_g[1], F32, mod=(sh2, sc2), seg_of_tile=lat_seg,
                         router=(moe_router_w[0], moe_router_b[0]))
    assert lat_rows % MOE_TILE == 0 and lat_rows % MOE_COMBINE_TILE == 0
    row_token, tile_expert, n_active, pos = _moe_plan(route, moe_w1.shape[1])
    xs = _moe_gather(h2, row_token, n_active)
    ys = _moe_experts(xs, tile_expert, n_active, moe_w1[0], moe_w3[0], moe_w2[0])
    xl = _moe_combine(xl, lat_rows, ys, pos, route, g2, lat_seg)
    out = _rmsnorm("norm_final", xl, final_g, F32)[0]
    return out.reshape(nb, seq, d)
```

```python
import functools
import math

import jax
import jax.numpy as jnp
import numpy as np
from jax import lax
from jax.experimental import pallas as pl
from jax.experimental.pallas import tpu as pltpu

F32 = jnp.float32
BF16 = jnp.bfloat16

GRID_W = 64
HEAD_DIM = 128
N_MOD = 6
HY_ORDER = 2
HY_POS_BANDS = 16
HY_DECAY_TARGET = 1e-2
HY_FAST_DECAY = 0.3
HY_SLOW_DECAY = 1.5
NA_WIN_R = 8
NA_WIN_C = 16
MLA_ROPE = 64
ROPE_BASE = 10000.0
TOP_K = 2
EPS = 1e-6

LANES = 128
MOD_ROWS = 8
V7X_VMEM_LIMIT = 58 * 2 ** 20
NEG_BIG = -1e30
ROUTE_EXPERT = 0
ROUTE_GATE = TOP_K
MOE_TILE = 512
MOE_COMBINE_TILE = 256
DMA_ISSUE_UNROLL = 8
TRIG_TILE = 128


def _tile(n, pref):
    if n <= pref:
        return n
    t = (pref // LANES) * LANES
    while t >= LANES:
        if n % t == 0:
            return t
        t -= LANES
    raise ValueError(f"no lane-aligned tile for {n} <= {pref}")


def _params(n_grid):
    return pltpu.CompilerParams(dimension_semantics=("arbitrary",) * n_grid,
                                vmem_limit_bytes=V7X_VMEM_LIMIT)


def _bf(a):
    return a if a.dtype == BF16 else a.astype(BF16)


def _dot(a, b):
    return jnp.dot(_bf(a), _bf(b), preferred_element_type=F32)


def _dot_nt(a, b):
    return lax.dot_general(_bf(a), _bf(b), (((1,), (1,)), ((), ())), preferred_element_type=F32)


def _split(a):
    hi = a.astype(BF16)
    lo = (a - hi.astype(F32)).astype(BF16)
    return hi, lo


def _dot3(a, b):
    ah, al = _split(a)
    bh, bl = _split(b)
    return (jnp.dot(ah, bh, preferred_element_type=F32) + jnp.dot(ah, bl, preferred_element_type=F32)
            + jnp.dot(al, bh, preferred_element_type=F32))


def _silu(a):
    return a * (1.0 / (1.0 + jnp.exp(-a)))


def _mm_body(*refs, n_x, n_w, n_e, n_o, groups, nk, epilogue, n_grid):
    xs = refs[:n_x]
    ws = refs[n_x:n_x + n_w]
    es = refs[n_x + n_w:n_x + n_w + n_e]
    outs = refs[n_x + n_w + n_e:n_x + n_w + n_e + n_o]
    accs = refs[n_x + n_w + n_e + n_o:]
    pids = [pl.program_id(a) for a in range(n_grid)]
    xv = [_bf(x[...]) for x in xs]
    wv = [_bf(w[...]) for w in ws]
    parts = []
    for group in groups:
        p = None
        for xi, wi in group:
            d = jnp.dot(xv[xi], wv[wi], preferred_element_type=F32)
            p = d if p is None else p + d
        parts.append(p)
    if nk == 1:
        epilogue(parts, es, outs, pids)
        return
    k = pids[-1]

    @pl.when(k == 0)
    def _():
        for a, p in zip(accs, parts):
            a[...] = p

    @pl.when(k > 0)
    def _():
        for a, p in zip(accs, parts):
            a[...] += p

    @pl.when(k == nk - 1)
    def _():
        epilogue([a[...] for a in accs], es, outs, pids)


def _mm(name, grid, xs, x_specs, ws, w_specs, es, e_specs, out_shapes, out_specs, groups, epilogue,
        nk=1, acc_block=None, aliases=None):
    body = functools.partial(_mm_body, n_x=len(xs), n_w=len(ws), n_e=len(es), n_o=len(out_shapes),
                             groups=groups, nk=nk, epilogue=epilogue, n_grid=len(grid))
    scratch = [pltpu.VMEM(acc_block, F32) for _ in groups] if nk > 1 else []
    res = pl.pallas_call(
        body, grid=grid, in_specs=list(x_specs) + list(w_specs) + list(e_specs),
        out_specs=list(out_specs), out_shape=list(out_shapes), scratch_shapes=scratch,
        input_output_aliases=aliases or {}, compiler_params=_params(len(grid)), name=name,
    )(*xs, *ws, *es)
    return res


def _store0(parts, es, outs, pids):
    outs[0][...] = parts[0].astype(outs[0].dtype)


def _linear(name, x, w, out_dtype, *, n_cols=None, tm_pref=1024, tn_pref=512):
    m, k = x.shape
    n = n_cols or w.shape[1]
    tm, tn = _tile(m, tm_pref), _tile(n, tn_pref)
    return _mm(name, (m // tm, n // tn),
               [x], [pl.BlockSpec((tm, k), lambda i, j: (i, 0))],
               [w], [pl.BlockSpec((k, tn), lambda i, j: (0, j))],
               [], [], [jax.ShapeDtypeStruct((m, n), out_dtype)],
               [pl.BlockSpec((tm, tn), lambda i, j: (i, j))], [[(0, 0)]], _store0)[0]


def _mod_body(c_ref, w_ref, b_ref, o_ref):
    o_ref[...] = _dot(_silu(c_ref[...]), w_ref[...]) + b_ref[...]


def _modulation(c_all, mod_w, mod_b):
    depth, d, n = mod_w.shape
    tn = _tile(n, 512)
    return pl.pallas_call(
        _mod_body, grid=(depth, n // tn),
        in_specs=[pl.BlockSpec((MOD_ROWS, d), lambda l, j: (0, 0)),
                  pl.BlockSpec((None, d, tn), lambda l, j: (l, 0, j)),
                  pl.BlockSpec((None, 1, tn), lambda l, j: (l, 0, j))],
        out_specs=pl.BlockSpec((None, MOD_ROWS, tn), lambda l, j: (l, 0, j)),
        out_shape=jax.ShapeDtypeStruct((depth, MOD_ROWS, n), F32),
        compiler_params=_params(2), name="modulation",
    )(c_all, mod_w, mod_b.reshape(depth, 1, n))


def _pack_bf16_pairs(y):
    n = y.shape[1] // 2
    bits = pltpu.bitcast(y.astype(BF16).astype(F32), jnp.uint32)
    return (bits[:, :n] >> 16) | (bits[:, n:] & jnp.uint32(0xFFFF0000))


def _unpack_bf16_pairs(w):
    lo = pltpu.bitcast(w << 16, F32)
    hi = pltpu.bitcast(w & jnp.uint32(0xFFFF0000), F32)
    return lo.astype(BF16), hi.astype(BF16)


def _norm_body(*refs, modulate, router, n_experts):
    it = iter(refs)
    x_ref, g_ref = next(it), next(it)
    x = x_ref[...]
    y = x * lax.rsqrt(jnp.mean(x * x, axis=-1, keepdims=True) + EPS) * g_ref[...]
    if modulate:
        sh_ref, sc_ref = next(it), next(it)
        y = y * (1.0 + sc_ref[...]) + sh_ref[...]
    if router:
        rw_ref, rb_ref = next(it), next(it)
    o_ref = next(it)
    if o_ref.dtype == jnp.uint32:
        o_ref[...] = _pack_bf16_pairs(y)
    else:
        o_ref[...] = y.astype(o_ref.dtype)
    if router:
        comb_ref = next(it)
        logits = _dot3(y, rw_ref[...]) + rb_ref[...]
        lane = lax.broadcasted_iota(jnp.int32, logits.shape, 1).astype(F32)
        neg = jnp.float32(-jnp.inf)
        logits = jnp.where(lane < n_experts, logits, neg)
        m1 = jnp.max(logits, axis=-1, keepdims=True)
        i1 = jnp.min(jnp.where(logits == m1, lane, float(LANES)), axis=-1, keepdims=True)
        rest = jnp.where(lane == i1, neg, logits)
        m2 = jnp.max(rest, axis=-1, keepdims=True)
        i2 = jnp.min(jnp.where(rest == m2, lane, float(LANES)), axis=-1, keepdims=True)
        e2 = jnp.exp(m2 - m1)
        g1 = 1.0 / (1.0 + e2)
        g2 = e2 / (1.0 + e2)
        comb_ref[...] = (jnp.where(lane == ROUTE_EXPERT, i1, 0.0) + jnp.where(lane == ROUTE_EXPERT + 1, i2, 0.0)
                         + jnp.where(lane == ROUTE_GATE, g1, 0.0) + jnp.where(lane == ROUTE_GATE + 1, g2, 0.0))


def _rmsnorm(name, x, gain, out_dtype, *, rows=None, width=None, col_block=0, mod=None, seg_of_tile=None,
             router=None, tr=256):
    if x.ndim == 3:
        per = x.shape[1]
        rows, width = x.shape[0] * per, x.shape[2]
        tr = _tile(per, tr)
        x_spec = pl.BlockSpec((None, tr, width), lambda i: (i // (per // tr), i % (per // tr), 0))
    else:
        rows = rows or x.shape[0]
        width = width or x.shape[1]
        tr = _tile(rows, tr)
        x_spec = pl.BlockSpec((tr, width), lambda i: (i, col_block))
    in_specs = [x_spec, pl.BlockSpec((1, width), lambda i: (0, 0))]
    args = [x, gain.reshape(1, width)]
    if mod is not None:
        seg = seg_of_tile(tr)
        for t in mod:
            in_specs.append(pl.BlockSpec((None, 1, width), lambda i: (seg(i), 0, 0)))
            args.append(t)
    out_width = width // 2 if out_dtype == jnp.uint32 else width
    out_shape = [jax.ShapeDtypeStruct((rows, out_width), out_dtype)]
    out_specs = [pl.BlockSpec((tr, out_width), lambda i: (i, 0))]
    n_experts = 0
    if router is not None:
        rw, rb = router
        n_experts = rw.shape[1]
        rw_p = jnp.pad(rw, ((0, 0), (0, LANES - n_experts)))
        rb_p = jnp.pad(rb, (0, LANES - n_experts)).reshape(1, LANES)
        in_specs += [pl.BlockSpec((width, LANES), lambda i: (0, 0)), pl.BlockSpec((1, LANES), lambda i: (0, 0))]
        args += [rw_p, rb_p]
        out_shape.append(jax.ShapeDtypeStruct((rows, LANES), F32))
        out_specs.append(pl.BlockSpec((tr, LANES), lambda i: (i, 0)))
    body = functools.partial(_norm_body, modulate=mod is not None, router=router is not None, n_experts=n_experts)
    return pl.pallas_call(body, grid=(rows // tr,), in_specs=in_specs, out_specs=out_specs, out_shape=out_shape,
                          compiler_params=_params(1), name=name)(*args)


def _group_dft_body(p_ref, cs_ref, o_ref, *, groups):
    cs = cs_ref[...]
    fw = groups * HEAD_DIM
    for g in range(groups):
        r = _dot(p_ref[:, g * HEAD_DIM:(g + 1) * HEAD_DIM], cs)
        o_ref[:, g * HEAD_DIM:(g + 1) * HEAD_DIM] = r[:, :HEAD_DIM].astype(o_ref.dtype)
        o_ref[:, fw + g * HEAD_DIM:fw + (g + 1) * HEAD_DIM] = r[:, HEAD_DIM:].astype(o_ref.dtype)


def _group_dft(proj, fw):
    rows = proj.shape[0]
    k = np.arange(HEAD_DIM)
    ang = (2.0 * np.pi / HEAD_DIM) * ((k[:, None] * k[None, :]) % HEAD_DIM)
    cs = jnp.asarray(np.concatenate([np.cos(ang), np.sin(ang)], axis=1), dtype=BF16)
    tm = _tile(rows, 512)
    body = functools.partial(_group_dft_body, groups=fw // HEAD_DIM)
    return pl.pallas_call(
        body, grid=(rows // tm,),
        in_specs=[pl.BlockSpec((tm, fw), lambda i: (i, 0)),
                  pl.BlockSpec((HEAD_DIM, 2 * HEAD_DIM), lambda i: (0, 0))],
        out_specs=pl.BlockSpec((tm, 2 * fw), lambda i: (i, 0)),
        out_shape=jax.ShapeDtypeStruct((rows, 2 * fw), BF16),
        compiler_params=_params(1), name="group_dft",
    )(proj, cs)


def _trig_body(ca_ref, sa_ref, cb_ref, sb_ref, o_ref, *, sin_sign, scale):
    ca, sa, cb, sb = ca_ref[...], sa_ref[...], cb_ref[...], sb_ref[...]
    cos = ca * cb - sa * sb
    sin = (sa * cb + ca * sb) * sin_sign
    o_ref[...] = (jnp.where(pl.program_id(0) == 0, cos, sin) * scale).astype(o_ref.dtype)


def _trig_matrix(name, phase, den, n_rows, n_cols, parts_on_rows, sin_sign=1.0, scale=1.0):
    tile = min(TRIG_TILE, n_rows)
    n_hi = n_rows // tile
    col = jnp.arange(n_cols, dtype=jnp.int32)[None, :]

    def tables(rows):
        ang = (phase(rows[:, None], col) % den).astype(F32) * (2.0 * math.pi / den)
        return jnp.cos(ang), jnp.sin(ang)

    zero = jnp.zeros((1,), jnp.int32)
    base_c, base_s = tables(zero)
    hi_c, hi_s = tables(jnp.arange(n_hi, dtype=jnp.int32) * tile)
    lo_c, lo_s = tables(jnp.arange(tile, dtype=jnp.int32))
    lo_c, lo_s = lo_c * base_c + lo_s * base_s, lo_s * base_c - lo_c * base_s
    if parts_on_rows:
        shape, out_map = (2 * n_rows, n_cols), (lambda p, h: (p * n_hi + h, 0))
    else:
        shape, out_map = (n_rows, 2 * n_cols), (lambda p, h: (h, p))
    body = functools.partial(_trig_body, sin_sign=sin_sign, scale=scale)
    hi_spec = pl.BlockSpec((None, 1, n_cols), lambda p, h: (h, 0, 0))
    lo_spec = pl.BlockSpec((tile, n_cols), lambda p, h: (0, 0))
    return pl.pallas_call(
        body, grid=(2, n_hi), in_specs=[hi_spec, hi_spec, lo_spec, lo_spec],
        out_specs=pl.BlockSpec((tile, n_cols), out_map),
        out_shape=jax.ShapeDtypeStruct(shape, BF16), compiler_params=_params(2), name=name,
    )(hi_c.reshape(n_hi, 1, n_cols), hi_s.reshape(n_hi, 1, n_cols), lo_c, lo_s)


def _position_dft(name, ab, row0, nb, length, fw):
    cs = _trig_matrix(f"{name}_matrix", lambda f, t: f * t, length, length, length, parts_on_rows=False,
                      sin_sign=-1.0, scale=1.0 / math.sqrt(length * HEAD_DIM))
    tm, tn = _tile(length, 1024), _tile(fw, 512)
    nm, nn = length // tm, fw // tn
    rb0 = row0 // length
    return _mm(name, (nb, nm, nn, 2),
               [cs], [pl.BlockSpec((tm, length), lambda b, i, j, k: (i, k))],
               [ab], [pl.BlockSpec((length, tn), lambda b, i, j, k: (rb0 + b, k * nn + j))],
               [], [], [jax.ShapeDtypeStruct((nb * length, fw), BF16)],
               [pl.BlockSpec((tm, tn), lambda b, i, j, k: (b * nm + i, j))],
               [[(0, 0)]], _store0, nk=2, acc_block=(tm, tn))[0]


def _short_conv_body(p_ref, w_ref, b_ref, o_ref):
    u = p_ref[...].astype(F32)
    n = u.shape[0]
    row = lax.broadcasted_iota(jnp.int32, u.shape, 0)
    prev = jnp.where(row == 0, 0.0, pltpu.roll(u, 1, 0))
    nxt = jnp.where(row == n - 1, 0.0, pltpu.roll(u, n - 1, 0))
    w = w_ref[...]
    o_ref[...] = (w[0:1] * prev + w[1:2] * u + w[2:3] * nxt + b_ref[...]).astype(o_ref.dtype)


def _short_conv(name, proj, col0, width, row0, nb, length, conv_w, conv_b):
    tc = _tile(math.gcd(col0, width), 512)
    cb0, rb0 = col0 // tc, row0 // length
    return pl.pallas_call(
        _short_conv_body, grid=(nb, width // tc),
        in_specs=[pl.BlockSpec((length, tc), lambda b, j: (rb0 + b, cb0 + j)),
                  pl.BlockSpec((3, tc), lambda b, j: (0, j)),
                  pl.BlockSpec((1, tc), lambda b, j: (0, j))],
        out_specs=pl.BlockSpec((length, tc), lambda b, j: (b, j)),
        out_shape=jax.ShapeDtypeStruct((nb * length, width), BF16),
        compiler_params=_params(2), name=name,
    )(proj, conv_w, conv_b.reshape(1, width))


def _filter_mlp_body(z_ref, pw_ref, pb_ref, w1_ref, b1_ref, w2_ref, b2_ref, fr_ref, o_ref):
    fr = fr_ref[...]
    h = jnp.sin(fr * (_dot3(z_ref[...], pw_ref[...]) + pb_ref[...]))
    h = jnp.sin(fr * (_dot3(h, w1_ref[...]) + b1_ref[...]))
    o_ref[...] = jnp.sin(fr * (_dot3(h, w2_ref[...]) + b2_ref[...]))


def _filter_out_body(h_ref, wf_ref, wb_ref, d_ref, o_ref):
    h = h_ref[...]
    d = d_ref[...]
    fwd = _dot3(h, wf_ref[...]) * d
    bwd = _dot3(h, wb_ref[...]) * d
    row = lax.broadcasted_iota(jnp.int32, bwd.shape, 0)
    bwd = jnp.where(row == 0, 0.0, bwd)
    o_ref[0] = fwd + bwd
    o_ref[1] = fwd - bwd


def _hyena_filter_sums(length, hw, pe_w, pe_b, w1, b1, w2, b2, w_out, freq):
    t = jnp.linspace(0.0, 1.0, length, dtype=F32)[:, None]
    w = (2.0 * math.pi / length) * jnp.arange(length, dtype=F32)[:, None]
    bands = jnp.linspace(1e-4, HY_POS_BANDS - 1, HY_POS_BANDS, dtype=F32)[None, :]
    z = jnp.concatenate([t, jnp.cos(bands * w), -jnp.sin(bands * w)], axis=-1)
    pos_dim, fwid = pe_w.shape
    z = jnp.pad(z, ((0, 0), (0, LANES - pos_dim)))
    pe_w = jnp.pad(pe_w, ((0, LANES - pos_dim), (0, 0)))
    row = lambda v: v.reshape(1, fwid)
    full = lambda a: pl.BlockSpec(a.shape, lambda i: (0,) * a.ndim)
    args = [z, pe_w, row(pe_b), w1, row(b1), w2, row(b2), row(freq)]
    h = pl.pallas_call(
        _filter_mlp_body, grid=(1,), in_specs=[full(a) for a in args],
        out_specs=pl.BlockSpec((length, fwid), lambda i: (0, 0)),
        out_shape=jax.ShapeDtypeStruct((length, fwid), F32),
        compiler_params=_params(1), name="hyena_filter_mlp",
    )(*args)
    max_decay = math.log(HY_DECAY_TARGET) / HY_FAST_DECAY
    min_decay = math.log(HY_DECAY_TARGET) / HY_SLOW_DECAY
    deltas = jnp.abs(jnp.linspace(min_decay, max_decay, hw, dtype=F32))
    decay = jnp.exp(-t * deltas)
    tc = _tile(hw, 256)
    nc = hw // tc
    return pl.pallas_call(
        _filter_out_body, grid=(HY_ORDER, nc),
        in_specs=[pl.BlockSpec((length, fwid), lambda n, j: (0, 0)),
                  pl.BlockSpec((fwid, tc), lambda n, j: (0, (2 * n) * nc + j)),
                  pl.BlockSpec((fwid, tc), lambda n, j: (0, (2 * n + 1) * nc + j)),
                  pl.BlockSpec((length, tc), lambda n, j: (0, j))],
        out_specs=pl.BlockSpec((2, length, tc), lambda n, j: (0, 0, n * nc + j)),
        out_shape=jax.ShapeDtypeStruct((2, length, HY_ORDER * hw), F32),
        compiler_params=_params(2), name="hyena_filter_out",
    )(h, w_out, w_out, decay)


def _hyena_spectrum_epilogue(parts, es, outs, pids):
    ur, us = parts
    kr, ks = es[0][0], es[0][1]
    outs[0][0] = (ur * kr - us * ks).astype(outs[0].dtype)
    outs[0][1] = (ur * ks + us * kr).astype(outs[0].dtype)


def _hyena_gate_epilogue(parts, es, outs, pids):
    gate_ref, z_ref, skip_ref = es
    gate, z = gate_ref[...].astype(F32), z_ref[...].astype(F32)
    outs[0][...] = (gate * (parts[0] + z * skip_ref[...])).astype(outs[0].dtype)


def _hyena_stream(tag, uc, nb, length, hw, filt, skip):
    fo_b = _trig_matrix(f"hyena_dft_{tag}", lambda f, t: (2 * f + 1) * t, 4 * length, length, length,
                        parts_on_rows=True)
    go_b = _trig_matrix(f"hyena_idft_{tag}", lambda t, f: (2 * f + 1) * t, 4 * length, length, length,
                        parts_on_rows=False, scale=1.0 / length)
    tm, tn = _tile(length, 1024), _tile(hw, 512)
    nm, nn = length // tm, hw // tn
    kspec = _mm(f"hyena_kspec_{tag}", (2 * nm, HY_ORDER * nn),
                [fo_b], [pl.BlockSpec((tm, length), lambda i, j: (i, 0))],
                [filt.reshape(2 * length, HY_ORDER * hw)],
                [pl.BlockSpec((length, tn), lambda i, j: (i // nm, j))],
                [], [], [jax.ShapeDtypeStruct((2 * length, HY_ORDER * hw), F32)],
                [pl.BlockSpec((tm, tn), lambda i, j: (i, j))], [[(0, 0)]], _store0)[0]
    kspec = kspec.reshape(2, length, HY_ORDER * hw)
    z = uc
    for n in range(HY_ORDER):
        z = _hyena_order(tag, n, uc, z, nb, length, hw, fo_b, go_b, kspec, skip, (tm, tn, nm, nn))
    return z


def _hyena_order(tag, n, uc, z, nb, length, hw, fo_b, go_b, kspec, skip, tiles):
    tm, tn, nm, nn = tiles
    gate_cb = (n + 1) * nn
    spec = _mm(f"hyena_fwd{n}_{tag}", (nm, nb, nn),
               [fo_b, fo_b], [pl.BlockSpec((tm, length), lambda i, b, j: (i, 0)),
                              pl.BlockSpec((tm, length), lambda i, b, j: (nm + i, 0))],
               [z], [pl.BlockSpec((length, tn), lambda i, b, j: (b, j))],
               [kspec], [pl.BlockSpec((2, tm, tn), lambda i, b, j: (0, i, n * nn + j))],
               [jax.ShapeDtypeStruct((nb, 2, length, hw), BF16)],
               [pl.BlockSpec((None, 2, tm, tn), lambda i, b, j: (b, 0, i, j))],
               [[(0, 0)], [(1, 0)]], _hyena_spectrum_epilogue)[0]
    return _mm(f"hyena_inv{n}_{tag}", (nb, nm, nn),
               [go_b], [pl.BlockSpec((tm, 2 * length), lambda b, i, j: (i, 0))],
               [spec.reshape(nb * 2 * length, hw)],
               [pl.BlockSpec((2 * length, tn), lambda b, i, j: (b, j))],
               [uc, z, skip.reshape(HY_ORDER, 1, hw)],
               [pl.BlockSpec((tm, tn), lambda b, i, j: (b * nm + i, gate_cb + j)),
                pl.BlockSpec((tm, tn), lambda b, i, j: (b * nm + i, j)),
                pl.BlockSpec((None, 1, tn), lambda b, i, j: (n, 0, j))],
               [jax.ShapeDtypeStruct((nb * length, hw), BF16)],
               [pl.BlockSpec((tm, tn), lambda b, i, j: (b * nm + i, j))],
               [[(0, 0)]], _hyena_gate_epilogue)[0]


def _resgate_epilogue(parts, es, outs, pids):
    res_ref, gate_ref = es[0], es[1]
    outs[0][...] = res_ref[...] + gate_ref[...] * parts[0]


def _mixer_out(name, res, za, zb, w_out, gate, seg_of_tile, in_place):
    rows, half = za.shape
    d = res.shape[-1]
    tm, tn = _tile(rows, 1024), _tile(d, 512)
    if res.ndim == 3:
        assert not in_place
        tm = _tile(res.shape[1], 1024)
        per = res.shape[1] // tm
        res_spec = pl.BlockSpec((None, tm, tn), lambda i, j: (i // per, i % per, j))
    else:
        res_spec = pl.BlockSpec((tm, tn), lambda i, j: (i, j))
    seg = seg_of_tile(tm)
    return _mm(name, (rows // tm, d // tn),
               [za, zb], [pl.BlockSpec((tm, half), lambda i, j: (i, 0))] * 2,
               [w_out, w_out], [pl.BlockSpec((half, tn), lambda i, j: (0, j)),
                                pl.BlockSpec((half, tn), lambda i, j: (1, j))],
               [res, gate], [res_spec, pl.BlockSpec((None, 1, tn), lambda i, j: (seg(i), 0, j))],
               [jax.ShapeDtypeStruct((rows, d), F32)],
               [pl.BlockSpec((tm, tn), lambda i, j: (i, j))],
               [[(0, 0), (1, 1)]], _resgate_epilogue, aliases={4: 0} if in_place else {})[0]


def _swiglu_epilogue(parts, es, outs, pids):
    outs[0][...] = (_silu(parts[0]) * parts[1]).astype(outs[0].dtype)


def _ffn_up(name, h, w1, w3):
    m, k = h.shape
    n = w1.shape[1]
    tm, tn = _tile(m, 1024), _tile(n, 256)
    return _mm(name, (m // tm, n // tn),
               [h], [pl.BlockSpec((tm, k), lambda i, j: (i, 0))],
               [w1, w3], [pl.BlockSpec((k, tn), lambda i, j: (0, j))] * 2,
               [], [], [jax.ShapeDtypeStruct((m, n), BF16)],
               [pl.BlockSpec((tm, tn), lambda i, j: (i, j))],
               [[(0, 0)], [(0, 1)]], _swiglu_epilogue)[0]


def _down_residual(name, x_res, rows, u, w2, gate, seg_of_tile, tm_pref, tk_pref, tn_pref):
    d = x_res.shape[1]
    kdim = u.shape[1]
    tm, tn, tk = _tile(rows, tm_pref), _tile(d, tn_pref), _tile(kdim, tk_pref)
    nk = kdim // tk
    seg = seg_of_tile(tm)
    return _mm(name, (rows // tm, d // tn, nk),
               [u], [pl.BlockSpec((tm, tk), lambda i, j, k: (i, k))],
               [w2], [pl.BlockSpec((tk, tn), lambda i, j, k: (k, j))],
               [x_res, gate], [pl.BlockSpec((tm, tn), lambda i, j, k: (i, j)),
                               pl.BlockSpec((None, 1, tn), lambda i, j, k: (seg(i), 0, j))],
               [jax.ShapeDtypeStruct(x_res.shape, F32)],
               [pl.BlockSpec((tm, tn), lambda i, j, k: (i, j))],
               [[(0, 0)]], _resgate_epilogue, nk=nk, acc_block=(tm, tn), aliases={2: 0})[0]


def _moe_plan(route, n_experts):
    t = route.shape[0]
    n_tiles = TOP_K * t // MOE_TILE + n_experts
    expert = route[:, ROUTE_EXPERT:ROUTE_EXPERT + TOP_K].astype(jnp.int32).T.reshape(-1)
    onehot = (expert[:, None] == jnp.arange(n_experts, dtype=jnp.int32)[None, :]).astype(jnp.int32)
    running = jnp.cumsum(onehot, axis=0)
    rank = jnp.take_along_axis(running, expert[:, None], axis=1)[:, 0] - 1
    tiles_per_expert = (running[-1] + MOE_TILE - 1) // MOE_TILE
    tile_end = jnp.cumsum(tiles_per_expert)
    row = (tile_end - tiles_per_expert)[expert] * MOE_TILE + rank
    token = jnp.tile(jnp.arange(t, dtype=jnp.int32), TOP_K)
    row_token = jnp.zeros((n_tiles * MOE_TILE,), jnp.int32).at[row].set(token)
    tile_expert = jnp.searchsorted(tile_end, jnp.arange(n_tiles, dtype=jnp.int32), side="right")
    tile_expert = jnp.minimum(tile_expert, n_experts - 1).astype(jnp.int32)
    pos = row.reshape(TOP_K, t // MOE_COMBINE_TILE, MOE_COMBINE_TILE).transpose(1, 0, 2)
    return row_token.reshape(n_tiles, 1, MOE_TILE), tile_expert, tile_end[-1:].astype(jnp.int32), pos


def _row_copy(src_hbm, row, dst_ref, dst_row, sem):
    return pltpu.make_async_copy(src_hbm.at[pl.ds(row, 1)], dst_ref.at[pl.ds(dst_row, 1)], sem)


def _wait_rows(src_hbm, buf_ref, sem):
    pltpu.make_async_copy(src_hbm.at[pl.ds(0, buf_ref.shape[0])], buf_ref, sem).wait()


def _moe_gather_body(na_ref, tok_ref, src_hbm, o_ref, buf_ref, sem):
    rows = buf_ref.shape[0]
    active = pl.program_id(0) < na_ref[0]

    @pl.when(active)
    def _():
        def issue(r, carry):
            _row_copy(src_hbm, tok_ref[0, r], buf_ref, r, sem).start()
            return carry

        lax.fori_loop(0, rows, issue, 0, unroll=DMA_ISSUE_UNROLL)
        _wait_rows(src_hbm, buf_ref, sem)
        n = buf_ref.shape[1]
        o_ref[:, :n], o_ref[:, n:] = _unpack_bf16_pairs(buf_ref[...])

    @pl.when(jnp.logical_not(active))
    def _():
        o_ref[...] = jnp.zeros(o_ref.shape, o_ref.dtype)


def _dma_params(n_grid):
    return pltpu.CompilerParams(dimension_semantics=("arbitrary",) * n_grid, vmem_limit_bytes=V7X_VMEM_LIMIT,
                                disable_bounds_checks=True)


def _moe_gather(h, row_token, n_active):
    n_tiles = row_token.shape[0]
    d = 2 * h.shape[1]
    return pl.pallas_call(
        _moe_gather_body, grid=(n_tiles,),
        in_specs=[pl.BlockSpec(memory_space=pltpu.SMEM),
                  pl.BlockSpec((None, 1, MOE_TILE), lambda i: (i, 0, 0), memory_space=pltpu.SMEM),
                  pl.BlockSpec(memory_space=pl.ANY)],
        out_specs=pl.BlockSpec((MOE_TILE, d), lambda i: (i, 0)),
        out_shape=jax.ShapeDtypeStruct((n_tiles * MOE_TILE, d), BF16),
        scratch_shapes=[pltpu.VMEM((MOE_TILE,) + h.shape[1:], h.dtype), pltpu.SemaphoreType.DMA(())],
        compiler_params=_dma_params(1), name="moe_gather",
    )(n_active, row_token, h)


def _moe_up_body(te_ref, na_ref, x_ref, w1_ref, w3_ref, o_ref):
    active = pl.program_id(1) < na_ref[0]

    @pl.when(active)
    def _():
        x = x_ref[...]
        o_ref[...] = (_silu(_dot(x, w1_ref[...])) * _dot(x, w3_ref[...])).astype(o_ref.dtype)

    @pl.when(jnp.logical_not(active))
    def _():
        o_ref[...] = jnp.zeros(o_ref.shape, o_ref.dtype)


def _moe_down_body(te_ref, na_ref, u_ref, w2_ref, o_ref):
    active = pl.program_id(1) < na_ref[0]

    @pl.when(active)
    def _():
        o_ref[...] = _dot(u_ref[...], w2_ref[...])

    @pl.when(jnp.logical_not(active))
    def _():
        o_ref[...] = jnp.zeros(o_ref.shape, o_ref.dtype)


def _moe_experts(xs, tile_expert, n_active, w1, w3, w2):
    ne, d, f = w1.shape
    n_tiles = xs.shape[0] // MOE_TILE
    tn = _tile(f, 512)
    up = pl.pallas_call(
        _moe_up_body,
        grid_spec=pltpu.PrefetchScalarGridSpec(
            num_scalar_prefetch=2, grid=(f // tn, n_tiles),
            in_specs=[pl.BlockSpec((MOE_TILE, d), lambda j, i, te, na: (i, 0)),
                      pl.BlockSpec((None, d, tn), lambda j, i, te, na: (te[i], 0, j)),
                      pl.BlockSpec((None, d, tn), lambda j, i, te, na: (te[i], 0, j))],
            out_specs=pl.BlockSpec((MOE_TILE, tn), lambda j, i, te, na: (i, j))),
        out_shape=jax.ShapeDtypeStruct((xs.shape[0], f), BF16),
        compiler_params=_params(2), name="moe_up",
    )(tile_expert, n_active, xs, w1, w3)
    tn = _tile(d, 1024)
    return pl.pallas_call(
        _moe_down_body,
        grid_spec=pltpu.PrefetchScalarGridSpec(
            num_scalar_prefetch=2, grid=(d // tn, n_tiles),
            in_specs=[pl.BlockSpec((MOE_TILE, f), lambda j, i, te, na: (i, 0)),
                      pl.BlockSpec((None, f, tn), lambda j, i, te, na: (te[i], 0, j))],
            out_specs=pl.BlockSpec((MOE_TILE, tn), lambda j, i, te, na: (i, j))),
        out_shape=jax.ShapeDtypeStruct((xs.shape[0], d), F32),
        compiler_params=_params(2), name="moe_down",
    )(tile_expert, n_active, up, w2)


def _moe_combine_body(pos_ref, ys_hbm, res_ref, route_ref, gate_ref, o_ref, buf_ref, sem):
    rows = res_ref.shape[0]

    def issue(r, carry):
        for k in range(TOP_K):
            _row_copy(ys_hbm, pos_ref[k, r], buf_ref.at[k], r, sem).start()
        return carry

    lax.fori_loop(0, rows, issue, 0, unroll=DMA_ISSUE_UNROLL)
    for k in range(TOP_K):
        _wait_rows(ys_hbm, buf_ref.at[k], sem)
    route = route_ref[...]
    lane = lax.broadcasted_iota(jnp.int32, route.shape, 1)
    moe = None
    for k in range(TOP_K):
        g = jnp.sum(jnp.where(lane == ROUTE_GATE + k, route, 0.0), axis=-1, keepdims=True)
        term = g * buf_ref[k]
        moe = term if moe is None else moe + term
    o_ref[...] = res_ref[...] + gate_ref[...] * moe


def _moe_combine(x_res, rows, ys, pos, route, gate, seg_of_tile):
    d = x_res.shape[1]
    tm = MOE_COMBINE_TILE
    seg = seg_of_tile(tm)
    return pl.pallas_call(
        _moe_combine_body, grid=(rows // tm,),
        in_specs=[pl.BlockSpec((None, TOP_K, tm), lambda i: (i, 0, 0), memory_space=pltpu.SMEM),
                  pl.BlockSpec(memory_space=pl.ANY),
                  pl.BlockSpec((tm, d), lambda i: (i, 0)),
                  pl.BlockSpec((tm, LANES), lambda i: (i, 0)),
                  pl.BlockSpec((None, 1, d), lambda i: (seg(i), 0, 0))],
        out_specs=pl.BlockSpec((tm, d), lambda i: (i, 0)),
        out_shape=jax.ShapeDtypeStruct(x_res.shape, F32),
        scratch_shapes=[pltpu.VMEM((TOP_K, tm, d), F32), pltpu.SemaphoreType.DMA(())],
        input_output_aliases={2: 0},
        compiler_params=_dma_params(1), name="moe_combine",
    )(pos, ys, x_res, route, gate)


NA_BLOCK_ROWS = 4


def _na_body(q_ref, k_ref, v_ref, kc_ref, vc_ref, band_ref, o_ref, va_ref, bias_ref, *, seq, ctx_len, tables,
             blocks, win):
    qscale = HEAD_DIM ** -0.5 * math.log2(math.e)
    bq = NA_BLOCK_ROWS * GRID_W

    @pl.when(pl.program_id(1) == 0)
    def _():
        for case, table in enumerate(tables):
            for ri, row in enumerate(table):
                for rj, entry in enumerate(row):
                    bias_ref[case, ri * GRID_W:(ri + 1) * GRID_W, rj * GRID_W:(rj + 1) * GRID_W] = band_ref[int(entry)]

    va_ref[0:seq, 0:HEAD_DIM] = v_ref[...].astype(BF16)
    va_ref[seq:seq + ctx_len, 0:HEAD_DIM] = vc_ref[...].astype(BF16)
    lane = lax.broadcasted_iota(jnp.int32, (seq + ctx_len, va_ref.shape[1] - HEAD_DIM), 1)
    va_ref[:, HEAD_DIM:] = jnp.where(lane == 0, 1.0, 0.0).astype(BF16)
    kc = kc_ref[...].astype(BF16)
    for kb, (case, ws) in enumerate(blocks):
        q = (q_ref[kb * bq:(kb + 1) * bq, :] * qscale).astype(BF16)
        k0 = ws * GRID_W
        s_loc = _dot_nt(q, k_ref[k0:k0 + win, :]) + bias_ref[case]
        s_ctx = _dot_nt(q, kc)
        m = jnp.maximum(jnp.max(s_loc, axis=-1, keepdims=True), jnp.max(s_ctx, axis=-1, keepdims=True))
        acc = _dot(jnp.exp2(s_loc - m), va_ref[k0:k0 + win, :]) + _dot(jnp.exp2(s_ctx - m), va_ref[seq:seq + ctx_len, :])
        o_ref[kb * bq:(kb + 1) * bq, :] = (acc[:, :HEAD_DIM] / acc[:, HEAD_DIM:HEAD_DIM + 1]).astype(o_ref.dtype)


def _na_bias_slabs(rpb, grid_rows, win_rows):
    heads = rpb.shape[0]
    cols = np.arange(GRID_W)
    start = np.clip(cols - NA_WIN_C // 2, 0, GRID_W - NA_WIN_C)
    kc = cols[None, :]
    inside = (kc >= start[:, None]) & (kc < start[:, None] + NA_WIN_C)
    dc = np.clip(kc - cols[:, None] + (NA_WIN_C - 1), 0, 2 * NA_WIN_C - 2)
    band = jnp.where(jnp.asarray(inside)[None, None], rpb[:, :, dc], NEG_BIG)
    masked = band.shape[1]
    band = jnp.concatenate([band, jnp.full((heads, 1, GRID_W, GRID_W), NEG_BIG, F32)], axis=1)
    win = min(win_rows + NA_BLOCK_ROWS, grid_rows)
    tables, blocks = [], []
    for r0 in range(0, grid_rows, NA_BLOCK_ROWS):
        ws = int(np.clip(r0 - win_rows // 2, 0, grid_rows - win))
        idx = np.full((NA_BLOCK_ROWS, win), masked)
        for ri in range(NA_BLOCK_ROWS):
            r = r0 + ri
            rs = int(np.clip(r - win_rows // 2, 0, grid_rows - win_rows))
            assert ws <= rs and rs + win_rows <= ws + win
            for key_row in range(rs, rs + win_rows):
                idx[ri, key_row - ws] = key_row - r + (NA_WIN_R - 1)
        for case, t in enumerate(tables):
            if np.array_equal(t, idx):
                break
        else:
            case = len(tables)
            tables.append(idx)
        blocks.append((case, ws))
    return band * math.log2(math.e), tuple(tuple(map(tuple, t)) for t in tables), tuple(blocks), win * GRID_W


def _neighbourhood_attention(proj, proj_ctx, nb, seq, ctx_len, heads, rpb):
    grid_rows = seq // GRID_W
    assert grid_rows % NA_BLOCK_ROWS == 0
    win_rows = min(NA_WIN_R, grid_rows)
    band, tables, blocks, win = _na_bias_slabs(rpb, grid_rows, win_rows)
    body = functools.partial(_na_body, seq=seq, ctx_len=ctx_len, tables=tables, blocks=blocks, win=win)
    return pl.pallas_call(
        body, grid=(heads, nb),
        in_specs=[pl.BlockSpec((seq, HEAD_DIM), lambda h, b: (b, h)),
                  pl.BlockSpec((seq, HEAD_DIM), lambda h, b: (b, heads + h)),
                  pl.BlockSpec((seq, HEAD_DIM), lambda h, b: (b, 2 * heads + h)),
                  pl.BlockSpec((ctx_len, HEAD_DIM), lambda h, b: (b, heads + h)),
                  pl.BlockSpec((ctx_len, HEAD_DIM), lambda h, b: (b, 2 * heads + h)),
                  pl.BlockSpec((None,) + band.shape[1:], lambda h, b: (h, 0, 0, 0))],
        out_specs=pl.BlockSpec((seq, HEAD_DIM), lambda h, b: (b, h)),
        out_shape=jax.ShapeDtypeStruct((nb * seq, heads * HEAD_DIM), BF16),
        scratch_shapes=[pltpu.VMEM((seq + ctx_len, HEAD_DIM + LANES), BF16),
                        pltpu.VMEM((len(tables), NA_BLOCK_ROWS * GRID_W, win), F32)],
        compiler_params=_params(2), name="neighbourhood_attention",
    )(proj, proj, proj, proj_ctx, proj_ctx, band)


def _mla_body(q_ref, kvl_ref, kvc_ref, krl_ref, krc_ref, cq_ref, sq_ref, ck_ref, sk_ref, o_ref,
              qf_ref, kf_ref, va_ref, *, nope, vdim, seq, ctx_len, chunks):
    qscale = (nope + MLA_ROPE) ** -0.5 * math.log2(math.e)
    rope_end = nope + MLA_ROPE

    @pl.when(pl.program_id(2) == 0)
    def _():
        krl = krl_ref[...]
        kf_ref[0:seq, 0:nope] = kvl_ref[:, 0:nope].astype(BF16)
        kf_ref[0:seq, nope:rope_end] = (krl[:, :MLA_ROPE] * ck_ref[...] + krl[:, MLA_ROPE:] * sk_ref[...]).astype(BF16)
        kf_ref[seq:seq + ctx_len, 0:nope] = kvc_ref[:, 0:nope].astype(BF16)
        kf_ref[seq:seq + ctx_len, nope:rope_end] = krc_ref[:, 0:MLA_ROPE].astype(BF16)
        kf_ref[:, rope_end:] = jnp.zeros((seq + ctx_len, kf_ref.shape[1] - rope_end), BF16)
        va_ref[0:seq, 0:vdim] = kvl_ref[:, nope:nope + vdim].astype(BF16)
        va_ref[seq:seq + ctx_len, 0:vdim] = kvc_ref[:, nope:nope + vdim].astype(BF16)
        lane = lax.broadcasted_iota(jnp.int32, (seq + ctx_len, va_ref.shape[1] - vdim), 1)
        va_ref[:, vdim:] = jnp.where(lane == 0, 1.0, 0.0).astype(BF16)

    q = q_ref[...]
    qf_ref[:, 0:nope] = (q[:, :nope] * qscale).astype(BF16)
    qr = q[:, nope:rope_end] * cq_ref[...] + q[:, rope_end:] * sq_ref[...]
    qf_ref[:, nope:rope_end] = (qr * qscale).astype(BF16)
    qf_ref[:, rope_end:] = jnp.zeros((q.shape[0], qf_ref.shape[1] - rope_end), BF16)
    qf = qf_ref[...]
    m = jnp.full((q.shape[0], 1), -jnp.inf, F32)
    acc = jnp.zeros((q.shape[0], va_ref.shape[1]), F32)
    for start, size in chunks:
        s = _dot_nt(qf, kf_ref[start:start + size, :])
        m_new = jnp.maximum(m, jnp.max(s, axis=-1, keepdims=True))
        p = jnp.exp2(s - m_new)
        acc = acc * jnp.exp2(m - m_new) + _dot(p, va_ref[start:start + size, :])
        m = m_new
    o_ref[...] = (acc[:, :vdim] / acc[:, vdim:vdim + 1]).astype(o_ref.dtype)


def _rope_tables(seq):
    pos = np.arange(seq)
    row = (pos // GRID_W).astype(np.float32)
    col = (pos % GRID_W).astype(np.float32)
    half = MLA_ROPE // 2
    inv = jnp.asarray(ROPE_BASE, F32) ** (-jnp.arange(0, half, 2, dtype=F32) / half)
    ang_r = jnp.asarray(row)[:, None] * inv
    ang_c = jnp.asarray(col)[:, None] * inv
    ang = jnp.concatenate([ang_r, ang_r, ang_c, ang_c], axis=-1)
    return jnp.cos(ang), jnp.sin(ang)


def _rotate_partner_columns(w):
    q = MLA_ROPE // 4
    parts = [w[:, i * q:(i + 1) * q] for i in range(4)]
    return jnp.concatenate([-parts[1], parts[0], -parts[3], parts[2]], axis=1)


def _latent_attention(q, kv, kv_ctx, kr, kr_ctx, nb, seq, ctx_len, heads, nope, vdim):
    cos, sin = _rope_tables(seq)
    tq = _tile(seq, 1024)
    nq = seq // tq
    qw = nope + 2 * MLA_ROPE
    keys = seq + ctx_len
    n_chunks = -(-keys // 1024)
    while keys % (n_chunks * LANES):
        n_chunks += 1
    chunk = keys // n_chunks
    chunks = tuple((s, chunk) for s in range(0, keys, chunk))
    kw = nope + 2 * MLA_ROPE
    vw = vdim + LANES
    body = functools.partial(_mla_body, nope=nope, vdim=vdim, seq=seq, ctx_len=ctx_len, chunks=chunks)
    return pl.pallas_call(
        body, grid=(nb, heads, nq),
        scratch_shapes=[pltpu.VMEM((tq, kw), BF16), pltpu.VMEM((keys, kw), BF16), pltpu.VMEM((keys, vw), BF16)],
        in_specs=[pl.BlockSpec((tq, qw), lambda b, h, i: (b * nq + i, h)),
                  pl.BlockSpec((seq, nope + vdim), lambda b, h, i: (b, h)),
                  pl.BlockSpec((ctx_len, nope + vdim), lambda b, h, i: (b, h)),
                  pl.BlockSpec((seq, 2 * MLA_ROPE), lambda b, h, i: (b, 0)),
                  pl.BlockSpec((ctx_len, 2 * MLA_ROPE), lambda b, h, i: (b, 0)),
                  pl.BlockSpec((tq, MLA_ROPE), lambda b, h, i: (i, 0)),
                  pl.BlockSpec((tq, MLA_ROPE), lambda b, h, i: (i, 0)),
                  pl.BlockSpec((seq, MLA_ROPE), lambda b, h, i: (0, 0)),
                  pl.BlockSpec((seq, MLA_ROPE), lambda b, h, i: (0, 0))],
        out_specs=pl.BlockSpec((tq, vdim), lambda b, h, i: (b * nq + i, h)),
        out_shape=jax.ShapeDtypeStruct((nb * seq, heads * vdim), BF16),
        compiler_params=_params(3), name="latent_attention",
    )(q, kv, kv_ctx, kr, kr_ctx, cos, sin, cos, sin)


def kernel(x, c, ctx, c_ctx, mod_w, mod_b, norm_mix_g, norm_ffn_g, ev_w_in, ev_w_out, hy_conv_w, hy_conv_b, hy_pe_w, hy_pe_b, hy_w1, hy_b1, hy_w2, hy_b2, hy_w_out, hy_freq, hy_skip, ffn_w1, ffn_w3, ffn_w2, od_w_in, od_w_out, na_rpb, mla_q_g, mla_w_uq, mla_kv_g, mla_w_ukv, moe_router_w, moe_router_b, moe_w1, moe_w3, moe_w2, final_g):
    nb, seq, d = x.shape
    ctx_len = ctx.shape[1]
    depth = mod_w.shape[0]
    assert depth == 2 and ev_w_in.shape[0] == 1 and od_w_in.shape[0] == 1, "even layer then odd (last) layer"
    lat_rows, ctx_rows = nb * seq, nb * ctx_len
    assert nb + 1 <= MOD_ROWS

    def lat_seg(tile_rows):
        assert seq % tile_rows == 0
        return lambda i: (i * tile_rows) // seq

    def ctx_seg(tile_rows):
        return lambda i: nb

    c_all = jnp.concatenate([c, c_ctx[None, :], jnp.zeros((MOD_ROWS - nb - 1, d), F32)], axis=0)
    mod = _modulation(c_all, mod_w, mod_b).reshape(depth, MOD_ROWS, N_MOD, 1, d)
    mods = [[mod[i, :, m] for m in range(N_MOD)] for i in range(depth)]

    sh1, sc1, g1, sh2, sc2, g2 = mods[0]
    fw = (ev_w_out.shape[1]) // 2
    hw = ev_w_out.shape[1] - fw
    filt_args = (hy_pe_w[0], hy_pe_b[0], hy_w1[0], hy_b1[0], hy_w2[0], hy_b2[0], hy_w_out[0], hy_freq[0])
    w2_b = ffn_w2[0].astype(BF16)
    streams = {}
    for tag, src, length, seg_fn in (("lat", x, seq, lat_seg), ("ctx", ctx, ctx_len, ctx_seg)):
        h = _rmsnorm(f"norm_mix0_{tag}", src, norm_mix_g[0], BF16, mod=(sh1, sc1), seg_of_tile=seg_fn)[0]
        proj = _linear(f"even_in_{tag}", h, ev_w_in[0], BF16)
        ab = _group_dft(proj, fw)
        zf = _position_dft(f"fourier_{tag}", ab, 0, nb, length, fw)
        uc = _short_conv(f"short_conv_{tag}", proj, fw, (HY_ORDER + 1) * hw, 0, nb, length,
                         hy_conv_w[0], hy_conv_b[0])
        filt = _hyena_filter_sums(length, hw, *filt_args)
        zh = _hyena_stream(tag, uc, nb, length, hw, filt, hy_skip[0])
        xs_ = _mixer_out(f"even_out_{tag}", src, zf, zh, ev_w_out[0], g1, seg_fn, in_place=False)
        h2 = _rmsnorm(f"norm_ffn0_{tag}", xs_, norm_ffn_g[0], BF16, mod=(sh2, sc2), seg_of_tile=seg_fn)[0]
        u = _ffn_up(f"ffn_up_{tag}", h2, ffn_w1[0], ffn_w3[0])
        streams[tag] = _down_residual(f"ffn_down_{tag}", xs_, xs_.shape[0], u, w2_b, g2, seg_fn,
                                      tm_pref=512, tk_pref=u.shape[1], tn_pref=512)
    xl, xc = streams["lat"], streams["ctx"]

    sh1, sc1, g1, sh2, sc2, g2 = mods[1]
    w_in = od_w_in[0]
    na_w = od_w_out.shape[1] // 2
    heads = na_w // HEAD_DIM
    q_rank = mla_w_uq.shape[1]
    kv_rank = mla_w_ukv.shape[1]
    mla_heads = (od_w_out.shape[1] - na_w) // HEAD_DIM
    nope = mla_w_uq.shape[2] // mla_heads - MLA_ROPE
    vdim = mla_w_ukv.shape[2] // mla_heads - nope
    main_cols = 3 * na_w + q_rank + kv_rank
    w_kr = w_in[:, main_cols:main_cols + MLA_ROPE]
    w_kr = jnp.concatenate([w_kr, _rotate_partner_columns(w_kr)], axis=1)
    kv_col_block = (3 * na_w + q_rank) // kv_rank
    proj, kr, kv = {}, {}, {}
    for tag, xs_, seg_fn in (("lat", xl, lat_seg), ("ctx", xc, ctx_seg)):
        h = _rmsnorm(f"norm_mix1_{tag}", xs_, norm_mix_g[1], BF16, mod=(sh1, sc1), seg_of_tile=seg_fn)[0]
        proj[tag] = _linear(f"odd_in_{tag}", h, w_in, F32, n_cols=main_cols)
        kr[tag] = _linear(f"odd_in_rope_{tag}", h, w_kr, F32)
        ckv = _rmsnorm(f"norm_kv_{tag}", proj[tag], mla_kv_g[0], BF16, width=kv_rank, col_block=kv_col_block)[0]
        kv[tag] = _linear(f"mla_kv_{tag}", ckv, mla_w_ukv[0], BF16, tn_pref=2048)
    cq = _rmsnorm("norm_q", proj["lat"], mla_q_g[0], BF16, width=q_rank, col_block=3 * na_w // q_rank)[0]
    w_uq = mla_w_uq[0].reshape(q_rank, mla_heads, nope + MLA_ROPE)
    w_uq = jnp.concatenate([w_uq, jnp.stack([_rotate_partner_columns(w_uq[:, hh, nope:])
                                             for hh in range(mla_heads)], axis=1)], axis=-1)
    q = _linear("mla_q", cq, w_uq.reshape(q_rank, mla_heads * (nope + 2 * MLA_ROPE)), F32, tn_pref=2048)
    z_na = _neighbourhood_attention(proj["lat"], proj["ctx"], nb, seq, ctx_len, heads, na_rpb[0])
    z_mla = _latent_attention(q, kv["lat"], kv["ctx"], kr["lat"], kr["ctx"], nb, seq, ctx_len, mla_heads, nope, vdim)
    xl = _mixer_out("odd_out", xl, z_na, z_mla, od_w_out[0], g1, lat_seg, in_place=True)
    h2, route = _rmsnorm("norm_ffn1", xl, norm_ffn_g[1], jnp.uint32, mod=(sh2, sc2), seg_of_tile=lat_seg,
                         router=(moe_router_w[0], moe_router_b[0]))
    assert lat_rows % MOE_TILE == 0 and lat_rows % MOE_COMBINE_TILE == 0
    row_token, tile_expert, n_active, pos = _moe_plan(route, moe_w1.shape[1])
    xs = _moe_gather(h2, row_token, n_active)
    ys = _moe_experts(xs, tile_expert, n_active, moe_w1[0], moe_w3[0], moe_w2[0])
    xl = _moe_combine(xl, lat_rows, ys, pos, route, g2, lat_seg)
    out = _rmsnorm("norm_final", xl, final_g, F32)[0]
    return out.reshape(nb, seq, d)
```

```python
import functools
import math

import jax
import jax.numpy as jnp
import numpy as np
from jax import lax
from jax.experimental import pallas as pl
from jax.experimental.pallas import tpu as pltpu

F32 = jnp.float32
BF16 = jnp.bfloat16

GRID_W = 64
HEAD_DIM = 128
N_MOD = 6
HY_ORDER = 2
HY_POS_BANDS = 16
HY_DECAY_TARGET = 1e-2
HY_FAST_DECAY = 0.3
HY_SLOW_DECAY = 1.5
NA_WIN_R = 8
NA_WIN_C = 16
MLA_ROPE = 64
ROPE_BASE = 10000.0
TOP_K = 2
EPS = 1e-6

LANES = 128
MOD_ROWS = 8
V7X_VMEM_LIMIT = 58 * 2 ** 20
NEG_BIG = -1e30
ROUTE_EXPERT = 0
ROUTE_GATE = TOP_K
MOE_TILE = 512
MOE_COMBINE_TILE = 256
DMA_ISSUE_UNROLL = 8
TRIG_TILE = 128


def _tile(n, pref):
    if n <= pref:
        return n
    t = (pref // LANES) * LANES
    while t >= LANES:
        if n % t == 0:
            return t
        t -= LANES
    raise ValueError(f"no lane-aligned tile for {n} <= {pref}")


def _params(n_grid):
    return pltpu.CompilerParams(dimension_semantics=("arbitrary",) * n_grid,
                                vmem_limit_bytes=V7X_VMEM_LIMIT)


def _bf(a):
    return a if a.dtype == BF16 else a.astype(BF16)


def _dot(a, b):
    return jnp.dot(_bf(a), _bf(b), preferred_element_type=F32)


def _dot_nt(a, b):
    return lax.dot_general(_bf(a), _bf(b), (((1,), (1,)), ((), ())), preferred_element_type=F32)


def _split(a):
    hi = a.astype(BF16)
    lo = (a - hi.astype(F32)).astype(BF16)
    return hi, lo


def _dot3(a, b):
    ah, al = _split(a)
    bh, bl = _split(b)
    return (jnp.dot(ah, bh, preferred_element_type=F32) + jnp.dot(ah, bl, preferred_element_type=F32)
            + jnp.dot(al, bh, preferred_element_type=F32))


def _silu(a):
    return a * (1.0 / (1.0 + jnp.exp(-a)))


def _mm_body(*refs, n_x, n_w, n_e, n_o, groups, nk, epilogue, n_grid):
    xs = refs[:n_x]
    ws = refs[n_x:n_x + n_w]
    es = refs[n_x + n_w:n_x + n_w + n_e]
    outs = refs[n_x + n_w + n_e:n_x + n_w + n_e + n_o]
    accs = refs[n_x + n_w + n_e + n_o:]
    pids = [pl.program_id(a) for a in range(n_grid)]
    xv = [_bf(x[...]) for x in xs]
    wv = [_bf(w[...]) for w in ws]
    parts = []
    for group in groups:
        p = None
        for xi, wi in group:
            d = jnp.dot(xv[xi], wv[wi], preferred_element_type=F32)
            p = d if p is None else p + d
        parts.append(p)
    if nk == 1:
        epilogue(parts, es, outs, pids)
        return
    k = pids[-1]

    @pl.when(k == 0)
    def _():
        for a, p in zip(accs, parts):
            a[...] = p

    @pl.when(k > 0)
    def _():
        for a, p in zip(accs, parts):
            a[...] += p

    @pl.when(k == nk - 1)
    def _():
        epilogue([a[...] for a in accs], es, outs, pids)


def _mm(name, grid, xs, x_specs, ws, w_specs, es, e_specs, out_shapes, out_specs, groups, epilogue,
        nk=1, acc_block=None, aliases=None):
    body = functools.partial(_mm_body, n_x=len(xs), n_w=len(ws), n_e=len(es), n_o=len(out_shapes),
                             groups=groups, nk=nk, epilogue=epilogue, n_grid=len(grid))
    scratch = [pltpu.VMEM(acc_block, F32) for _ in groups] if nk > 1 else []
    res = pl.pallas_call(
        body, grid=grid, in_specs=list(x_specs) + list(w_specs) + list(e_specs),
        out_specs=list(out_specs), out_shape=list(out_shapes), scratch_shapes=scratch,
        input_output_aliases=aliases or {}, compiler_params=_params(len(grid)), name=name,
    )(*xs, *ws, *es)
    return res


def _store0(parts, es, outs, pids):
    outs[0][...] = parts[0].astype(outs[0].dtype)


def _linear(name, x, w, out_dtype, *, n_cols=None, tm_pref=1024, tn_pref=512):
    m, k = x.shape
    n = n_cols or w.shape[1]
    tm, tn = _tile(m, tm_pref), _tile(n, tn_pref)
    return _mm(name, (m // tm, n // tn),
               [x], [pl.BlockSpec((tm, k), lambda i, j: (i, 0))],
               [w], [pl.BlockSpec((k, tn), lambda i, j: (0, j))],
               [], [], [jax.ShapeDtypeStruct((m, n), out_dtype)],
               [pl.BlockSpec((tm, tn), lambda i, j: (i, j))], [[(0, 0)]], _store0)[0]


def _mod_body(c_ref, w_ref, b_ref, o_ref):
    o_ref[...] = _dot(_silu(c_ref[...]), w_ref[...]) + b_ref[...]


def _modulation(c_all, mod_w, mod_b):
    depth, d, n = mod_w.shape
    tn = _tile(n, 512)
    return pl.pallas_call(
        _mod_body, grid=(depth, n // tn),
        in_specs=[pl.BlockSpec((MOD_ROWS, d), lambda l, j: (0, 0)),
                  pl.BlockSpec((None, d, tn), lambda l, j: (l, 0, j)),
                  pl.BlockSpec((None, 1, tn), lambda l, j: (l, 0, j))],
        out_specs=pl.BlockSpec((None, MOD_ROWS, tn), lambda l, j: (l, 0, j)),
        out_shape=jax.ShapeDtypeStruct((depth, MOD_ROWS, n), F32),
        compiler_params=_params(2), name="modulation",
    )(c_all, mod_w, mod_b.reshape(depth, 1, n))


def _pack_bf16_pairs(y):
    n = y.shape[1] // 2
    bits = pltpu.bitcast(y.astype(BF16).astype(F32), jnp.uint32)
    return (bits[:, :n] >> 16) | (bits[:, n:] & jnp.uint32(0xFFFF0000))


def _unpack_bf16_pairs(w):
    return pltpu.bitcast(w << 16, F32), pltpu.bitcast(w & jnp.uint32(0xFFFF0000), F32)


def _norm_body(*refs, modulate, router, n_experts):
    it = iter(refs)
    x_ref, g_ref = next(it), next(it)
    x = x_ref[...]
    y = x * lax.rsqrt(jnp.mean(x * x, axis=-1, keepdims=True) + EPS) * g_ref[...]
    if modulate:
        sh_ref, sc_ref = next(it), next(it)
        y = y * (1.0 + sc_ref[...]) + sh_ref[...]
    if router:
        rw_ref, rb_ref = next(it), next(it)
    o_ref = next(it)
    if o_ref.dtype == jnp.uint32:
        o_ref[...] = _pack_bf16_pairs(y)
    else:
        o_ref[...] = y.astype(o_ref.dtype)
    if router:
        comb_ref = next(it)
        logits = _dot3(y, rw_ref[...]) + rb_ref[...]
        lane = lax.broadcasted_iota(jnp.int32, logits.shape, 1).astype(F32)
        neg = jnp.float32(-jnp.inf)
        logits = jnp.where(lane < n_experts, logits, neg)
        m1 = jnp.max(logits, axis=-1, keepdims=True)
        i1 = jnp.min(jnp.where(logits == m1, lane, float(LANES)), axis=-1, keepdims=True)
        rest = jnp.where(lane == i1, neg, logits)
        m2 = jnp.max(rest, axis=-1, keepdims=True)
        i2 = jnp.min(jnp.where(rest == m2, lane, float(LANES)), axis=-1, keepdims=True)
        e2 = jnp.exp(m2 - m1)
        g1 = 1.0 / (1.0 + e2)
        g2 = e2 / (1.0 + e2)
        comb_ref[...] = (jnp.where(lane == ROUTE_EXPERT, i1, 0.0) + jnp.where(lane == ROUTE_EXPERT + 1, i2, 0.0)
                         + jnp.where(lane == ROUTE_GATE, g1, 0.0) + jnp.where(lane == ROUTE_GATE + 1, g2, 0.0))


def _rmsnorm(name, x, gain, out_dtype, *, rows=None, width=None, col_block=0, mod=None, seg_of_tile=None,
             router=None, tr=256):
    if x.ndim == 3:
        per = x.shape[1]
        rows, width = x.shape[0] * per, x.shape[2]
        tr = _tile(per, tr)
        x_spec = pl.BlockSpec((None, tr, width), lambda i: (i // (per // tr), i % (per // tr), 0))
    else:
        rows = rows or x.shape[0]
        width = width or x.shape[1]
        tr = _tile(rows, tr)
        x_spec = pl.BlockSpec((tr, width), lambda i: (i, col_block))
    in_specs = [x_spec, pl.BlockSpec((1, width), lambda i: (0, 0))]
    args = [x, gain.reshape(1, width)]
    if mod is not None:
        seg = seg_of_tile(tr)
        for t in mod:
            in_specs.append(pl.BlockSpec((None, 1, width), lambda i: (seg(i), 0, 0)))
            args.append(t)
    out_width = width // 2 if out_dtype == jnp.uint32 else width
    out_shape = [jax.ShapeDtypeStruct((rows, out_width), out_dtype)]
    out_specs = [pl.BlockSpec((tr, out_width), lambda i: (i, 0))]
    n_experts = 0
    if router is not None:
        rw, rb = router
        n_experts = rw.shape[1]
        rw_p = jnp.pad(rw, ((0, 0), (0, LANES - n_experts)))
        rb_p = jnp.pad(rb, (0, LANES - n_experts)).reshape(1, LANES)
        in_specs += [pl.BlockSpec((width, LANES), lambda i: (0, 0)), pl.BlockSpec((1, LANES), lambda i: (0, 0))]
        args += [rw_p, rb_p]
        out_shape.append(jax.ShapeDtypeStruct((rows, LANES), F32))
        out_specs.append(pl.BlockSpec((tr, LANES), lambda i: (i, 0)))
    body = functools.partial(_norm_body, modulate=mod is not None, router=router is not None, n_experts=n_experts)
    return pl.pallas_call(body, grid=(rows // tr,), in_specs=in_specs, out_specs=out_specs, out_shape=out_shape,
                          compiler_params=_params(1), name=name)(*args)


def _group_dft_body(p_ref, cs_ref, o_ref, *, groups):
    cs = cs_ref[...]
    fw = groups * HEAD_DIM
    for g in range(groups):
        r = _dot(p_ref[:, g * HEAD_DIM:(g + 1) * HEAD_DIM], cs)
        o_ref[:, g * HEAD_DIM:(g + 1) * HEAD_DIM] = r[:, :HEAD_DIM].astype(o_ref.dtype)
        o_ref[:, fw + g * HEAD_DIM:fw + (g + 1) * HEAD_DIM] = r[:, HEAD_DIM:].astype(o_ref.dtype)


def _group_dft(proj, fw):
    rows = proj.shape[0]
    k = np.arange(HEAD_DIM)
    ang = (2.0 * np.pi / HEAD_DIM) * ((k[:, None] * k[None, :]) % HEAD_DIM)
    cs = jnp.asarray(np.concatenate([np.cos(ang), np.sin(ang)], axis=1), dtype=BF16)
    tm = _tile(rows, 512)
    body = functools.partial(_group_dft_body, groups=fw // HEAD_DIM)
    return pl.pallas_call(
        body, grid=(rows // tm,),
        in_specs=[pl.BlockSpec((tm, fw), lambda i: (i, 0)),
                  pl.BlockSpec((HEAD_DIM, 2 * HEAD_DIM), lambda i: (0, 0))],
        out_specs=pl.BlockSpec((tm, 2 * fw), lambda i: (i, 0)),
        out_shape=jax.ShapeDtypeStruct((rows, 2 * fw), BF16),
        compiler_params=_params(1), name="group_dft",
    )(proj, cs)


def _trig_body(ca_ref, sa_ref, cb_ref, sb_ref, o_ref, *, sin_sign, scale):
    ca, sa, cb, sb = ca_ref[...], sa_ref[...], cb_ref[...], sb_ref[...]
    cos = ca * cb - sa * sb
    sin = (sa * cb + ca * sb) * sin_sign
    o_ref[...] = (jnp.where(pl.program_id(0) == 0, cos, sin) * scale).astype(o_ref.dtype)


def _trig_matrix(name, phase, den, n_rows, n_cols, parts_on_rows, sin_sign=1.0, scale=1.0):
    tile = min(TRIG_TILE, n_rows)
    n_hi = n_rows // tile
    col = jnp.arange(n_cols, dtype=jnp.int32)[None, :]

    def tables(rows):
        ang = (phase(rows[:, None], col) % den).astype(F32) * (2.0 * math.pi / den)
        return jnp.cos(ang), jnp.sin(ang)

    zero = jnp.zeros((1,), jnp.int32)
    base_c, base_s = tables(zero)
    hi_c, hi_s = tables(jnp.arange(n_hi, dtype=jnp.int32) * tile)
    lo_c, lo_s = tables(jnp.arange(tile, dtype=jnp.int32))
    lo_c, lo_s = lo_c * base_c + lo_s * base_s, lo_s * base_c - lo_c * base_s
    if parts_on_rows:
        shape, out_map = (2 * n_rows, n_cols), (lambda p, h: (p * n_hi + h, 0))
    else:
        shape, out_map = (n_rows, 2 * n_cols), (lambda p, h: (h, p))
    body = functools.partial(_trig_body, sin_sign=sin_sign, scale=scale)
    hi_spec = pl.BlockSpec((None, 1, n_cols), lambda p, h: (h, 0, 0))
    lo_spec = pl.BlockSpec((tile, n_cols), lambda p, h: (0, 0))
    return pl.pallas_call(
        body, grid=(2, n_hi), in_specs=[hi_spec, hi_spec, lo_spec, lo_spec],
        out_specs=pl.BlockSpec((tile, n_cols), out_map),
        out_shape=jax.ShapeDtypeStruct(shape, BF16), compiler_params=_params(2), name=name,
    )(hi_c.reshape(n_hi, 1, n_cols), hi_s.reshape(n_hi, 1, n_cols), lo_c, lo_s)


def _position_dft(name, ab, row0, nb, length, fw):
    cs = _trig_matrix(f"{name}_matrix", lambda f, t: f * t, length, length, length, parts_on_rows=False,
                      sin_sign=-1.0, scale=1.0 / math.sqrt(length * HEAD_DIM))
    tm, tn = _tile(length, 1024), _tile(fw, 512)
    nm, nn = length // tm, fw // tn
    rb0 = row0 // length
    return _mm(name, (nb, nm, nn, 2),
               [cs], [pl.BlockSpec((tm, length), lambda b, i, j, k: (i, k))],
               [ab], [pl.BlockSpec((length, tn), lambda b, i, j, k: (rb0 + b, k * nn + j))],
               [], [], [jax.ShapeDtypeStruct((nb * length, fw), BF16)],
               [pl.BlockSpec((tm, tn), lambda b, i, j, k: (b * nm + i, j))],
               [[(0, 0)]], _store0, nk=2, acc_block=(tm, tn))[0]


def _short_conv_body(p_ref, w_ref, b_ref, o_ref):
    u = p_ref[...].astype(F32)
    n = u.shape[0]
    row = lax.broadcasted_iota(jnp.int32, u.shape, 0)
    prev = jnp.where(row == 0, 0.0, pltpu.roll(u, 1, 0))
    nxt = jnp.where(row == n - 1, 0.0, pltpu.roll(u, n - 1, 0))
    w = w_ref[...]
    o_ref[...] = (w[0:1] * prev + w[1:2] * u + w[2:3] * nxt + b_ref[...]).astype(o_ref.dtype)


def _short_conv(name, proj, col0, width, row0, nb, length, conv_w, conv_b):
    tc = _tile(math.gcd(col0, width), 512)
    cb0, rb0 = col0 // tc, row0 // length
    return pl.pallas_call(
        _short_conv_body, grid=(nb, width // tc),
        in_specs=[pl.BlockSpec((length, tc), lambda b, j: (rb0 + b, cb0 + j)),
                  pl.BlockSpec((3, tc), lambda b, j: (0, j)),
                  pl.BlockSpec((1, tc), lambda b, j: (0, j))],
        out_specs=pl.BlockSpec((length, tc), lambda b, j: (b, j)),
        out_shape=jax.ShapeDtypeStruct((nb * length, width), BF16),
        compiler_params=_params(2), name=name,
    )(proj, conv_w, conv_b.reshape(1, width))


def _filter_mlp_body(z_ref, pw_ref, pb_ref, w1_ref, b1_ref, w2_ref, b2_ref, fr_ref, o_ref):
    fr = fr_ref[...]
    h = jnp.sin(fr * (_dot3(z_ref[...], pw_ref[...]) + pb_ref[...]))
    h = jnp.sin(fr * (_dot3(h, w1_ref[...]) + b1_ref[...]))
    o_ref[...] = jnp.sin(fr * (_dot3(h, w2_ref[...]) + b2_ref[...]))


def _filter_out_body(h_ref, wf_ref, wb_ref, d_ref, o_ref):
    h = h_ref[...]
    d = d_ref[...]
    fwd = _dot3(h, wf_ref[...]) * d
    bwd = _dot3(h, wb_ref[...]) * d
    row = lax.broadcasted_iota(jnp.int32, bwd.shape, 0)
    bwd = jnp.where(row == 0, 0.0, bwd)
    o_ref[0] = fwd + bwd
    o_ref[1] = fwd - bwd


def _hyena_filter_sums(length, hw, pe_w, pe_b, w1, b1, w2, b2, w_out, freq):
    t = jnp.linspace(0.0, 1.0, length, dtype=F32)[:, None]
    w = (2.0 * math.pi / length) * jnp.arange(length, dtype=F32)[:, None]
    bands = jnp.linspace(1e-4, HY_POS_BANDS - 1, HY_POS_BANDS, dtype=F32)[None, :]
    z = jnp.concatenate([t, jnp.cos(bands * w), -jnp.sin(bands * w)], axis=-1)
    pos_dim, fwid = pe_w.shape
    z = jnp.pad(z, ((0, 0), (0, LANES - pos_dim)))
    pe_w = jnp.pad(pe_w, ((0, LANES - pos_dim), (0, 0)))
    row = lambda v: v.reshape(1, fwid)
    full = lambda a: pl.BlockSpec(a.shape, lambda i: (0,) * a.ndim)
    args = [z, pe_w, row(pe_b), w1, row(b1), w2, row(b2), row(freq)]
    h = pl.pallas_call(
        _filter_mlp_body, grid=(1,), in_specs=[full(a) for a in args],
        out_specs=pl.BlockSpec((length, fwid), lambda i: (0, 0)),
        out_shape=jax.ShapeDtypeStruct((length, fwid), F32),
        compiler_params=_params(1), name="hyena_filter_mlp",
    )(*args)
    max_decay = math.log(HY_DECAY_TARGET) / HY_FAST_DECAY
    min_decay = math.log(HY_DECAY_TARGET) / HY_SLOW_DECAY
    deltas = jnp.abs(jnp.linspace(min_decay, max_decay, hw, dtype=F32))
    decay = jnp.exp(-t * deltas)
    tc = _tile(hw, 256)
    nc = hw // tc
    return pl.pallas_call(
        _filter_out_body, grid=(HY_ORDER, nc),
        in_specs=[pl.BlockSpec((length, fwid), lambda n, j: (0, 0)),
                  pl.BlockSpec((fwid, tc), lambda n, j: (0, (2 * n) * nc + j)),
                  pl.BlockSpec((fwid, tc), lambda n, j: (0, (2 * n + 1) * nc + j)),
                  pl.BlockSpec((length, tc), lambda n, j: (0, j))],
        out_specs=pl.BlockSpec((2, length, tc), lambda n, j: (0, 0, n * nc + j)),
        out_shape=jax.ShapeDtypeStruct((2, length, HY_ORDER * hw), F32),
        compiler_params=_params(2), name="hyena_filter_out",
    )(h, w_out, w_out, decay)


def _hyena_spectrum_epilogue(parts, es, outs, pids):
    ur, us = parts
    kr, ks = es[0][0], es[0][1]
    outs[0][0] = (ur * kr - us * ks).astype(outs[0].dtype)
    outs[0][1] = (ur * ks + us * kr).astype(outs[0].dtype)


def _hyena_gate_epilogue(parts, es, outs, pids):
    gate_ref, z_ref, skip_ref = es
    gate, z = gate_ref[...].astype(F32), z_ref[...].astype(F32)
    outs[0][...] = (gate * (parts[0] + z * skip_ref[...])).astype(outs[0].dtype)


def _hyena_stream(tag, uc, nb, length, hw, filt, skip):
    fo_b = _trig_matrix(f"hyena_dft_{tag}", lambda f, t: (2 * f + 1) * t, 4 * length, length, length,
                        parts_on_rows=True)
    go_b = _trig_matrix(f"hyena_idft_{tag}", lambda t, f: (2 * f + 1) * t, 4 * length, length, length,
                        parts_on_rows=False, scale=1.0 / length)
    tm, tn = _tile(length, 1024), _tile(hw, 512)
    nm, nn = length // tm, hw // tn
    kspec = _mm(f"hyena_kspec_{tag}", (2 * nm, HY_ORDER * nn),
                [fo_b], [pl.BlockSpec((tm, length), lambda i, j: (i, 0))],
                [filt.reshape(2 * length, HY_ORDER * hw)],
                [pl.BlockSpec((length, tn), lambda i, j: (i // nm, j))],
                [], [], [jax.ShapeDtypeStruct((2 * length, HY_ORDER * hw), F32)],
                [pl.BlockSpec((tm, tn), lambda i, j: (i, j))], [[(0, 0)]], _store0)[0]
    kspec = kspec.reshape(2, length, HY_ORDER * hw)
    z = uc
    for n in range(HY_ORDER):
        z = _hyena_order(tag, n, uc, z, nb, length, hw, fo_b, go_b, kspec, skip, (tm, tn, nm, nn))
    return z


def _hyena_order(tag, n, uc, z, nb, length, hw, fo_b, go_b, kspec, skip, tiles):
    tm, tn, nm, nn = tiles
    gate_cb = (n + 1) * nn
    spec = _mm(f"hyena_fwd{n}_{tag}", (nm, nb, nn),
               [fo_b, fo_b], [pl.BlockSpec((tm, length), lambda i, b, j: (i, 0)),
                              pl.BlockSpec((tm, length), lambda i, b, j: (nm + i, 0))],
               [z], [pl.BlockSpec((length, tn), lambda i, b, j: (b, j))],
               [kspec], [pl.BlockSpec((2, tm, tn), lambda i, b, j: (0, i, n * nn + j))],
               [jax.ShapeDtypeStruct((nb, 2, length, hw), BF16)],
               [pl.BlockSpec((None, 2, tm, tn), lambda i, b, j: (b, 0, i, j))],
               [[(0, 0)], [(1, 0)]], _hyena_spectrum_epilogue)[0]
    return _mm(f"hyena_inv{n}_{tag}", (nb, nm, nn),
               [go_b], [pl.BlockSpec((tm, 2 * length), lambda b, i, j: (i, 0))],
               [spec.reshape(nb * 2 * length, hw)],
               [pl.BlockSpec((2 * length, tn), lambda b, i, j: (b, j))],
               [uc, z, skip.reshape(HY_ORDER, 1, hw)],
               [pl.BlockSpec((tm, tn), lambda b, i, j: (b * nm + i, gate_cb + j)),
                pl.BlockSpec((tm, tn), lambda b, i, j: (b * nm + i, j)),
                pl.BlockSpec((None, 1, tn), lambda b, i, j: (n, 0, j))],
               [jax.ShapeDtypeStruct((nb * length, hw), BF16)],
               [pl.BlockSpec((tm, tn), lambda b, i, j: (b * nm + i, j))],
               [[(0, 0)]], _hyena_gate_epilogue)[0]


def _resgate_epilogue(parts, es, outs, pids):
    res_ref, gate_ref = es[0], es[1]
    outs[0][...] = res_ref[...] + gate_ref[...] * parts[0]


def _mixer_out(name, res, za, zb, w_out, gate, seg_of_tile, in_place):
    rows, half = za.shape
    d = res.shape[-1]
    tm, tn = _tile(rows, 1024), _tile(d, 512)
    if res.ndim == 3:
        assert not in_place
        tm = _tile(res.shape[1], 1024)
        per = res.shape[1] // tm
        res_spec = pl.BlockSpec((None, tm, tn), lambda i, j: (i // per, i % per, j))
    else:
        res_spec = pl.BlockSpec((tm, tn), lambda i, j: (i, j))
    seg = seg_of_tile(tm)
    return _mm(name, (rows // tm, d // tn),
               [za, zb], [pl.BlockSpec((tm, half), lambda i, j: (i, 0))] * 2,
               [w_out, w_out], [pl.BlockSpec((half, tn), lambda i, j: (0, j)),
                                pl.BlockSpec((half, tn), lambda i, j: (1, j))],
               [res, gate], [res_spec, pl.BlockSpec((None, 1, tn), lambda i, j: (seg(i), 0, j))],
               [jax.ShapeDtypeStruct((rows, d), F32)],
               [pl.BlockSpec((tm, tn), lambda i, j: (i, j))],
               [[(0, 0), (1, 1)]], _resgate_epilogue, aliases={4: 0} if in_place else {})[0]


def _swiglu_epilogue(parts, es, outs, pids):
    outs[0][...] = (_silu(parts[0]) * parts[1]).astype(outs[0].dtype)


def _ffn_up(name, h, w1, w3):
    m, k = h.shape
    n = w1.shape[1]
    tm, tn = _tile(m, 1024), _tile(n, 256)
    return _mm(name, (m // tm, n // tn),
               [h], [pl.BlockSpec((tm, k), lambda i, j: (i, 0))],
               [w1, w3], [pl.BlockSpec((k, tn), lambda i, j: (0, j))] * 2,
               [], [], [jax.ShapeDtypeStruct((m, n), BF16)],
               [pl.BlockSpec((tm, tn), lambda i, j: (i, j))],
               [[(0, 0)], [(0, 1)]], _swiglu_epilogue)[0]


def _down_residual(name, x_res, rows, u, w2, gate, seg_of_tile, tm_pref, tk_pref, tn_pref):
    d = x_res.shape[1]
    kdim = u.shape[1]
    tm, tn, tk = _tile(rows, tm_pref), _tile(d, tn_pref), _tile(kdim, tk_pref)
    nk = kdim // tk
    seg = seg_of_tile(tm)
    return _mm(name, (rows // tm, d // tn, nk),
               [u], [pl.BlockSpec((tm, tk), lambda i, j, k: (i, k))],
               [w2], [pl.BlockSpec((tk, tn), lambda i, j, k: (k, j))],
               [x_res, gate], [pl.BlockSpec((tm, tn), lambda i, j, k: (i, j)),
                               pl.BlockSpec((None, 1, tn), lambda i, j, k: (seg(i), 0, j))],
               [jax.ShapeDtypeStruct(x_res.shape, F32)],
               [pl.BlockSpec((tm, tn), lambda i, j, k: (i, j))],
               [[(0, 0)]], _resgate_epilogue, nk=nk, acc_block=(tm, tn), aliases={2: 0})[0]


def _moe_plan(route, n_experts):
    t = route.shape[0]
    n_tiles = TOP_K * t // MOE_TILE + n_experts
    expert = route[:, ROUTE_EXPERT:ROUTE_EXPERT + TOP_K].astype(jnp.int32).T.reshape(-1)
    onehot = (expert[:, None] == jnp.arange(n_experts, dtype=jnp.int32)[None, :]).astype(jnp.int32)
    running = jnp.cumsum(onehot, axis=0)
    rank = jnp.take_along_axis(running, expert[:, None], axis=1)[:, 0] - 1
    tiles_per_expert = (running[-1] + MOE_TILE - 1) // MOE_TILE
    tile_end = jnp.cumsum(tiles_per_expert)
    row = (tile_end - tiles_per_expert)[expert] * MOE_TILE + rank
    token = jnp.tile(jnp.arange(t, dtype=jnp.int32), TOP_K)
    row_token = jnp.zeros((n_tiles * MOE_TILE,), jnp.int32).at[row].set(token)
    tile_expert = jnp.searchsorted(tile_end, jnp.arange(n_tiles, dtype=jnp.int32), side="right")
    tile_expert = jnp.minimum(tile_expert, n_experts - 1).astype(jnp.int32)
    pos = row.reshape(TOP_K, t // MOE_COMBINE_TILE, MOE_COMBINE_TILE).transpose(1, 0, 2)
    return row_token.reshape(n_tiles, 1, MOE_TILE), tile_expert, tile_end[-1:].astype(jnp.int32), pos


def _row_copy(src_hbm, row, dst_ref, dst_row, sem):
    return pltpu.make_async_copy(src_hbm.at[pl.ds(row, 1)], dst_ref.at[pl.ds(dst_row, 1)], sem)


def _wait_rows(src_hbm, buf_ref, sem):
    pltpu.make_async_copy(src_hbm.at[pl.ds(0, buf_ref.shape[0])], buf_ref, sem).wait()


def _moe_gather_body(na_ref, tok_ref, src_hbm, o_ref, buf_ref, sem):
    rows = buf_ref.shape[0]
    active = pl.program_id(0) < na_ref[0]

    @pl.when(active)
    def _():
        def issue(r, carry):
            _row_copy(src_hbm, tok_ref[0, r], buf_ref, r, sem).start()
            return carry

        lax.fori_loop(0, rows, issue, 0, unroll=DMA_ISSUE_UNROLL)
        _wait_rows(src_hbm, buf_ref, sem)
        n = buf_ref.shape[1]
        lo, hi = _unpack_bf16_pairs(buf_ref[...])
        o_ref[:, :n] = lo.astype(o_ref.dtype)
        o_ref[:, n:] = hi.astype(o_ref.dtype)

    @pl.when(jnp.logical_not(active))
    def _():
        o_ref[...] = jnp.zeros(o_ref.shape, o_ref.dtype)


def _dma_params(n_grid):
    return pltpu.CompilerParams(dimension_semantics=("arbitrary",) * n_grid, vmem_limit_bytes=V7X_VMEM_LIMIT,
                                disable_bounds_checks=True)


def _moe_gather(h, row_token, n_active):
    n_tiles = row_token.shape[0]
    d = 2 * h.shape[1]
    return pl.pallas_call(
        _moe_gather_body, grid=(n_tiles,),
        in_specs=[pl.BlockSpec(memory_space=pltpu.SMEM),
                  pl.BlockSpec((None, 1, MOE_TILE), lambda i: (i, 0, 0), memory_space=pltpu.SMEM),
                  pl.BlockSpec(memory_space=pl.ANY)],
        out_specs=pl.BlockSpec((MOE_TILE, d), lambda i: (i, 0)),
        out_shape=jax.ShapeDtypeStruct((n_tiles * MOE_TILE, d), BF16),
        scratch_shapes=[pltpu.VMEM((MOE_TILE,) + h.shape[1:], h.dtype), pltpu.SemaphoreType.DMA(())],
        compiler_params=_dma_params(1), name="moe_gather",
    )(n_active, row_token, h)


def _moe_up_body(te_ref, na_ref, x_ref, w1_ref, w3_ref, o_ref):
    active = pl.program_id(1) < na_ref[0]

    @pl.when(active)
    def _():
        x = x_ref[...]
        o_ref[...] = (_silu(_dot(x, w1_ref[...])) * _dot(x, w3_ref[...])).astype(o_ref.dtype)

    @pl.when(jnp.logical_not(active))
    def _():
        o_ref[...] = jnp.zeros(o_ref.shape, o_ref.dtype)


def _moe_down_body(te_ref, na_ref, u_ref, w2_ref, o_ref):
    active = pl.program_id(1) < na_ref[0]

    @pl.when(active)
    def _():
        o_ref[...] = _pack_bf16_pairs(_dot(u_ref[...], w2_ref[...]))

    @pl.when(jnp.logical_not(active))
    def _():
        o_ref[...] = jnp.zeros(o_ref.shape, o_ref.dtype)


def _moe_experts(xs, tile_expert, n_active, w1, w3, w2):
    ne, d, f = w1.shape
    n_tiles = xs.shape[0] // MOE_TILE
    tn = _tile(f, 512)
    up = pl.pallas_call(
        _moe_up_body,
        grid_spec=pltpu.PrefetchScalarGridSpec(
            num_scalar_prefetch=2, grid=(f // tn, n_tiles),
            in_specs=[pl.BlockSpec((MOE_TILE, d), lambda j, i, te, na: (i, 0)),
                      pl.BlockSpec((None, d, tn), lambda j, i, te, na: (te[i], 0, j)),
                      pl.BlockSpec((None, d, tn), lambda j, i, te, na: (te[i], 0, j))],
            out_specs=pl.BlockSpec((MOE_TILE, tn), lambda j, i, te, na: (i, j))),
        out_shape=jax.ShapeDtypeStruct((xs.shape[0], f), BF16),
        compiler_params=_params(2), name="moe_up",
    )(tile_expert, n_active, xs, w1, w3)
    tn = _tile(d, 1024)
    return pl.pallas_call(
        _moe_down_body,
        grid_spec=pltpu.PrefetchScalarGridSpec(
            num_scalar_prefetch=2, grid=(d // tn, n_tiles),
            in_specs=[pl.BlockSpec((MOE_TILE, f), lambda j, i, te, na: (i, 0)),
                      pl.BlockSpec((None, f, tn), lambda j, i, te, na: (te[i], 0, j))],
            out_specs=pl.BlockSpec((MOE_TILE, tn // 2), lambda j, i, te, na: (i, j))),
        out_shape=jax.ShapeDtypeStruct((xs.shape[0], d // 2), jnp.uint32),
        compiler_params=_params(2), name="moe_down",
    )(tile_expert, n_active, up, w2), tn


def _moe_combine_body(pos_ref, ys_hbm, res_ref, route_ref, gate_ref, o_ref, buf_ref, sem, *, pack_tile):
    rows = res_ref.shape[0]

    def issue(r, carry):
        for k in range(TOP_K):
            _row_copy(ys_hbm, pos_ref[k, r], buf_ref.at[k], r, sem).start()
        return carry

    lax.fori_loop(0, rows, issue, 0, unroll=DMA_ISSUE_UNROLL)
    for k in range(TOP_K):
        _wait_rows(ys_hbm, buf_ref.at[k], sem)
    route = route_ref[...]
    lane = lax.broadcasted_iota(jnp.int32, route.shape, 1)
    gates = [jnp.sum(jnp.where(lane == ROUTE_GATE + k, route, 0.0), axis=-1, keepdims=True) for k in range(TOP_K)]
    half = pack_tile // 2
    for j in range(o_ref.shape[1] // pack_tile):
        halves = [_unpack_bf16_pairs(buf_ref[k, :, j * half:(j + 1) * half]) for k in range(TOP_K)]
        for part in range(2):
            cols = slice(j * pack_tile + part * half, j * pack_tile + (part + 1) * half)
            moe = None
            for k in range(TOP_K):
                term = gates[k] * halves[k][part]
                moe = term if moe is None else moe + term
            o_ref[:, cols] = res_ref[:, cols] + gate_ref[:, cols] * moe


def _moe_combine(x_res, rows, ys, pack_tile, pos, route, gate, seg_of_tile):
    d = x_res.shape[1]
    tm = MOE_COMBINE_TILE
    seg = seg_of_tile(tm)
    return pl.pallas_call(
        functools.partial(_moe_combine_body, pack_tile=pack_tile), grid=(rows // tm,),
        in_specs=[pl.BlockSpec((None, TOP_K, tm), lambda i: (i, 0, 0), memory_space=pltpu.SMEM),
                  pl.BlockSpec(memory_space=pl.ANY),
                  pl.BlockSpec((tm, d), lambda i: (i, 0)),
                  pl.BlockSpec((tm, LANES), lambda i: (i, 0)),
                  pl.BlockSpec((None, 1, d), lambda i: (seg(i), 0, 0))],
        out_specs=pl.BlockSpec((tm, d), lambda i: (i, 0)),
        out_shape=jax.ShapeDtypeStruct(x_res.shape, F32),
        scratch_shapes=[pltpu.VMEM((TOP_K, tm) + ys.shape[1:], ys.dtype), pltpu.SemaphoreType.DMA(())],
        input_output_aliases={2: 0},
        compiler_params=_dma_params(1), name="moe_combine",
    )(pos, ys, x_res, route, gate)


NA_BLOCK_ROWS = 4


def _na_body(q_ref, k_ref, v_ref, kc_ref, vc_ref, band_ref, o_ref, va_ref, bias_ref, *, seq, ctx_len, tables,
             blocks, win):
    qscale = HEAD_DIM ** -0.5 * math.log2(math.e)
    bq = NA_BLOCK_ROWS * GRID_W

    @pl.when(pl.program_id(1) == 0)
    def _():
        for case, table in enumerate(tables):
            for ri, row in enumerate(table):
                for rj, entry in enumerate(row):
                    bias_ref[case, ri * GRID_W:(ri + 1) * GRID_W, rj * GRID_W:(rj + 1) * GRID_W] = band_ref[int(entry)]

    va_ref[0:seq, 0:HEAD_DIM] = v_ref[...].astype(BF16)
    va_ref[seq:seq + ctx_len, 0:HEAD_DIM] = vc_ref[...].astype(BF16)
    lane = lax.broadcasted_iota(jnp.int32, (seq + ctx_len, va_ref.shape[1] - HEAD_DIM), 1)
    va_ref[:, HEAD_DIM:] = jnp.where(lane == 0, 1.0, 0.0).astype(BF16)
    kc = kc_ref[...].astype(BF16)
    for kb, (case, ws) in enumerate(blocks):
        q = (q_ref[kb * bq:(kb + 1) * bq, :] * qscale).astype(BF16)
        k0 = ws * GRID_W
        s_loc = _dot_nt(q, k_ref[k0:k0 + win, :]) + bias_ref[case]
        s_ctx = _dot_nt(q, kc)
        m = jnp.maximum(jnp.max(s_loc, axis=-1, keepdims=True), jnp.max(s_ctx, axis=-1, keepdims=True))
        acc = _dot(jnp.exp2(s_loc - m), va_ref[k0:k0 + win, :]) + _dot(jnp.exp2(s_ctx - m), va_ref[seq:seq + ctx_len, :])
        o_ref[kb * bq:(kb + 1) * bq, :] = (acc[:, :HEAD_DIM] / acc[:, HEAD_DIM:HEAD_DIM + 1]).astype(o_ref.dtype)


def _na_bias_slabs(rpb, grid_rows, win_rows):
    heads = rpb.shape[0]
    cols = np.arange(GRID_W)
    start = np.clip(cols - NA_WIN_C // 2, 0, GRID_W - NA_WIN_C)
    kc = cols[None, :]
    inside = (kc >= start[:, None]) & (kc < start[:, None] + NA_WIN_C)
    dc = np.clip(kc - cols[:, None] + (NA_WIN_C - 1), 0, 2 * NA_WIN_C - 2)
    band = jnp.where(jnp.asarray(inside)[None, None], rpb[:, :, dc], NEG_BIG)
    masked = band.shape[1]
    band = jnp.concatenate([band, jnp.full((heads, 1, GRID_W, GRID_W), NEG_BIG, F32)], axis=1)
    win = min(win_rows + NA_BLOCK_ROWS, grid_rows)
    tables, blocks = [], []
    for r0 in range(0, grid_rows, NA_BLOCK_ROWS):
        ws = int(np.clip(r0 - win_rows // 2, 0, grid_rows - win))
        idx = np.full((NA_BLOCK_ROWS, win), masked)
        for ri in range(NA_BLOCK_ROWS):
            r = r0 + ri
            rs = int(np.clip(r - win_rows // 2, 0, grid_rows - win_rows))
            assert ws <= rs and rs + win_rows <= ws + win
            for key_row in range(rs, rs + win_rows):
                idx[ri, key_row - ws] = key_row - r + (NA_WIN_R - 1)
        for case, t in enumerate(tables):
            if np.array_equal(t, idx):
                break
        else:
            case = len(tables)
            tables.append(idx)
        blocks.append((case, ws))
    return band * math.log2(math.e), tuple(tuple(map(tuple, t)) for t in tables), tuple(blocks), win * GRID_W


def _neighbourhood_attention(proj, proj_ctx, nb, seq, ctx_len, heads, rpb):
    grid_rows = seq // GRID_W
    assert grid_rows % NA_BLOCK_ROWS == 0
    win_rows = min(NA_WIN_R, grid_rows)
    band, tables, blocks, win = _na_bias_slabs(rpb, grid_rows, win_rows)
    body = functools.partial(_na_body, seq=seq, ctx_len=ctx_len, tables=tables, blocks=blocks, win=win)
    return pl.pallas_call(
        body, grid=(heads, nb),
        in_specs=[pl.BlockSpec((seq, HEAD_DIM), lambda h, b: (b, h)),
                  pl.BlockSpec((seq, HEAD_DIM), lambda h, b: (b, heads + h)),
                  pl.BlockSpec((seq, HEAD_DIM), lambda h, b: (b, 2 * heads + h)),
                  pl.BlockSpec((ctx_len, HEAD_DIM), lambda h, b: (b, heads + h)),
                  pl.BlockSpec((ctx_len, HEAD_DIM), lambda h, b: (b, 2 * heads + h)),
                  pl.BlockSpec((None,) + band.shape[1:], lambda h, b: (h, 0, 0, 0))],
        out_specs=pl.BlockSpec((seq, HEAD_DIM), lambda h, b: (b, h)),
        out_shape=jax.ShapeDtypeStruct((nb * seq, heads * HEAD_DIM), BF16),
        scratch_shapes=[pltpu.VMEM((seq + ctx_len, HEAD_DIM + LANES), BF16),
                        pltpu.VMEM((len(tables), NA_BLOCK_ROWS * GRID_W, win), F32)],
        compiler_params=_params(2), name="neighbourhood_attention",
    )(proj, proj, proj, proj_ctx, proj_ctx, band)


def _mla_body(q_ref, kvl_ref, kvc_ref, krl_ref, krc_ref, cq_ref, sq_ref, ck_ref, sk_ref, o_ref,
              qf_ref, kf_ref, va_ref, *, nope, vdim, seq, ctx_len, chunks):
    qscale = (nope + MLA_ROPE) ** -0.5 * math.log2(math.e)
    rope_end = nope + MLA_ROPE

    @pl.when(pl.program_id(2) == 0)
    def _():
        krl = krl_ref[...]
        kf_ref[0:seq, 0:nope] = kvl_ref[:, 0:nope].astype(BF16)
        kf_ref[0:seq, nope:rope_end] = (krl[:, :MLA_ROPE] * ck_ref[...] + krl[:, MLA_ROPE:] * sk_ref[...]).astype(BF16)
        kf_ref[seq:seq + ctx_len, 0:nope] = kvc_ref[:, 0:nope].astype(BF16)
        kf_ref[seq:seq + ctx_len, nope:rope_end] = krc_ref[:, 0:MLA_ROPE].astype(BF16)
        kf_ref[:, rope_end:] = jnp.zeros((seq + ctx_len, kf_ref.shape[1] - rope_end), BF16)
        va_ref[0:seq, 0:vdim] = kvl_ref[:, nope:nope + vdim].astype(BF16)
        va_ref[seq:seq + ctx_len, 0:vdim] = kvc_ref[:, nope:nope + vdim].astype(BF16)
        lane = lax.broadcasted_iota(jnp.int32, (seq + ctx_len, va_ref.shape[1] - vdim), 1)
        va_ref[:, vdim:] = jnp.where(lane == 0, 1.0, 0.0).astype(BF16)

    q = q_ref[...]
    qf_ref[:, 0:nope] = (q[:, :nope] * qscale).astype(BF16)
    qr = q[:, nope:rope_end] * cq_ref[...] + q[:, rope_end:] * sq_ref[...]
    qf_ref[:, nope:rope_end] = (qr * qscale).astype(BF16)
    qf_ref[:, rope_end:] = jnp.zeros((q.shape[0], qf_ref.shape[1] - rope_end), BF16)
    qf = qf_ref[...]
    m = jnp.full((q.shape[0], 1), -jnp.inf, F32)
    acc = jnp.zeros((q.shape[0], va_ref.shape[1]), F32)
    for start, size in chunks:
        s = _dot_nt(qf, kf_ref[start:start + size, :])
        m_new = jnp.maximum(m, jnp.max(s, axis=-1, keepdims=True))
        p = jnp.exp2(s - m_new)
        acc = acc * jnp.exp2(m - m_new) + _dot(p, va_ref[start:start + size, :])
        m = m_new
    o_ref[...] = (acc[:, :vdim] / acc[:, vdim:vdim + 1]).astype(o_ref.dtype)


def _rope_tables(seq):
    pos = np.arange(seq)
    row = (pos // GRID_W).astype(np.float32)
    col = (pos % GRID_W).astype(np.float32)
    half = MLA_ROPE // 2
    inv = jnp.asarray(ROPE_BASE, F32) ** (-jnp.arange(0, half, 2, dtype=F32) / half)
    ang_r = jnp.asarray(row)[:, None] * inv
    ang_c = jnp.asarray(col)[:, None] * inv
    ang = jnp.concatenate([ang_r, ang_r, ang_c, ang_c], axis=-1)
    return jnp.cos(ang), jnp.sin(ang)


def _rotate_partner_columns(w):
    q = MLA_ROPE // 4
    parts = [w[:, i * q:(i + 1) * q] for i in range(4)]
    return jnp.concatenate([-parts[1], parts[0], -parts[3], parts[2]], axis=1)


def _latent_attention(q, kv, kv_ctx, kr, kr_ctx, nb, seq, ctx_len, heads, nope, vdim):
    cos, sin = _rope_tables(seq)
    tq = _tile(seq, 1024)
    nq = seq // tq
    qw = nope + 2 * MLA_ROPE
    keys = seq + ctx_len
    n_chunks = -(-keys // 1024)
    while keys % (n_chunks * LANES):
        n_chunks += 1
    chunk = keys // n_chunks
    chunks = tuple((s, chunk) for s in range(0, keys, chunk))
    kw = nope + 2 * MLA_ROPE
    vw = vdim + LANES
    body = functools.partial(_mla_body, nope=nope, vdim=vdim, seq=seq, ctx_len=ctx_len, chunks=chunks)
    return pl.pallas_call(
        body, grid=(nb, heads, nq),
        scratch_shapes=[pltpu.VMEM((tq, kw), BF16), pltpu.VMEM((keys, kw), BF16), pltpu.VMEM((keys, vw), BF16)],
        in_specs=[pl.BlockSpec((tq, qw), lambda b, h, i: (b * nq + i, h)),
                  pl.BlockSpec((seq, nope + vdim), lambda b, h, i: (b, h)),
                  pl.BlockSpec((ctx_len, nope + vdim), lambda b, h, i: (b, h)),
                  pl.BlockSpec((seq, 2 * MLA_ROPE), lambda b, h, i: (b, 0)),
                  pl.BlockSpec((ctx_len, 2 * MLA_ROPE), lambda b, h, i: (b, 0)),
                  pl.BlockSpec((tq, MLA_ROPE), lambda b, h, i: (i, 0)),
                  pl.BlockSpec((tq, MLA_ROPE), lambda b, h, i: (i, 0)),
                  pl.BlockSpec((seq, MLA_ROPE), lambda b, h, i: (0, 0)),
                  pl.BlockSpec((seq, MLA_ROPE), lambda b, h, i: (0, 0))],
        out_specs=pl.BlockSpec((tq, vdim), lambda b, h, i: (b * nq + i, h)),
        out_shape=jax.ShapeDtypeStruct((nb * seq, heads * vdim), BF16),
        compiler_params=_params(3), name="latent_attention",
    )(q, kv, kv_ctx, kr, kr_ctx, cos, sin, cos, sin)


def kernel(x, c, ctx, c_ctx, mod_w, mod_b, norm_mix_g, norm_ffn_g, ev_w_in, ev_w_out, hy_conv_w, hy_conv_b, hy_pe_w, hy_pe_b, hy_w1, hy_b1, hy_w2, hy_b2, hy_w_out, hy_freq, hy_skip, ffn_w1, ffn_w3, ffn_w2, od_w_in, od_w_out, na_rpb, mla_q_g, mla_w_uq, mla_kv_g, mla_w_ukv, moe_router_w, moe_router_b, moe_w1, moe_w3, moe_w2, final_g):
    nb, seq, d = x.shape
    ctx_len = ctx.shape[1]
    depth = mod_w.shape[0]
    assert depth == 2 and ev_w_in.shape[0] == 1 and od_w_in.shape[0] == 1, "even layer then odd (last) layer"
    lat_rows, ctx_rows = nb * seq, nb * ctx_len
    assert nb + 1 <= MOD_ROWS

    def lat_seg(tile_rows):
        assert seq % tile_rows == 0
        return lambda i: (i * tile_rows) // seq

    def ctx_seg(tile_rows):
        return lambda i: nb

    c_all = jnp.concatenate([c, c_ctx[None, :], jnp.zeros((MOD_ROWS - nb - 1, d), F32)], axis=0)
    mod = _modulation(c_all, mod_w, mod_b).reshape(depth, MOD_ROWS, N_MOD, 1, d)
    mods = [[mod[i, :, m] for m in range(N_MOD)] for i in range(depth)]

    sh1, sc1, g1, sh2, sc2, g2 = mods[0]
    fw = (ev_w_out.shape[1]) // 2
    hw = ev_w_out.shape[1] - fw
    filt_args = (hy_pe_w[0], hy_pe_b[0], hy_w1[0], hy_b1[0], hy_w2[0], hy_b2[0], hy_w_out[0], hy_freq[0])
    w2_b = ffn_w2[0].astype(BF16)
    streams = {}
    for tag, src, length, seg_fn in (("lat", x, seq, lat_seg), ("ctx", ctx.reshape(ctx_rows, d), ctx_len, ctx_seg)):
        h = _rmsnorm(f"norm_mix0_{tag}", src, norm_mix_g[0], BF16, mod=(sh1, sc1), seg_of_tile=seg_fn)[0]
        proj = _linear(f"even_in_{tag}", h, ev_w_in[0], BF16)
        ab = _group_dft(proj, fw)
        zf = _position_dft(f"fourier_{tag}", ab, 0, nb, length, fw)
        uc = _short_conv(f"short_conv_{tag}", proj, fw, (HY_ORDER + 1) * hw, 0, nb, length,
                         hy_conv_w[0], hy_conv_b[0])
        filt = _hyena_filter_sums(length, hw, *filt_args)
        zh = _hyena_stream(tag, uc, nb, length, hw, filt, hy_skip[0])
        xs_ = _mixer_out(f"even_out_{tag}", src, zf, zh, ev_w_out[0], g1, seg_fn, in_place=False)
        h2 = _rmsnorm(f"norm_ffn0_{tag}", xs_, norm_ffn_g[0], BF16, mod=(sh2, sc2), seg_of_tile=seg_fn)[0]
        u = _ffn_up(f"ffn_up_{tag}", h2, ffn_w1[0], ffn_w3[0])
        streams[tag] = _down_residual(f"ffn_down_{tag}", xs_, xs_.shape[0], u, w2_b, g2, seg_fn,
                                      tm_pref=512, tk_pref=u.shape[1], tn_pref=512)
    xl, xc = streams["lat"], streams["ctx"]

    sh1, sc1, g1, sh2, sc2, g2 = mods[1]
    w_in = od_w_in[0]
    na_w = od_w_out.shape[1] // 2
    heads = na_w // HEAD_DIM
    q_rank = mla_w_uq.shape[1]
    kv_rank = mla_w_ukv.shape[1]
    mla_heads = (od_w_out.shape[1] - na_w) // HEAD_DIM
    nope = mla_w_uq.shape[2] // mla_heads - MLA_ROPE
    vdim = mla_w_ukv.shape[2] // mla_heads - nope
    main_cols = 3 * na_w + q_rank + kv_rank
    w_kr = w_in[:, main_cols:main_cols + MLA_ROPE]
    w_kr = jnp.concatenate([w_kr, _rotate_partner_columns(w_kr)], axis=1)
    kv_col_block = (3 * na_w + q_rank) // kv_rank
    proj, kr, kv = {}, {}, {}
    for tag, xs_, seg_fn in (("lat", xl, lat_seg), ("ctx", xc, ctx_seg)):
        h = _rmsnorm(f"norm_mix1_{tag}", xs_, norm_mix_g[1], BF16, mod=(sh1, sc1), seg_of_tile=seg_fn)[0]
        proj[tag] = _linear(f"odd_in_{tag}", h, w_in, F32, n_cols=main_cols)
        kr[tag] = _linear(f"odd_in_rope_{tag}", h, w_kr, F32)
        ckv = _rmsnorm(f"norm_kv_{tag}", proj[tag], mla_kv_g[0], BF16, width=kv_rank, col_block=kv_col_block)[0]
        kv[tag] = _linear(f"mla_kv_{tag}", ckv, mla_w_ukv[0], BF16, tn_pref=2048)
    cq = _rmsnorm("norm_q", proj["lat"], mla_q_g[0], BF16, width=q_rank, col_block=3 * na_w // q_rank)[0]
    w_uq = mla_w_uq[0].reshape(q_rank, mla_heads, nope + MLA_ROPE)
    w_uq = jnp.concatenate([w_uq, jnp.stack([_rotate_partner_columns(w_uq[:, hh, nope:])
                                             for hh in range(mla_heads)], axis=1)], axis=-1)
    q = _linear("mla_q", cq, w_uq.reshape(q_rank, mla_heads * (nope + 2 * MLA_ROPE)), F32, tn_pref=2048)
    z_na = _neighbourhood_attention(proj["lat"], proj["ctx"], nb, seq, ctx_len, heads, na_rpb[0])
    z_mla = _latent_attention(q, kv["lat"], kv["ctx"], kr["lat"], kr["ctx"], nb, seq, ctx_len, mla_heads, nope, vdim)
    xl = _mixer_out("odd_out", xl, z_na, z_mla, od_w_out[0], g1, lat_seg, in_place=True)
    h2, route = _rmsnorm("norm_ffn1", xl, norm_ffn_g[1], jnp.uint32, mod=(sh2, sc2), seg_of_tile=lat_seg,
                         router=(moe_router_w[0], moe_router_b[0]))
    assert lat_rows % MOE_TILE == 0 and lat_rows % MOE_COMBINE_TILE == 0
    row_token, tile_expert, n_active, pos = _moe_plan(route, moe_w1.shape[1])
    xs = _moe_gather(h2, row_token, n_active)
    ys, pack_tile = _moe_experts(xs, tile_expert, n_active, moe_w1[0], moe_w3[0], moe_w2[0])
    xl = _moe_combine(xl, lat_rows, ys, pack_tile, pos, route, g2, lat_seg)
    out = _rmsnorm("norm_final", xl, final_g, F32)[0]
    return out.reshape(nb, seq, d)
```

```python
import functools
import math

import jax
import jax.numpy as jnp
import numpy as np
from jax import lax
from jax.experimental import pallas as pl
from jax.experimental.pallas import tpu as pltpu

F32 = jnp.float32
BF16 = jnp.bfloat16

GRID_W = 64
HEAD_DIM = 128
N_MOD = 6
HY_ORDER = 2
HY_POS_BANDS = 16
HY_DECAY_TARGET = 1e-2
HY_FAST_DECAY = 0.3
HY_SLOW_DECAY = 1.5
NA_WIN_R = 8
NA_WIN_C = 16
MLA_ROPE = 64
ROPE_BASE = 10000.0
TOP_K = 2
EPS = 1e-6

LANES = 128
MOD_ROWS = 8
V7X_VMEM_LIMIT = 58 * 2 ** 20
NEG_BIG = -1e30
ROUTE_EXPERT = 0
ROUTE_GATE = TOP_K
MOE_TILE = 512
MOE_COMBINE_TILE = 256
DMA_ISSUE_UNROLL = 8
TRIG_TILE = 128


def _tile(n, pref):
    if n <= pref:
        return n
    t = (pref // LANES) * LANES
    while t >= LANES:
        if n % t == 0:
            return t
        t -= LANES
    raise ValueError(f"no lane-aligned tile for {n} <= {pref}")


def _params(n_grid):
    return pltpu.CompilerParams(dimension_semantics=("arbitrary",) * n_grid,
                                vmem_limit_bytes=V7X_VMEM_LIMIT)


def _bf(a):
    return a if a.dtype == BF16 else a.astype(BF16)


def _dot(a, b):
    return jnp.dot(_bf(a), _bf(b), preferred_element_type=F32)


def _dot_nt(a, b):
    return lax.dot_general(_bf(a), _bf(b), (((1,), (1,)), ((), ())), preferred_element_type=F32)


def _split(a):
    hi = a.astype(BF16)
    lo = (a - hi.astype(F32)).astype(BF16)
    return hi, lo


def _dot3(a, b):
    ah, al = _split(a)
    bh, bl = _split(b)
    return (jnp.dot(ah, bh, preferred_element_type=F32) + jnp.dot(ah, bl, preferred_element_type=F32)
            + jnp.dot(al, bh, preferred_element_type=F32))


def _silu(a):
    return a * (1.0 / (1.0 + jnp.exp(-a)))


def _mm_body(*refs, n_x, n_w, n_e, n_o, groups, nk, epilogue, n_grid):
    xs = refs[:n_x]
    ws = refs[n_x:n_x + n_w]
    es = refs[n_x + n_w:n_x + n_w + n_e]
    outs = refs[n_x + n_w + n_e:n_x + n_w + n_e + n_o]
    accs = refs[n_x + n_w + n_e + n_o:]
    pids = [pl.program_id(a) for a in range(n_grid)]
    xv = [_bf(x[...]) for x in xs]
    wv = [_bf(w[...]) for w in ws]
    parts = []
    for group in groups:
        p = None
        for xi, wi in group:
            d = jnp.dot(xv[xi], wv[wi], preferred_element_type=F32)
            p = d if p is None else p + d
        parts.append(p)
    if nk == 1:
        epilogue(parts, es, outs, pids)
        return
    k = pids[-1]

    @pl.when(k == 0)
    def _():
        for a, p in zip(accs, parts):
            a[...] = p

    @pl.when(k > 0)
    def _():
        for a, p in zip(accs, parts):
            a[...] += p

    @pl.when(k == nk - 1)
    def _():
        epilogue([a[...] for a in accs], es, outs, pids)


def _mm(name, grid, xs, x_specs, ws, w_specs, es, e_specs, out_shapes, out_specs, groups, epilogue,
        nk=1, acc_block=None, aliases=None):
    body = functools.partial(_mm_body, n_x=len(xs), n_w=len(ws), n_e=len(es), n_o=len(out_shapes),
                             groups=groups, nk=nk, epilogue=epilogue, n_grid=len(grid))
    scratch = [pltpu.VMEM(acc_block, F32) for _ in groups] if nk > 1 else []
    res = pl.pallas_call(
        body, grid=grid, in_specs=list(x_specs) + list(w_specs) + list(e_specs),
        out_specs=list(out_specs), out_shape=list(out_shapes), scratch_shapes=scratch,
        input_output_aliases=aliases or {}, compiler_params=_params(len(grid)), name=name,
    )(*xs, *ws, *es)
    return res


def _store0(parts, es, outs, pids):
    outs[0][...] = parts[0].astype(outs[0].dtype)


def _linear(name, x, w, out_dtype, *, n_cols=None, tm_pref=1024, tn_pref=512):
    m, k = x.shape
    n = n_cols or w.shape[1]
    tm, tn = _tile(m, tm_pref), _tile(n, tn_pref)
    return _mm(name, (m // tm, n // tn),
               [x], [pl.BlockSpec((tm, k), lambda i, j: (i, 0))],
               [w], [pl.BlockSpec((k, tn), lambda i, j: (0, j))],
               [], [], [jax.ShapeDtypeStruct((m, n), out_dtype)],
               [pl.BlockSpec((tm, tn), lambda i, j: (i, j))], [[(0, 0)]], _store0)[0]


def _mod_body(c_ref, w_ref, b_ref, o_ref):
    o_ref[...] = _dot(_silu(c_ref[...]), w_ref[...]) + b_ref[...]


def _modulation(c_all, mod_w, mod_b):
    depth, d, n = mod_w.shape
    tn = _tile(n, 512)
    return pl.pallas_call(
        _mod_body, grid=(depth, n // tn),
        in_specs=[pl.BlockSpec((MOD_ROWS, d), lambda l, j: (0, 0)),
                  pl.BlockSpec((None, d, tn), lambda l, j: (l, 0, j)),
                  pl.BlockSpec((None, 1, tn), lambda l, j: (l, 0, j))],
        out_specs=pl.BlockSpec((None, MOD_ROWS, tn), lambda l, j: (l, 0, j)),
        out_shape=jax.ShapeDtypeStruct((depth, MOD_ROWS, n), F32),
        compiler_params=_params(2), name="modulation",
    )(c_all, mod_w, mod_b.reshape(depth, 1, n))


def _pack_bf16_pairs(y):
    n = y.shape[1] // 2
    bits = pltpu.bitcast(y.astype(BF16).astype(F32), jnp.uint32)
    return (bits[:, :n] >> 16) | (bits[:, n:] & jnp.uint32(0xFFFF0000))


def _unpack_bf16_pairs(w):
    return pltpu.bitcast(w << 16, F32), pltpu.bitcast(w & jnp.uint32(0xFFFF0000), F32)


def _norm_body(*refs, modulate, router, n_experts):
    it = iter(refs)
    x_ref, g_ref = next(it), next(it)
    x = x_ref[...]
    y = x * lax.rsqrt(jnp.mean(x * x, axis=-1, keepdims=True) + EPS) * g_ref[...]
    if modulate:
        sh_ref, sc_ref = next(it), next(it)
        y = y * (1.0 + sc_ref[...]) + sh_ref[...]
    if router:
        rw_ref, rb_ref = next(it), next(it)
    o_ref = next(it)
    if o_ref.dtype == jnp.uint32:
        o_ref[...] = _pack_bf16_pairs(y)
    else:
        o_ref[...] = y.astype(o_ref.dtype)
    if router:
        comb_ref = next(it)
        logits = _dot3(y, rw_ref[...]) + rb_ref[...]
        lane = lax.broadcasted_iota(jnp.int32, logits.shape, 1).astype(F32)
        neg = jnp.float32(-jnp.inf)
        logits = jnp.where(lane < n_experts, logits, neg)
        m1 = jnp.max(logits, axis=-1, keepdims=True)
        i1 = jnp.min(jnp.where(logits == m1, lane, float(LANES)), axis=-1, keepdims=True)
        rest = jnp.where(lane == i1, neg, logits)
        m2 = jnp.max(rest, axis=-1, keepdims=True)
        i2 = jnp.min(jnp.where(rest == m2, lane, float(LANES)), axis=-1, keepdims=True)
        e2 = jnp.exp(m2 - m1)
        g1 = 1.0 / (1.0 + e2)
        g2 = e2 / (1.0 + e2)
        comb_ref[...] = (jnp.where(lane == ROUTE_EXPERT, i1, 0.0) + jnp.where(lane == ROUTE_EXPERT + 1, i2, 0.0)
                         + jnp.where(lane == ROUTE_GATE, g1, 0.0) + jnp.where(lane == ROUTE_GATE + 1, g2, 0.0))


def _rmsnorm(name, x, gain, out_dtype, *, rows=None, width=None, col_block=0, mod=None, seg_of_tile=None,
             router=None, tr=256):
    if x.ndim == 3:
        per = x.shape[1]
        rows, width = x.shape[0] * per, x.shape[2]
        tr = _tile(per, tr)
        x_spec = pl.BlockSpec((None, tr, width), lambda i: (i // (per // tr), i % (per // tr), 0))
    else:
        rows = rows or x.shape[0]
        width = width or x.shape[1]
        tr = _tile(rows, tr)
        x_spec = pl.BlockSpec((tr, width), lambda i: (i, col_block))
    in_specs = [x_spec, pl.BlockSpec((1, width), lambda i: (0, 0))]
    args = [x, gain.reshape(1, width)]
    if mod is not None:
        seg = seg_of_tile(tr)
        for t in mod:
            in_specs.append(pl.BlockSpec((None, 1, width), lambda i: (seg(i), 0, 0)))
            args.append(t)
    out_width = width // 2 if out_dtype == jnp.uint32 else width
    out_shape = [jax.ShapeDtypeStruct((rows, out_width), out_dtype)]
    out_specs = [pl.BlockSpec((tr, out_width), lambda i: (i, 0))]
    n_experts = 0
    if router is not None:
        rw, rb = router
        n_experts = rw.shape[1]
        rw_p = jnp.pad(rw, ((0, 0), (0, LANES - n_experts)))
        rb_p = jnp.pad(rb, (0, LANES - n_experts)).reshape(1, LANES)
        in_specs += [pl.BlockSpec((width, LANES), lambda i: (0, 0)), pl.BlockSpec((1, LANES), lambda i: (0, 0))]
        args += [rw_p, rb_p]
        out_shape.append(jax.ShapeDtypeStruct((rows, LANES), F32))
        out_specs.append(pl.BlockSpec((tr, LANES), lambda i: (i, 0)))
    body = functools.partial(_norm_body, modulate=mod is not None, router=router is not None, n_experts=n_experts)
    return pl.pallas_call(body, grid=(rows // tr,), in_specs=in_specs, out_specs=out_specs, out_shape=out_shape,
                          compiler_params=_params(1), name=name)(*args)


def _group_dft_body(p_ref, cs_ref, o_ref, *, groups):
    cs = cs_ref[...]
    fw = groups * HEAD_DIM
    for g in range(groups):
        r = _dot(p_ref[:, g * HEAD_DIM:(g + 1) * HEAD_DIM], cs)
        o_ref[:, g * HEAD_DIM:(g + 1) * HEAD_DIM] = r[:, :HEAD_DIM].astype(o_ref.dtype)
        o_ref[:, fw + g * HEAD_DIM:fw + (g + 1) * HEAD_DIM] = r[:, HEAD_DIM:].astype(o_ref.dtype)


def _group_dft(proj, fw):
    rows = proj.shape[0]
    k = np.arange(HEAD_DIM)
    ang = (2.0 * np.pi / HEAD_DIM) * ((k[:, None] * k[None, :]) % HEAD_DIM)
    cs = jnp.asarray(np.concatenate([np.cos(ang), np.sin(ang)], axis=1), dtype=BF16)
    tm = _tile(rows, 512)
    body = functools.partial(_group_dft_body, groups=fw // HEAD_DIM)
    return pl.pallas_call(
        body, grid=(rows // tm,),
        in_specs=[pl.BlockSpec((tm, fw), lambda i: (i, 0)),
                  pl.BlockSpec((HEAD_DIM, 2 * HEAD_DIM), lambda i: (0, 0))],
        out_specs=pl.BlockSpec((tm, 2 * fw), lambda i: (i, 0)),
        out_shape=jax.ShapeDtypeStruct((rows, 2 * fw), BF16),
        compiler_params=_params(1), name="group_dft",
    )(proj, cs)


def _trig_body(ca_ref, sa_ref, cb_ref, sb_ref, o_ref, *, sin_sign, scale):
    ca, sa, cb, sb = ca_ref[...], sa_ref[...], cb_ref[...], sb_ref[...]
    cos = ca * cb - sa * sb
    sin = (sa * cb + ca * sb) * sin_sign
    o_ref[...] = (jnp.where(pl.program_id(0) == 0, cos, sin) * scale).astype(o_ref.dtype)


def _trig_matrix(name, phase, den, n_rows, n_cols, parts_on_rows, sin_sign=1.0, scale=1.0):
    tile = min(TRIG_TILE, n_rows)
    n_hi = n_rows // tile
    col = jnp.arange(n_cols, dtype=jnp.int32)[None, :]

    def tables(rows):
        ang = (phase(rows[:, None], col) % den).astype(F32) * (2.0 * math.pi / den)
        return jnp.cos(ang), jnp.sin(ang)

    zero = jnp.zeros((1,), jnp.int32)
    base_c, base_s = tables(zero)
    hi_c, hi_s = tables(jnp.arange(n_hi, dtype=jnp.int32) * tile)
    lo_c, lo_s = tables(jnp.arange(tile, dtype=jnp.int32))
    lo_c, lo_s = lo_c * base_c + lo_s * base_s, lo_s * base_c - lo_c * base_s
    if parts_on_rows:
        shape, out_map = (2 * n_rows, n_cols), (lambda p, h: (p * n_hi + h, 0))
    else:
        shape, out_map = (n_rows, 2 * n_cols), (lambda p, h: (h, p))
    body = functools.partial(_trig_body, sin_sign=sin_sign, scale=scale)
    hi_spec = pl.BlockSpec((None, 1, n_cols), lambda p, h: (h, 0, 0))
    lo_spec = pl.BlockSpec((tile, n_cols), lambda p, h: (0, 0))
    return pl.pallas_call(
        body, grid=(2, n_hi), in_specs=[hi_spec, hi_spec, lo_spec, lo_spec],
        out_specs=pl.BlockSpec((tile, n_cols), out_map),
        out_shape=jax.ShapeDtypeStruct(shape, BF16), compiler_params=_params(2), name=name,
    )(hi_c.reshape(n_hi, 1, n_cols), hi_s.reshape(n_hi, 1, n_cols), lo_c, lo_s)


def _position_dft(name, ab, row0, nb, length, fw):
    cs = _trig_matrix(f"{name}_matrix", lambda f, t: f * t, length, length, length, parts_on_rows=False,
                      sin_sign=-1.0, scale=1.0 / math.sqrt(length * HEAD_DIM))
    tm, tn = _tile(length, 1024), _tile(fw, 512)
    nm, nn = length // tm, fw // tn
    rb0 = row0 // length
    return _mm(name, (nb, nm, nn, 2),
               [cs], [pl.BlockSpec((tm, length), lambda b, i, j, k: (i, k))],
               [ab], [pl.BlockSpec((length, tn), lambda b, i, j, k: (rb0 + b, k * nn + j))],
               [], [], [jax.ShapeDtypeStruct((nb * length, fw), BF16)],
               [pl.BlockSpec((tm, tn), lambda b, i, j, k: (b * nm + i, j))],
               [[(0, 0)]], _store0, nk=2, acc_block=(tm, tn))[0]


def _short_conv_body(p_ref, w_ref, b_ref, o_ref):
    u = p_ref[...].astype(F32)
    n = u.shape[0]
    row = lax.broadcasted_iota(jnp.int32, u.shape, 0)
    prev = jnp.where(row == 0, 0.0, pltpu.roll(u, 1, 0))
    nxt = jnp.where(row == n - 1, 0.0, pltpu.roll(u, n - 1, 0))
    w = w_ref[...]
    o_ref[...] = (w[0:1] * prev + w[1:2] * u + w[2:3] * nxt + b_ref[...]).astype(o_ref.dtype)


def _short_conv(name, proj, col0, width, row0, nb, length, conv_w, conv_b):
    tc = _tile(math.gcd(col0, width), 512)
    cb0, rb0 = col0 // tc, row0 // length
    return pl.pallas_call(
        _short_conv_body, grid=(nb, width // tc),
        in_specs=[pl.BlockSpec((length, tc), lambda b, j: (rb0 + b, cb0 + j)),
                  pl.BlockSpec((3, tc), lambda b, j: (0, j)),
                  pl.BlockSpec((1, tc), lambda b, j: (0, j))],
        out_specs=pl.BlockSpec((length, tc), lambda b, j: (b, j)),
        out_shape=jax.ShapeDtypeStruct((nb * length, width), BF16),
        compiler_params=_params(2), name=name,
    )(proj, conv_w, conv_b.reshape(1, width))


def _filter_mlp_body(z_ref, pw_ref, pb_ref, w1_ref, b1_ref, w2_ref, b2_ref, fr_ref, o_ref):
    fr = fr_ref[...]
    h = jnp.sin(fr * (_dot3(z_ref[...], pw_ref[...]) + pb_ref[...]))
    h = jnp.sin(fr * (_dot3(h, w1_ref[...]) + b1_ref[...]))
    o_ref[...] = jnp.sin(fr * (_dot3(h, w2_ref[...]) + b2_ref[...]))


def _filter_out_body(h_ref, wf_ref, wb_ref, d_ref, o_ref):
    h = h_ref[...]
    d = d_ref[...]
    fwd = _dot3(h, wf_ref[...]) * d
    bwd = _dot3(h, wb_ref[...]) * d
    row = lax.broadcasted_iota(jnp.int32, bwd.shape, 0)
    bwd = jnp.where(row == 0, 0.0, bwd)
    o_ref[0] = fwd + bwd
    o_ref[1] = fwd - bwd


def _hyena_filter_sums(length, hw, pe_w, pe_b, w1, b1, w2, b2, w_out, freq):
    t = jnp.linspace(0.0, 1.0, length, dtype=F32)[:, None]
    w = (2.0 * math.pi / length) * jnp.arange(length, dtype=F32)[:, None]
    bands = jnp.linspace(1e-4, HY_POS_BANDS - 1, HY_POS_BANDS, dtype=F32)[None, :]
    z = jnp.concatenate([t, jnp.cos(bands * w), -jnp.sin(bands * w)], axis=-1)
    pos_dim, fwid = pe_w.shape
    z = jnp.pad(z, ((0, 0), (0, LANES - pos_dim)))
    pe_w = jnp.pad(pe_w, ((0, LANES - pos_dim), (0, 0)))
    row = lambda v: v.reshape(1, fwid)
    full = lambda a: pl.BlockSpec(a.shape, lambda i: (0,) * a.ndim)
    args = [z, pe_w, row(pe_b), w1, row(b1), w2, row(b2), row(freq)]
    h = pl.pallas_call(
        _filter_mlp_body, grid=(1,), in_specs=[full(a) for a in args],
        out_specs=pl.BlockSpec((length, fwid), lambda i: (0, 0)),
        out_shape=jax.ShapeDtypeStruct((length, fwid), F32),
        compiler_params=_params(1), name="hyena_filter_mlp",
    )(*args)
    max_decay = math.log(HY_DECAY_TARGET) / HY_FAST_DECAY
    min_decay = math.log(HY_DECAY_TARGET) / HY_SLOW_DECAY
    deltas = jnp.abs(jnp.linspace(min_decay, max_decay, hw, dtype=F32))
    decay = jnp.exp(-t * deltas)
    tc = _tile(hw, 256)
    nc = hw // tc
    return pl.pallas_call(
        _filter_out_body, grid=(HY_ORDER, nc),
        in_specs=[pl.BlockSpec((length, fwid), lambda n, j: (0, 0)),
                  pl.BlockSpec((fwid, tc), lambda n, j: (0, (2 * n) * nc + j)),
                  pl.BlockSpec((fwid, tc), lambda n, j: (0, (2 * n + 1) * nc + j)),
                  pl.BlockSpec((length, tc), lambda n, j: (0, j))],
        out_specs=pl.BlockSpec((2, length, tc), lambda n, j: (0, 0, n * nc + j)),
        out_shape=jax.ShapeDtypeStruct((2, length, HY_ORDER * hw), F32),
        compiler_params=_params(2), name="hyena_filter_out",
    )(h, w_out, w_out, decay)


def _hyena_spectrum_epilogue(parts, es, outs, pids):
    ur, us = parts
    kr, ks = es[0][0], es[0][1]
    outs[0][0] = (ur * kr - us * ks).astype(outs[0].dtype)
    outs[0][1] = (ur * ks + us * kr).astype(outs[0].dtype)


def _hyena_gate_epilogue(parts, es, outs, pids):
    gate_ref, z_ref, skip_ref = es
    gate, z = gate_ref[...].astype(F32), z_ref[...].astype(F32)
    outs[0][...] = (gate * (parts[0] + z * skip_ref[...])).astype(outs[0].dtype)


def _hyena_stream(tag, uc, nb, length, hw, filt, skip):
    fo_b = _trig_matrix(f"hyena_dft_{tag}", lambda f, t: (2 * f + 1) * t, 4 * length, length, length,
                        parts_on_rows=True)
    go_b = _trig_matrix(f"hyena_idft_{tag}", lambda t, f: (2 * f + 1) * t, 4 * length, length, length,
                        parts_on_rows=False, scale=1.0 / length)
    tm, tn = _tile(length, 1024), _tile(hw, 512)
    nm, nn = length // tm, hw // tn
    kspec = _mm(f"hyena_kspec_{tag}", (2 * nm, HY_ORDER * nn),
                [fo_b], [pl.BlockSpec((tm, length), lambda i, j: (i, 0))],
                [filt.reshape(2 * length, HY_ORDER * hw)],
                [pl.BlockSpec((length, tn), lambda i, j: (i // nm, j))],
                [], [], [jax.ShapeDtypeStruct((2 * length, HY_ORDER * hw), F32)],
                [pl.BlockSpec((tm, tn), lambda i, j: (i, j))], [[(0, 0)]], _store0)[0]
    kspec = kspec.reshape(2, length, HY_ORDER * hw)
    z = uc
    for n in range(HY_ORDER):
        z = _hyena_order(tag, n, uc, z, nb, length, hw, fo_b, go_b, kspec, skip, (tm, tn, nm, nn))
    return z


def _hyena_order(tag, n, uc, z, nb, length, hw, fo_b, go_b, kspec, skip, tiles):
    tm, tn, nm, nn = tiles
    gate_cb = (n + 1) * nn
    spec = _mm(f"hyena_fwd{n}_{tag}", (nm, nb, nn),
               [fo_b, fo_b], [pl.BlockSpec((tm, length), lambda i, b, j: (i, 0)),
                              pl.BlockSpec((tm, length), lambda i, b, j: (nm + i, 0))],
               [z], [pl.BlockSpec((length, tn), lambda i, b, j: (b, j))],
               [kspec], [pl.BlockSpec((2, tm, tn), lambda i, b, j: (0, i, n * nn + j))],
               [jax.ShapeDtypeStruct((nb, 2, length, hw), BF16)],
               [pl.BlockSpec((None, 2, tm, tn), lambda i, b, j: (b, 0, i, j))],
               [[(0, 0)], [(1, 0)]], _hyena_spectrum_epilogue)[0]
    return _mm(f"hyena_inv{n}_{tag}", (nb, nm, nn),
               [go_b], [pl.BlockSpec((tm, 2 * length), lambda b, i, j: (i, 0))],
               [spec.reshape(nb * 2 * length, hw)],
               [pl.BlockSpec((2 * length, tn), lambda b, i, j: (b, j))],
               [uc, z, skip.reshape(HY_ORDER, 1, hw)],
               [pl.BlockSpec((tm, tn), lambda b, i, j: (b * nm + i, gate_cb + j)),
                pl.BlockSpec((tm, tn), lambda b, i, j: (b * nm + i, j)),
                pl.BlockSpec((None, 1, tn), lambda b, i, j: (n, 0, j))],
               [jax.ShapeDtypeStruct((nb * length, hw), BF16)],
               [pl.BlockSpec((tm, tn), lambda b, i, j: (b * nm + i, j))],
               [[(0, 0)]], _hyena_gate_epilogue)[0]


def _resgate_epilogue(parts, es, outs, pids):
    res_ref, gate_ref = es[0], es[1]
    outs[0][...] = res_ref[...] + gate_ref[...] * parts[0]


def _mixer_out(name, res, za, zb, w_out, gate, seg_of_tile, in_place):
    rows, half = za.shape
    d = res.shape[-1]
    tm, tn = _tile(rows, 1024), _tile(d, 512)
    if res.ndim == 3:
        assert not in_place
        tm = _tile(res.shape[1], 1024)
        per = res.shape[1] // tm
        res_spec = pl.BlockSpec((None, tm, tn), lambda i, j: (i // per, i % per, j))
    else:
        res_spec = pl.BlockSpec((tm, tn), lambda i, j: (i, j))
    seg = seg_of_tile(tm)
    return _mm(name, (rows // tm, d // tn),
               [za, zb], [pl.BlockSpec((tm, half), lambda i, j: (i, 0))] * 2,
               [w_out, w_out], [pl.BlockSpec((half, tn), lambda i, j: (0, j)),
                                pl.BlockSpec((half, tn), lambda i, j: (1, j))],
               [res, gate], [res_spec, pl.BlockSpec((None, 1, tn), lambda i, j: (seg(i), 0, j))],
               [jax.ShapeDtypeStruct((rows, d), F32)],
               [pl.BlockSpec((tm, tn), lambda i, j: (i, j))],
               [[(0, 0), (1, 1)]], _resgate_epilogue, aliases={4: 0} if in_place else {})[0]


def _swiglu_epilogue(parts, es, outs, pids):
    outs[0][...] = (_silu(parts[0]) * parts[1]).astype(outs[0].dtype)


def _ffn_up(name, h, w1, w3):
    m, k = h.shape
    n = w1.shape[1]
    tm, tn = _tile(m, 1024), _tile(n, 256)
    return _mm(name, (m // tm, n // tn),
               [h], [pl.BlockSpec((tm, k), lambda i, j: (i, 0))],
               [w1, w3], [pl.BlockSpec((k, tn), lambda i, j: (0, j))] * 2,
               [], [], [jax.ShapeDtypeStruct((m, n), BF16)],
               [pl.BlockSpec((tm, tn), lambda i, j: (i, j))],
               [[(0, 0)], [(0, 1)]], _swiglu_epilogue)[0]


def _down_residual(name, x_res, rows, u, w2, gate, seg_of_tile, tm_pref, tk_pref, tn_pref):
    d = x_res.shape[1]
    kdim = u.shape[1]
    tm, tn, tk = _tile(rows, tm_pref), _tile(d, tn_pref), _tile(kdim, tk_pref)
    nk = kdim // tk
    seg = seg_of_tile(tm)
    return _mm(name, (rows // tm, d // tn, nk),
               [u], [pl.BlockSpec((tm, tk), lambda i, j, k: (i, k))],
               [w2], [pl.BlockSpec((tk, tn), lambda i, j, k: (k, j))],
               [x_res, gate], [pl.BlockSpec((tm, tn), lambda i, j, k: (i, j)),
                               pl.BlockSpec((None, 1, tn), lambda i, j, k: (seg(i), 0, j))],
               [jax.ShapeDtypeStruct(x_res.shape, F32)],
               [pl.BlockSpec((tm, tn), lambda i, j, k: (i, j))],
               [[(0, 0)]], _resgate_epilogue, nk=nk, acc_block=(tm, tn), aliases={2: 0})[0]


def _moe_plan(route, n_experts):
    t = route.shape[0]
    n_tiles = TOP_K * t // MOE_TILE + n_experts
    expert = route[:, ROUTE_EXPERT:ROUTE_EXPERT + TOP_K].astype(jnp.int32).T.reshape(-1)
    onehot = (expert[:, None] == jnp.arange(n_experts, dtype=jnp.int32)[None, :]).astype(jnp.int32)
    running = jnp.cumsum(onehot, axis=0)
    rank = jnp.take_along_axis(running, expert[:, None], axis=1)[:, 0] - 1
    tiles_per_expert = (running[-1] + MOE_TILE - 1) // MOE_TILE
    tile_end = jnp.cumsum(tiles_per_expert)
    row = (tile_end - tiles_per_expert)[expert] * MOE_TILE + rank
    token = jnp.tile(jnp.arange(t, dtype=jnp.int32), TOP_K)
    row_token = jnp.zeros((n_tiles * MOE_TILE,), jnp.int32).at[row].set(token)
    tile_expert = jnp.searchsorted(tile_end, jnp.arange(n_tiles, dtype=jnp.int32), side="right")
    tile_expert = jnp.minimum(tile_expert, n_experts - 1).astype(jnp.int32)
    pos = row.reshape(TOP_K, t // MOE_COMBINE_TILE, MOE_COMBINE_TILE).transpose(1, 0, 2)
    return row_token.reshape(n_tiles, 1, MOE_TILE), tile_expert, tile_end[-1:].astype(jnp.int32), pos


def _row_copy(src_hbm, row, dst_ref, dst_row, sem):
    return pltpu.make_async_copy(src_hbm.at[pl.ds(row, 1)], dst_ref.at[pl.ds(dst_row, 1)], sem)


def _wait_rows(src_hbm, buf_ref, sem):
    pltpu.make_async_copy(src_hbm.at[pl.ds(0, buf_ref.shape[0])], buf_ref, sem).wait()


def _moe_gather_body(na_ref, tok_ref, src_hbm, o_ref, buf_ref, sem):
    rows = buf_ref.shape[0]
    active = pl.program_id(0) < na_ref[0]

    @pl.when(active)
    def _():
        def issue(r, carry):
            _row_copy(src_hbm, tok_ref[0, r], buf_ref, r, sem).start()
            return carry

        lax.fori_loop(0, rows, issue, 0, unroll=DMA_ISSUE_UNROLL)
        _wait_rows(src_hbm, buf_ref, sem)
        n = buf_ref.shape[1]
        lo, hi = _unpack_bf16_pairs(buf_ref[...])
        o_ref[:, :n] = lo.astype(o_ref.dtype)
        o_ref[:, n:] = hi.astype(o_ref.dtype)

    @pl.when(jnp.logical_not(active))
    def _():
        o_ref[...] = jnp.zeros(o_ref.shape, o_ref.dtype)


def _dma_params(n_grid):
    return pltpu.CompilerParams(dimension_semantics=("arbitrary",) * n_grid, vmem_limit_bytes=V7X_VMEM_LIMIT,
                                disable_bounds_checks=True)


def _moe_gather(h, row_token, n_active):
    n_tiles = row_token.shape[0]
    d = 2 * h.shape[1]
    return pl.pallas_call(
        _moe_gather_body, grid=(n_tiles,),
        in_specs=[pl.BlockSpec(memory_space=pltpu.SMEM),
                  pl.BlockSpec((None, 1, MOE_TILE), lambda i: (i, 0, 0), memory_space=pltpu.SMEM),
                  pl.BlockSpec(memory_space=pl.ANY)],
        out_specs=pl.BlockSpec((MOE_TILE, d), lambda i: (i, 0)),
        out_shape=jax.ShapeDtypeStruct((n_tiles * MOE_TILE, d), BF16),
        scratch_shapes=[pltpu.VMEM((MOE_TILE,) + h.shape[1:], h.dtype), pltpu.SemaphoreType.DMA(())],
        compiler_params=_dma_params(1), name="moe_gather",
    )(n_active, row_token, h)


def _moe_up_body(te_ref, na_ref, x_ref, w1_ref, w3_ref, o_ref):
    active = pl.program_id(1) < na_ref[0]

    @pl.when(active)
    def _():
        x = x_ref[...]
        o_ref[...] = (_silu(_dot(x, w1_ref[...])) * _dot(x, w3_ref[...])).astype(o_ref.dtype)

    @pl.when(jnp.logical_not(active))
    def _():
        o_ref[...] = jnp.zeros(o_ref.shape, o_ref.dtype)


def _moe_down_body(te_ref, na_ref, u_ref, w2_ref, o_ref):
    active = pl.program_id(1) < na_ref[0]

    @pl.when(active)
    def _():
        o_ref[...] = _pack_bf16_pairs(_dot(u_ref[...], w2_ref[...]))

    @pl.when(jnp.logical_not(active))
    def _():
        o_ref[...] = jnp.zeros(o_ref.shape, o_ref.dtype)


def _moe_experts(xs, tile_expert, n_active, w1, w3, w2):
    ne, d, f = w1.shape
    n_tiles = xs.shape[0] // MOE_TILE
    tn = _tile(f, 512)
    up = pl.pallas_call(
        _moe_up_body,
        grid_spec=pltpu.PrefetchScalarGridSpec(
            num_scalar_prefetch=2, grid=(f // tn, n_tiles),
            in_specs=[pl.BlockSpec((MOE_TILE, d), lambda j, i, te, na: (i, 0)),
                      pl.BlockSpec((None, d, tn), lambda j, i, te, na: (te[i], 0, j)),
                      pl.BlockSpec((None, d, tn), lambda j, i, te, na: (te[i], 0, j))],
            out_specs=pl.BlockSpec((MOE_TILE, tn), lambda j, i, te, na: (i, j))),
        out_shape=jax.ShapeDtypeStruct((xs.shape[0], f), BF16),
        compiler_params=_params(2), name="moe_up",
    )(tile_expert, n_active, xs, w1, w3)
    tn = _tile(d, 1024)
    return pl.pallas_call(
        _moe_down_body,
        grid_spec=pltpu.PrefetchScalarGridSpec(
            num_scalar_prefetch=2, grid=(d // tn, n_tiles),
            in_specs=[pl.BlockSpec((MOE_TILE, f), lambda j, i, te, na: (i, 0)),
                      pl.BlockSpec((None, f, tn), lambda j, i, te, na: (te[i], 0, j))],
            out_specs=pl.BlockSpec((MOE_TILE, tn // 2), lambda j, i, te, na: (i, j))),
        out_shape=jax.ShapeDtypeStruct((xs.shape[0], d // 2), jnp.uint32),
        compiler_params=_params(2), name="moe_down",
    )(tile_expert, n_active, up, w2), tn


def _moe_combine_body(pos_ref, ys_hbm, res_ref, route_ref, gate_ref, fg_ref, o_ref, buf_ref, sem, *, pack_tile):
    rows = res_ref.shape[0]

    def issue(r, carry):
        for k in range(TOP_K):
            _row_copy(ys_hbm, pos_ref[k, r], buf_ref.at[k], r, sem).start()
        return carry

    lax.fori_loop(0, rows, issue, 0, unroll=DMA_ISSUE_UNROLL)
    for k in range(TOP_K):
        _wait_rows(ys_hbm, buf_ref.at[k], sem)
    route = route_ref[...]
    lane = lax.broadcasted_iota(jnp.int32, route.shape, 1)
    gates = [jnp.sum(jnp.where(lane == ROUTE_GATE + k, route, 0.0), axis=-1, keepdims=True) for k in range(TOP_K)]
    half = pack_tile // 2
    for j in range(o_ref.shape[1] // pack_tile):
        halves = [_unpack_bf16_pairs(buf_ref[k, :, j * half:(j + 1) * half]) for k in range(TOP_K)]
        for part in range(2):
            cols = slice(j * pack_tile + part * half, j * pack_tile + (part + 1) * half)
            moe = None
            for k in range(TOP_K):
                term = gates[k] * halves[k][part]
                moe = term if moe is None else moe + term
            o_ref[:, cols] = res_ref[:, cols] + gate_ref[:, cols] * moe
    xf = o_ref[...]
    o_ref[...] = xf * lax.rsqrt(jnp.mean(xf * xf, axis=-1, keepdims=True) + EPS) * fg_ref[...]


def _moe_combine(x_res, rows, ys, pack_tile, pos, route, gate, seg_of_tile, final_gain):
    d = x_res.shape[1]
    tm = MOE_COMBINE_TILE
    seg = seg_of_tile(tm)
    return pl.pallas_call(
        functools.partial(_moe_combine_body, pack_tile=pack_tile), grid=(rows // tm,),
        in_specs=[pl.BlockSpec((None, TOP_K, tm), lambda i: (i, 0, 0), memory_space=pltpu.SMEM),
                  pl.BlockSpec(memory_space=pl.ANY),
                  pl.BlockSpec((tm, d), lambda i: (i, 0)),
                  pl.BlockSpec((tm, LANES), lambda i: (i, 0)),
                  pl.BlockSpec((None, 1, d), lambda i: (seg(i), 0, 0)),
                  pl.BlockSpec((1, d), lambda i: (0, 0))],
        out_specs=pl.BlockSpec((tm, d), lambda i: (i, 0)),
        out_shape=jax.ShapeDtypeStruct((rows, d), F32),
        scratch_shapes=[pltpu.VMEM((TOP_K, tm) + ys.shape[1:], ys.dtype), pltpu.SemaphoreType.DMA(())],
        compiler_params=_dma_params(1), name="moe_combine_norm",
    )(pos, ys, x_res, route, gate, final_gain.reshape(1, d))


NA_BLOCK_ROWS = 4


def _na_body(q_ref, k_ref, v_ref, kc_ref, vc_ref, band_ref, o_ref, va_ref, bias_ref, *, seq, ctx_len, tables,
             blocks, win):
    qscale = HEAD_DIM ** -0.5 * math.log2(math.e)
    bq = NA_BLOCK_ROWS * GRID_W

    @pl.when(pl.program_id(1) == 0)
    def _():
        for case, table in enumerate(tables):
            for ri, row in enumerate(table):
                for rj, entry in enumerate(row):
                    bias_ref[case, ri * GRID_W:(ri + 1) * GRID_W, rj * GRID_W:(rj + 1) * GRID_W] = band_ref[int(entry)]

    va_ref[0:seq, 0:HEAD_DIM] = v_ref[...].astype(BF16)
    va_ref[seq:seq + ctx_len, 0:HEAD_DIM] = vc_ref[...].astype(BF16)
    lane = lax.broadcasted_iota(jnp.int32, (seq + ctx_len, va_ref.shape[1] - HEAD_DIM), 1)
    va_ref[:, HEAD_DIM:] = jnp.where(lane == 0, 1.0, 0.0).astype(BF16)
    kc = kc_ref[...].astype(BF16)
    for kb, (case, ws) in enumerate(blocks):
        q = (q_ref[kb * bq:(kb + 1) * bq, :] * qscale).astype(BF16)
        k0 = ws * GRID_W
        s_loc = _dot_nt(q, k_ref[k0:k0 + win, :]) + bias_ref[case]
        s_ctx = _dot_nt(q, kc)
        m = jnp.maximum(jnp.max(s_loc, axis=-1, keepdims=True), jnp.max(s_ctx, axis=-1, keepdims=True))
        acc = _dot(jnp.exp2(s_loc - m), va_ref[k0:k0 + win, :]) + _dot(jnp.exp2(s_ctx - m), va_ref[seq:seq + ctx_len, :])
        o_ref[kb * bq:(kb + 1) * bq, :] = (acc[:, :HEAD_DIM] / acc[:, HEAD_DIM:HEAD_DIM + 1]).astype(o_ref.dtype)


def _na_bias_slabs(rpb, grid_rows, win_rows):
    heads = rpb.shape[0]
    cols = np.arange(GRID_W)
    start = np.clip(cols - NA_WIN_C // 2, 0, GRID_W - NA_WIN_C)
    kc = cols[None, :]
    inside = (kc >= start[:, None]) & (kc < start[:, None] + NA_WIN_C)
    dc = np.clip(kc - cols[:, None] + (NA_WIN_C - 1), 0, 2 * NA_WIN_C - 2)
    band = jnp.where(jnp.asarray(inside)[None, None], rpb[:, :, dc], NEG_BIG)
    masked = band.shape[1]
    band = jnp.concatenate([band, jnp.full((heads, 1, GRID_W, GRID_W), NEG_BIG, F32)], axis=1)
    win = min(win_rows + NA_BLOCK_ROWS, grid_rows)
    tables, blocks = [], []
    for r0 in range(0, grid_rows, NA_BLOCK_ROWS):
        ws = int(np.clip(r0 - win_rows // 2, 0, grid_rows - win))
        idx = np.full((NA_BLOCK_ROWS, win), masked)
        for ri in range(NA_BLOCK_ROWS):
            r = r0 + ri
            rs = int(np.clip(r - win_rows // 2, 0, grid_rows - win_rows))
            assert ws <= rs and rs + win_rows <= ws + win
            for key_row in range(rs, rs + win_rows):
                idx[ri, key_row - ws] = key_row - r + (NA_WIN_R - 1)
        for case, t in enumerate(tables):
            if np.array_equal(t, idx):
                break
        else:
            case = len(tables)
            tables.append(idx)
        blocks.append((case, ws))
    return band * math.log2(math.e), tuple(tuple(map(tuple, t)) for t in tables), tuple(blocks), win * GRID_W


def _neighbourhood_attention(proj, proj_ctx, nb, seq, ctx_len, heads, rpb):
    grid_rows = seq // GRID_W
    assert grid_rows % NA_BLOCK_ROWS == 0
    win_rows = min(NA_WIN_R, grid_rows)
    band, tables, blocks, win = _na_bias_slabs(rpb, grid_rows, win_rows)
    body = functools.partial(_na_body, seq=seq, ctx_len=ctx_len, tables=tables, blocks=blocks, win=win)
    return pl.pallas_call(
        body, grid=(heads, nb),
        in_specs=[pl.BlockSpec((seq, HEAD_DIM), lambda h, b: (b, h)),
                  pl.BlockSpec((seq, HEAD_DIM), lambda h, b: (b, heads + h)),
                  pl.BlockSpec((seq, HEAD_DIM), lambda h, b: (b, 2 * heads + h)),
                  pl.BlockSpec((ctx_len, HEAD_DIM), lambda h, b: (b, heads + h)),
                  pl.BlockSpec((ctx_len, HEAD_DIM), lambda h, b: (b, 2 * heads + h)),
                  pl.BlockSpec((None,) + band.shape[1:], lambda h, b: (h, 0, 0, 0))],
        out_specs=pl.BlockSpec((seq, HEAD_DIM), lambda h, b: (b, h)),
        out_shape=jax.ShapeDtypeStruct((nb * seq, heads * HEAD_DIM), BF16),
        scratch_shapes=[pltpu.VMEM((seq + ctx_len, HEAD_DIM + LANES), BF16),
                        pltpu.VMEM((len(tables), NA_BLOCK_ROWS * GRID_W, win), F32)],
        compiler_params=_params(2), name="neighbourhood_attention",
    )(proj, proj, proj, proj_ctx, proj_ctx, band)


def _mla_body(q_ref, kvl_ref, kvc_ref, krl_ref, krc_ref, cq_ref, sq_ref, ck_ref, sk_ref, o_ref,
              qf_ref, kf_ref, va_ref, *, nope, vdim, seq, ctx_len, chunks):
    qscale = (nope + MLA_ROPE) ** -0.5 * math.log2(math.e)
    rope_end = nope + MLA_ROPE

    @pl.when(pl.program_id(2) == 0)
    def _():
        krl = krl_ref[...]
        kf_ref[0:seq, 0:nope] = kvl_ref[:, 0:nope].astype(BF16)
        kf_ref[0:seq, nope:rope_end] = (krl[:, :MLA_ROPE] * ck_ref[...] + krl[:, MLA_ROPE:] * sk_ref[...]).astype(BF16)
        kf_ref[seq:seq + ctx_len, 0:nope] = kvc_ref[:, 0:nope].astype(BF16)
        kf_ref[seq:seq + ctx_len, nope:rope_end] = krc_ref[:, 0:MLA_ROPE].astype(BF16)
        kf_ref[:, rope_end:] = jnp.zeros((seq + ctx_len, kf_ref.shape[1] - rope_end), BF16)
        va_ref[0:seq, 0:vdim] = kvl_ref[:, nope:nope + vdim].astype(BF16)
        va_ref[seq:seq + ctx_len, 0:vdim] = kvc_ref[:, nope:nope + vdim].astype(BF16)
        lane = lax.broadcasted_iota(jnp.int32, (seq + ctx_len, va_ref.shape[1] - vdim), 1)
        va_ref[:, vdim:] = jnp.where(lane == 0, 1.0, 0.0).astype(BF16)

    q = q_ref[...]
    qf_ref[:, 0:nope] = (q[:, :nope] * qscale).astype(BF16)
    qr = q[:, nope:rope_end] * cq_ref[...] + q[:, rope_end:] * sq_ref[...]
    qf_ref[:, nope:rope_end] = (qr * qscale).astype(BF16)
    qf_ref[:, rope_end:] = jnp.zeros((q.shape[0], qf_ref.shape[1] - rope_end), BF16)
    qf = qf_ref[...]
    m = jnp.full((q.shape[0], 1), -jnp.inf, F32)
    acc = jnp.zeros((q.shape[0], va_ref.shape[1]), F32)
    for start, size in chunks:
        s = _dot_nt(qf, kf_ref[start:start + size, :])
        m_new = jnp.maximum(m, jnp.max(s, axis=-1, keepdims=True))
        p = jnp.exp2(s - m_new)
        acc = acc * jnp.exp2(m - m_new) + _dot(p, va_ref[start:start + size, :])
        m = m_new
    o_ref[...] = (acc[:, :vdim] / acc[:, vdim:vdim + 1]).astype(o_ref.dtype)


def _rope_tables(seq):
    pos = np.arange(seq)
    row = (pos // GRID_W).astype(np.float32)
    col = (pos % GRID_W).astype(np.float32)
    half = MLA_ROPE // 2
    inv = jnp.asarray(ROPE_BASE, F32) ** (-jnp.arange(0, half, 2, dtype=F32) / half)
    ang_r = jnp.asarray(row)[:, None] * inv
    ang_c = jnp.asarray(col)[:, None] * inv
    ang = jnp.concatenate([ang_r, ang_r, ang_c, ang_c], axis=-1)
    return jnp.cos(ang), jnp.sin(ang)


def _rotate_partner_columns(w):
    q = MLA_ROPE // 4
    parts = [w[:, i * q:(i + 1) * q] for i in range(4)]
    return jnp.concatenate([-parts[1], parts[0], -parts[3], parts[2]], axis=1)


def _latent_attention(q, kv, kv_ctx, kr, kr_ctx, nb, seq, ctx_len, heads, nope, vdim):
    cos, sin = _rope_tables(seq)
    tq = _tile(seq, 1024)
    nq = seq // tq
    qw = nope + 2 * MLA_ROPE
    keys = seq + ctx_len
    n_chunks = -(-keys // 1024)
    while keys % (n_chunks * LANES):
        n_chunks += 1
    chunk = keys // n_chunks
    chunks = tuple((s, chunk) for s in range(0, keys, chunk))
    kw = nope + 2 * MLA_ROPE
    vw = vdim + LANES
    body = functools.partial(_mla_body, nope=nope, vdim=vdim, seq=seq, ctx_len=ctx_len, chunks=chunks)
    return pl.pallas_call(
        body, grid=(nb, heads, nq),
        scratch_shapes=[pltpu.VMEM((tq, kw), BF16), pltpu.VMEM((keys, kw), BF16), pltpu.VMEM((keys, vw), BF16)],
        in_specs=[pl.BlockSpec((tq, qw), lambda b, h, i: (b * nq + i, h)),
                  pl.BlockSpec((seq, nope + vdim), lambda b, h, i: (b, h)),
                  pl.BlockSpec((ctx_len, nope + vdim), lambda b, h, i: (b, h)),
                  pl.BlockSpec((seq, 2 * MLA_ROPE), lambda b, h, i: (b, 0)),
                  pl.BlockSpec((ctx_len, 2 * MLA_ROPE), lambda b, h, i: (b, 0)),
                  pl.BlockSpec((tq, MLA_ROPE), lambda b, h, i: (i, 0)),
                  pl.BlockSpec((tq, MLA_ROPE), lambda b, h, i: (i, 0)),
                  pl.BlockSpec((seq, MLA_ROPE), lambda b, h, i: (0, 0)),
                  pl.BlockSpec((seq, MLA_ROPE), lambda b, h, i: (0, 0))],
        out_specs=pl.BlockSpec((tq, vdim), lambda b, h, i: (b * nq + i, h)),
        out_shape=jax.ShapeDtypeStruct((nb * seq, heads * vdim), BF16),
        compiler_params=_params(3), name="latent_attention",
    )(q, kv, kv_ctx, kr, kr_ctx, cos, sin, cos, sin)


def kernel(x, c, ctx, c_ctx, mod_w, mod_b, norm_mix_g, norm_ffn_g, ev_w_in, ev_w_out, hy_conv_w, hy_conv_b, hy_pe_w, hy_pe_b, hy_w1, hy_b1, hy_w2, hy_b2, hy_w_out, hy_freq, hy_skip, ffn_w1, ffn_w3, ffn_w2, od_w_in, od_w_out, na_rpb, mla_q_g, mla_w_uq, mla_kv_g, mla_w_ukv, moe_router_w, moe_router_b, moe_w1, moe_w3, moe_w2, final_g):
    nb, seq, d = x.shape
    ctx_len = ctx.shape[1]
    depth = mod_w.shape[0]
    assert depth == 2 and ev_w_in.shape[0] == 1 and od_w_in.shape[0] == 1, "even layer then odd (last) layer"
    lat_rows, ctx_rows = nb * seq, nb * ctx_len
    assert nb + 1 <= MOD_ROWS

    def lat_seg(tile_rows):
        assert seq % tile_rows == 0
        return lambda i: (i * tile_rows) // seq

    def ctx_seg(tile_rows):
        return lambda i: nb

    c_all = jnp.concatenate([c, c_ctx[None, :], jnp.zeros((MOD_ROWS - nb - 1, d), F32)], axis=0)
    mod = _modulation(c_all, mod_w, mod_b).reshape(depth, MOD_ROWS, N_MOD, 1, d)
    mods = [[mod[i, :, m] for m in range(N_MOD)] for i in range(depth)]

    sh1, sc1, g1, sh2, sc2, g2 = mods[0]
    fw = (ev_w_out.shape[1]) // 2
    hw = ev_w_out.shape[1] - fw
    filt_args = (hy_pe_w[0], hy_pe_b[0], hy_w1[0], hy_b1[0], hy_w2[0], hy_b2[0], hy_w_out[0], hy_freq[0])
    w2_b = ffn_w2[0].astype(BF16)
    streams = {}
    for tag, src, length, seg_fn in (("lat", x, seq, lat_seg), ("ctx", ctx.reshape(ctx_rows, d), ctx_len, ctx_seg)):
        h = _rmsnorm(f"norm_mix0_{tag}", src, norm_mix_g[0], BF16, mod=(sh1, sc1), seg_of_tile=seg_fn)[0]
        proj = _linear(f"even_in_{tag}", h, ev_w_in[0], BF16)
        ab = _group_dft(proj, fw)
        zf = _position_dft(f"fourier_{tag}", ab, 0, nb, length, fw)
        uc = _short_conv(f"short_conv_{tag}", proj, fw, (HY_ORDER + 1) * hw, 0, nb, length,
                         hy_conv_w[0], hy_conv_b[0])
        filt = _hyena_filter_sums(length, hw, *filt_args)
        zh = _hyena_stream(tag, uc, nb, length, hw, filt, hy_skip[0])
        xs_ = _mixer_out(f"even_out_{tag}", src, zf, zh, ev_w_out[0], g1, seg_fn, in_place=False)
        h2 = _rmsnorm(f"norm_ffn0_{tag}", xs_, norm_ffn_g[0], BF16, mod=(sh2, sc2), seg_of_tile=seg_fn)[0]
        u = _ffn_up(f"ffn_up_{tag}", h2, ffn_w1[0], ffn_w3[0])
        streams[tag] = _down_residual(f"ffn_down_{tag}", xs_, xs_.shape[0], u, w2_b, g2, seg_fn,
                                      tm_pref=512, tk_pref=u.shape[1], tn_pref=512)
    xl, xc = streams["lat"], streams["ctx"]

    sh1, sc1, g1, sh2, sc2, g2 = mods[1]
    w_in = od_w_in[0]
    na_w = od_w_out.shape[1] // 2
    heads = na_w // HEAD_DIM
    q_rank = mla_w_uq.shape[1]
    kv_rank = mla_w_ukv.shape[1]
    mla_heads = (od_w_out.shape[1] - na_w) // HEAD_DIM
    nope = mla_w_uq.shape[2] // mla_heads - MLA_ROPE
    vdim = mla_w_ukv.shape[2] // mla_heads - nope
    main_cols = 3 * na_w + q_rank + kv_rank
    w_kr = w_in[:, main_cols:main_cols + MLA_ROPE]
    w_kr = jnp.concatenate([w_kr, _rotate_partner_columns(w_kr)], axis=1)
    kv_col_block = (3 * na_w + q_rank) // kv_rank
    proj, kr, kv = {}, {}, {}
    for tag, xs_, seg_fn in (("lat", xl, lat_seg), ("ctx", xc, ctx_seg)):
        h = _rmsnorm(f"norm_mix1_{tag}", xs_, norm_mix_g[1], BF16, mod=(sh1, sc1), seg_of_tile=seg_fn)[0]
        proj[tag] = _linear(f"odd_in_{tag}", h, w_in, F32, n_cols=main_cols)
        kr[tag] = _linear(f"odd_in_rope_{tag}", h, w_kr, F32)
        ckv = _rmsnorm(f"norm_kv_{tag}", proj[tag], mla_kv_g[0], BF16, width=kv_rank, col_block=kv_col_block)[0]
        kv[tag] = _linear(f"mla_kv_{tag}", ckv, mla_w_ukv[0], BF16, tn_pref=2048)
    cq = _rmsnorm("norm_q", proj["lat"], mla_q_g[0], BF16, width=q_rank, col_block=3 * na_w // q_rank)[0]
    w_uq = mla_w_uq[0].reshape(q_rank, mla_heads, nope + MLA_ROPE)
    w_uq = jnp.concatenate([w_uq, jnp.stack([_rotate_partner_columns(w_uq[:, hh, nope:])
                                             for hh in range(mla_heads)], axis=1)], axis=-1)
    q = _linear("mla_q", cq, w_uq.reshape(q_rank, mla_heads * (nope + 2 * MLA_ROPE)), F32, tn_pref=2048)
    z_na = _neighbourhood_attention(proj["lat"], proj["ctx"], nb, seq, ctx_len, heads, na_rpb[0])
    z_mla = _latent_attention(q, kv["lat"], kv["ctx"], kr["lat"], kr["ctx"], nb, seq, ctx_len, mla_heads, nope, vdim)
    xl = _mixer_out("odd_out", xl, z_na, z_mla, od_w_out[0], g1, lat_seg, in_place=True)
    h2, route = _rmsnorm("norm_ffn1", xl, norm_ffn_g[1], jnp.uint32, mod=(sh2, sc2), seg_of_tile=lat_seg,
                         router=(moe_router_w[0], moe_router_b[0]))
    assert lat_rows % MOE_TILE == 0 and lat_rows % MOE_COMBINE_TILE == 0
    row_token, tile_expert, n_active, pos = _moe_plan(route, moe_w1.shape[1])
    xs = _moe_gather(h2, row_token, n_active)
    ys, pack_tile = _moe_experts(xs, tile_expert, n_active, moe_w1[0], moe_w3[0], moe_w2[0])
    out = _moe_combine(xl, lat_rows, ys, pack_tile, pos, route, g2, lat_seg, final_g)
    return out.reshape(nb, seq, d)
```
